```python
import math
import jax, jax.numpy as jnp
from jax import lax
import numpy as np

D_MODEL = 2048
BATCH = 4
SEQ = 2048
DEPTH = 4

N_DIFF_HEADS = 8
DIFF_HEAD_DIM = 64
DIFF_WIDTH = N_DIFF_HEADS * 2 * DIFF_HEAD_DIM
N_GDN_HEADS = 8
GDN_HEAD_DIM = 128
GDN_WIDTH = N_GDN_HEADS * GDN_HEAD_DIM
CONV_WIDTH = 5
CHUNK = 64
Q_BLOCK = 128
FFN_HIDDEN = ((8 * D_MODEL // 3 + 255) // 256) * 256
ROPE_THETA = 10000.0
NORM_EPS = 1e-6
N_MOD = 6
IN_WIDTHS = (DIFF_WIDTH, DIFF_WIDTH, DIFF_WIDTH,
             GDN_WIDTH, GDN_WIDTH, GDN_WIDTH, GDN_WIDTH,
             2 * N_GDN_HEADS, 2 * N_GDN_HEADS,
             2 * D_MODEL)
IN_COLS = 3 * DIFF_WIDTH + 4 * GDN_WIDTH + 4 * N_GDN_HEADS + 2 * D_MODEL

kernel_name = 'hybrid_diffattn_gdn_encoder'


def _split(t, widths):
    out, start = [], 0
    for w in widths:
        out.append(t[..., start:start + w])
        start += w
    return out


def rms_norm(x, w):
    xf = x.astype(jnp.float32)
    y = xf * lax.rsqrt(jnp.mean(xf * xf, axis=-1, keepdims=True) + NORM_EPS)
    return (y * w.astype(jnp.float32)).astype(x.dtype)


def l2_norm(xf):
    return xf * lax.rsqrt(jnp.sum(xf * xf, axis=-1, keepdims=True) + NORM_EPS)


def rope_tables(positions):
    half = DIFF_HEAD_DIM // 2
    inv_freq = ROPE_THETA ** (-jnp.arange(half, dtype=jnp.float32) * 2.0 / DIFF_HEAD_DIM)
    ang = positions.astype(jnp.float32)[..., None] * inv_freq
    return jnp.cos(ang)[:, :, None, None, :], jnp.sin(ang)[:, :, None, None, :]


def apply_rope(x, cos, sin):
    half = DIFF_HEAD_DIM // 2
    cos = cos.astype(x.dtype)
    sin = sin.astype(x.dtype)
    x1, x2 = x[..., :half], x[..., half:]
    return jnp.concatenate([x1 * cos - x2 * sin, x2 * cos + x1 * sin], axis=-1)


def diff_attention(q, k, v, cos, sin, qn_w, kn_w, lam_vecs, subln_w, lambda_init):
    B, S, H, _, Dh = q.shape
    q = apply_rope(rms_norm(q, qn_w), cos, sin)
    k = apply_rope(rms_norm(k, kn_w), cos, sin)
    lv = lam_vecs.astype(jnp.float32)
    lam = jnp.exp(jnp.sum(lv[0] * lv[1])) - jnp.exp(jnp.sum(lv[2] * lv[3])) + lambda_init
    scale = DIFF_HEAD_DIM ** -0.5
    qb = q.reshape(B, S // Q_BLOCK, Q_BLOCK, H, 2, Dh).transpose(1, 0, 3, 4, 2, 5)
    kt = k.transpose(0, 2, 3, 1, 4)
    vt = v.transpose(0, 2, 1, 3)

    def block(q_blk):
        s = jnp.einsum('bhmqd,bhmkd->bhmqk', q_blk, kt).astype(jnp.float32) * scale
        p = jax.nn.softmax(s, axis=-1)
        p_diff = p[:, :, 0] - lam * p[:, :, 1]
        return jnp.einsum('bhqk,bhkv->bhqv', p_diff.astype(vt.dtype), vt)

    o = lax.map(block, qb)
    o = o.transpose(1, 0, 3, 2, 4).reshape(B, S, H, 2 * Dh)
    o = rms_norm(o, subln_w) * (1.0 - lambda_init)
    return o.reshape(B, S, H * 2 * Dh)


def short_conv(x, w):
    C = x.shape[-1]
    pad = CONV_WIDTH // 2
    return lax.conv_general_dilated(x, w[:, None, :].astype(x.dtype), window_strides=(1,),
                                    padding=[(pad, pad)], dimension_numbers=('NWC', 'WIO', 'NWC'),
                                    feature_group_count=C)


def gated_delta_chunked(q, k, v, g, beta):
    B, S, H, K = q.shape
    V = v.shape[-1]
    N = S // CHUNK

    def to_chunks(t):
        return jnp.swapaxes(t.reshape((B, N, CHUNK) + t.shape[2:]), 2, 3)

    qc, kc, vc, gc, bc = to_chunks(q), to_chunks(k), to_chunks(v), to_chunks(g), to_chunks(beta)
    g_cum = jnp.cumsum(gc, axis=-1)
    tri = jnp.tril(jnp.ones((CHUNK, CHUNK), dtype=bool))
    strict = jnp.tril(jnp.ones((CHUNK, CHUNK), dtype=bool), -1)
    diff = g_cum[..., :, None] - g_cum[..., None, :]
    decay = jnp.where(tri, jnp.exp(jnp.where(tri, diff, 0.0)), 0.0)
    k_beta = kc * bc[..., None]
    v_beta = vc * bc[..., None]
    L = jnp.where(strict, jnp.einsum('bnhik,bnhjk->bnhij', k_beta, kc) * decay, 0.0)
    A = jnp.eye(CHUNK, dtype=jnp.float32) + L
    u = lax.linalg.triangular_solve(A, v_beta, left_side=True, lower=True, unit_diagonal=True)
    w = lax.linalg.triangular_solve(A, k_beta * jnp.exp(g_cum)[..., None], left_side=True,
                                    lower=True, unit_diagonal=True)
    attn = jnp.einsum('bnhik,bnhjk->bnhij', qc, kc) * decay
    q_dec = qc * jnp.exp(g_cum)[..., None]
    k_tail = kc * jnp.exp(g_cum[..., -1:] - g_cum)[..., None]
    chunk_decay = jnp.exp(g_cum[..., -1])

    def step(state, inp):
        u_n, w_n, qd_n, at_n, kt_n, cd_n = inp
        v_new = u_n - jnp.einsum('bhck,bhkv->bhcv', w_n, state)
        o = jnp.einsum('bhck,bhkv->bhcv', qd_n, state) + jnp.einsum('bhij,bhjv->bhiv', at_n, v_new)
        state = state * cd_n[..., None, None] + jnp.einsum('bhck,bhcv->bhkv', kt_n, v_new)
        return state, o

    xs = tuple(jnp.moveaxis(t, 1, 0) for t in (u, w, q_dec, attn, k_tail, chunk_decay))
    s0 = jnp.zeros((B, H, K, V), jnp.float32)
    _, o = lax.scan(step, s0, xs)
    return o.transpose(1, 0, 3, 2, 4).reshape(B, S, H, V)


def gated_deltanet(q, k, v, z, b, a, conv_w, a_log, dt_bias, norm_w):
    B, S, _ = q.shape
    H, Dh = N_GDN_HEADS, GDN_HEAD_DIM
    qkv = jax.nn.silu(short_conv(jnp.concatenate([q, k, v], axis=-1), conv_w))
    q, k, v = jnp.split(qkv, 3, axis=-1)
    qf = l2_norm(q.reshape(B, S, H, Dh).astype(jnp.float32)) * (Dh ** -0.5)
    kf = l2_norm(k.reshape(B, S, H, Dh).astype(jnp.float32))
    vf = v.reshape(B, S, H, Dh).astype(jnp.float32)
    beta = jax.nn.sigmoid(b.astype(jnp.float32))
    g = -jnp.exp(a_log.astype(jnp.float32)) * jax.nn.softplus(a.astype(jnp.float32) + dt_bias.astype(jnp.float32))
    o_fwd = gated_delta_chunked(qf, kf, vf, g[:, :, 0], beta[:, :, 0])
    flip = lambda t: jnp.flip(t, axis=1)
    o_bwd = flip(gated_delta_chunked(flip(qf), flip(kf), flip(vf), flip(g[:, :, 1]), flip(beta[:, :, 1])))
    o = rms_norm(o_fwd + o_bwd, norm_w) * jax.nn.silu(z.reshape(B, S, H, Dh).astype(jnp.float32))
    return o.reshape(B, S, H * Dh).astype(z.dtype)


def hybrid_mixer(h, cos, sin, lambda_init, w_in, qn_w, kn_w, lam_vecs, subln_w, conv_w, a_log, dt_bias,
                 gdn_norm_w, w_branch_diff, w_branch_gdn, w_out):
    B, S, _ = h.shape
    proj = h @ w_in
    dq, dk, dv, gq, gk, gv, gz, gb, ga, gates = _split(proj, IN_WIDTHS)
    qk_shape = (B, S, N_DIFF_HEADS, 2, DIFF_HEAD_DIM)
    y_diff = diff_attention(dq.reshape(qk_shape), dk.reshape(qk_shape),
                            dv.reshape(B, S, N_DIFF_HEADS, 2 * DIFF_HEAD_DIM), cos, sin,
                            qn_w, kn_w, lam_vecs, subln_w, lambda_init)
    y_gdn = gated_deltanet(gq, gk, gv, gz, gb.reshape(B, S, 2, N_GDN_HEADS), ga.reshape(B, S, 2, N_GDN_HEADS),
                           conv_w, a_log, dt_bias, gdn_norm_w)
    g_diff, g_gdn = jnp.split(jax.nn.sigmoid(gates), 2, axis=-1)
    merged = g_diff * (y_diff @ w_branch_diff) + g_gdn * (y_gdn @ w_branch_gdn)
    return merged @ w_out


def swiglu(h, w_up, w_down):
    gate, up = jnp.split(h @ w_up, 2, axis=-1)
    return (jax.nn.silu(gate) * up) @ w_down


def setup_inputs(seed: int = 0) -> dict:
    key = jax.random.key(seed)
    ks = jax.random.split(key, 24)
    f32 = jnp.float32

    def nrm(k, shape, scale):
        return jax.random.normal(k, shape, f32) * scale

    x = nrm(ks[0], (BATCH, SEQ, D_MODEL), 1.0)
    c = nrm(ks[1], (BATCH, D_MODEL), 1.0)
    offsets = jax.random.randint(ks[2], (BATCH, 1), 0, 4096, dtype=jnp.int32)
    positions = offsets + jnp.arange(SEQ, dtype=jnp.int32)[None, :]
    ada_w = nrm(ks[3], (DEPTH, D_MODEL, N_MOD * D_MODEL), 0.5 * D_MODEL ** -0.5)
    ada_b = nrm(ks[4], (DEPTH, N_MOD * D_MODEL), 0.01)
    norm_mix_w = 1.0 + nrm(ks[5], (DEPTH, D_MODEL), 0.02)
    norm_ffn_w = 1.0 + nrm(ks[6], (DEPTH, D_MODEL), 0.02)
    w_in = nrm(ks[7], (DEPTH, D_MODEL, IN_COLS), D_MODEL ** -0.5)
    diff_qn_w = 1.0 + nrm(ks[8], (DEPTH, DIFF_HEAD_DIM), 0.02)
    diff_kn_w = 1.0 + nrm(ks[9], (DEPTH, DIFF_HEAD_DIM), 0.02)
    diff_lambda = nrm(ks[10], (DEPTH, 4, DIFF_HEAD_DIM), 0.1)
    diff_subln_w = 1.0 + nrm(ks[11], (DEPTH, 2 * DIFF_HEAD_DIM), 0.02)
    gdn_conv_w = nrm(ks[12], (DEPTH, CONV_WIDTH, 3 * GDN_WIDTH), CONV_WIDTH ** -0.5)
    gdn_a_log = jnp.log(jax.random.uniform(ks[13], (DEPTH, 2, N_GDN_HEADS), f32, 1.0, 16.0))
    u = jax.random.uniform(ks[14], (DEPTH, 2, N_GDN_HEADS), f32)
    dt = jnp.exp(u * (math.log(0.1) - math.log(0.001)) + math.log(0.001))
    gdn_dt_bias = dt + jnp.log(-jnp.expm1(-dt))
    gdn_norm_w = 1.0 + nrm(ks[15], (DEPTH, GDN_HEAD_DIM), 0.02)
    w_branch_diff = nrm(ks[16], (DEPTH, DIFF_WIDTH, D_MODEL), DIFF_WIDTH ** -0.5)
    w_branch_gdn = nrm(ks[17], (DEPTH, GDN_WIDTH, D_MODEL), GDN_WIDTH ** -0.5)
    w_out = nrm(ks[18], (DEPTH, D_MODEL, D_MODEL), D_MODEL ** -0.5)
    ffn_w_up = nrm(ks[19], (DEPTH, D_MODEL, 2 * FFN_HIDDEN), D_MODEL ** -0.5)
    ffn_w_down = nrm(ks[20], (DEPTH, FFN_HIDDEN, D_MODEL), FFN_HIDDEN ** -0.5)
    return {'x': x, 'c': c, 'positions': positions, 'ada_w': ada_w, 'ada_b': ada_b,
            'norm_mix_w': norm_mix_w, 'norm_ffn_w': norm_ffn_w, 'w_in': w_in,
            'diff_qn_w': diff_qn_w, 'diff_kn_w': diff_kn_w, 'diff_lambda': diff_lambda,
            'diff_subln_w': diff_subln_w, 'gdn_conv_w': gdn_conv_w, 'gdn_a_log': gdn_a_log,
            'gdn_dt_bias': gdn_dt_bias, 'gdn_norm_w': gdn_norm_w, 'w_branch_diff': w_branch_diff,
            'w_branch_gdn': w_branch_gdn, 'w_out': w_out, 'ffn_w_up': ffn_w_up, 'ffn_w_down': ffn_w_down}


def reference(x, c, positions, ada_w, ada_b, norm_mix_w, norm_ffn_w, w_in, diff_qn_w, diff_kn_w, diff_lambda,
              diff_subln_w, gdn_conv_w, gdn_a_log, gdn_dt_bias, gdn_norm_w, w_branch_diff, w_branch_gdn, w_out,
              ffn_w_up, ffn_w_down):
    cos, sin = rope_tables(positions)
    c_act = jax.nn.silu(c)
    for layer in range(DEPTH):
        lambda_init = 0.8 - 0.6 * math.exp(-0.3 * layer)
        mod = (c_act @ ada_w[layer] + ada_b[layer])[:, None, :]
        sh_m, sc_m, gt_m, sh_f, sc_f, gt_f = jnp.split(mod, N_MOD, axis=-1)
        h = rms_norm(x, norm_mix_w[layer]) * (1.0 + sc_m) + sh_m
        y = hybrid_mixer(h, cos, sin, lambda_init, w_in[layer], diff_qn_w[layer], diff_kn_w[layer],
                         diff_lambda[layer], diff_subln_w[layer], gdn_conv_w[layer], gdn_a_log[layer],
                         gdn_dt_bias[layer], gdn_norm_w[layer], w_branch_diff[layer], w_branch_gdn[layer],
                         w_out[layer])
        x = x + gt_m * y
        h = rms_norm(x, norm_ffn_w[layer]) * (1.0 + sc_f) + sh_f
        x = x + gt_f * swiglu(h, ffn_w_up[layer], ffn_w_down[layer])
    return x
```

```python
import functools
import math

import jax
import jax.numpy as jnp
from jax import lax
from jax.experimental import pallas as pl
from jax.experimental.pallas import tpu as pltpu

F32 = jnp.float32
BF16 = jnp.bfloat16

N_DIFF_HEADS = 8
DIFF_HEAD_DIM = 64
DIFF_WIDTH = N_DIFF_HEADS * 2 * DIFF_HEAD_DIM
N_GDN_HEADS = 8
GDN_HEAD_DIM = 128
GDN_WIDTH = N_GDN_HEADS * GDN_HEAD_DIM
CONV_WIDTH = 5
CHUNK = 64
ROPE_THETA = 10000.0
NORM_EPS = 1e-6
N_MOD = 6
LANES = 128
SUBLANES = 8
NEG_BIG = -1e30

MAIN_COLS = 3 * DIFF_WIDTH + 4 * GDN_WIDTH
SMALL_COLS = 4 * N_GDN_HEADS
VMEM_LIMIT = 48 * 1024 * 1024


def _silu(x):
    return x * jax.nn.sigmoid(x)


def _cparams(n_axes, vmem=VMEM_LIMIT):
    return pltpu.CompilerParams(dimension_semantics=("arbitrary",) * n_axes, vmem_limit_bytes=vmem)


def _layer_call(kernel, grid, in_specs, out_specs, out_shape, scratch_shapes=(), name=None):
    single = not isinstance(out_shape, (tuple, list))
    return pl.pallas_call(
        kernel,
        grid_spec=pltpu.PrefetchScalarGridSpec(
            num_scalar_prefetch=1, grid=grid, in_specs=in_specs,
            out_specs=out_specs if not single else out_specs,
            scratch_shapes=scratch_shapes),
        out_shape=out_shape,
        compiler_params=_cparams(len(grid)),
        name=name,
    )


def _ada_kernel(c_ref, w_ref, b_ref, o_ref):
    c = c_ref[...]
    a = _silu(c).astype(BF16)
    o_ref[0, 0] = jnp.dot(a, w_ref[0].astype(BF16), preferred_element_type=F32) + b_ref[0]


def _ada_modulation(c, ada_w, ada_b):
    depth, d, n6 = ada_w.shape
    b = c.shape[0]
    rows = -(-b // SUBLANES) * SUBLANES
    c_pad = jnp.pad(c, ((0, rows - b), (0, 0)))
    tn = 1024
    per = d // tn
    out = pl.pallas_call(
        _ada_kernel,
        grid=(depth, n6 // tn),
        in_specs=[
            pl.BlockSpec((rows, d), lambda l, j: (0, 0)),
            pl.BlockSpec((1, d, tn), lambda l, j: (l, 0, j)),
            pl.BlockSpec((1, 1, tn), lambda l, j: (l, 0, j)),
        ],
        out_specs=pl.BlockSpec((1, 1, rows, tn), lambda l, j: (l, j // per, 0, j % per)),
        out_shape=jax.ShapeDtypeStruct((depth, N_MOD, rows, d), F32),
        compiler_params=_cparams(2),
        name="ada_modulation",
    )(c_pad, ada_w, ada_b.reshape(depth, 1, n6))
    return out[:, :, :b].reshape(depth, N_MOD, b, 1, d)


def _rope_kernel(ang_ref, cos_ref, sin_ref):
    ang = ang_ref[...]
    lane = lax.broadcasted_iota(jnp.int32, ang.shape, 1)
    first = (lane % DIFF_HEAD_DIM) < (DIFF_HEAD_DIM // 2)
    cos_ref[...] = jnp.cos(ang)
    s = jnp.sin(ang)
    sin_ref[...] = jnp.where(first, -s, s)


def _rope_tables(positions):
    m = positions.size
    half = DIFF_HEAD_DIM // 2
    inv_freq = ROPE_THETA ** (-jnp.arange(half, dtype=F32) * 2.0 / DIFF_HEAD_DIM)
    ang = positions.reshape(m, 1).astype(F32) * jnp.tile(inv_freq, LANES // half)[None, :]
    tm = min(1024, m)
    spec = pl.BlockSpec((tm, LANES), lambda i: (i, 0))
    return pl.pallas_call(
        _rope_kernel, grid=(m // tm,), in_specs=[spec], out_specs=[spec, spec],
        out_shape=[jax.ShapeDtypeStruct((m, LANES), F32)] * 2,
        compiler_params=_cparams(1), name="rope_tables",
    )(ang)


def _norm_mod_kernel(l_ref, x_ref, w_ref, sc_ref, sh_ref, o_ref):
    x = x_ref[...]
    ms = jnp.mean(x * x, axis=-1, keepdims=True)
    y = x * lax.rsqrt(ms + NORM_EPS) * w_ref[0]
    o_ref[...] = (y * (1.0 + sc_ref[0, 0, 0]) + sh_ref[0, 0, 0]).astype(BF16)


def _norm_mod(layer, x, norm_w, mod, scale_idx, shift_idx, seq):
    m, d = x.shape
    tm = min(512, seq)
    return _layer_call(
        _norm_mod_kernel, (m // tm,),
        [
            pl.BlockSpec((tm, d), lambda i, l: (i, 0)),
            pl.BlockSpec((1, 1, d), lambda i, l: (l[0], 0, 0)),
            pl.BlockSpec((1, 1, 1, 1, d), lambda i, l: (l[0], scale_idx, i * tm // seq, 0, 0)),
            pl.BlockSpec((1, 1, 1, 1, d), lambda i, l: (l[0], shift_idx, i * tm // seq, 0, 0)),
        ],
        pl.BlockSpec((tm, d), lambda i, l: (i, 0)),
        jax.ShapeDtypeStruct((m, d), BF16), name="norm_mod",
    )(layer, x, norm_w, mod, mod)


def _matmul(layer, xs, ws, w_x, w_off, extras, epilogue, out_dtypes, n, tm, tn, name):
    m = xs[0].shape[0]
    nx, nw, ne = len(xs), len(ws), len(extras)
    in_specs = [pl.BlockSpec((tm, x.shape[1]), lambda i, j, l: (i, 0)) for x in xs]
    for w, off in zip(ws, w_off):
        in_specs.append(pl.BlockSpec((1, w.shape[1], tn), lambda i, j, l, off=off: (l[0], 0, j + off)))
    for _, bs, imap in extras:
        in_specs.append(pl.BlockSpec(bs, imap))
    out_specs = [pl.BlockSpec((tm, tn), lambda i, j, l: (i, j)) for _ in out_dtypes]
    out_shape = [jax.ShapeDtypeStruct((m, n), dt) for dt in out_dtypes]

    def kern(l_ref, *refs):
        x_refs, w_refs = refs[:nx], refs[nx:nx + nw]
        e_refs, o_refs = refs[nx + nw:nx + nw + ne], refs[nx + nw + ne:]
        accs = [jnp.dot(x_refs[xi][...], w_ref[0], preferred_element_type=F32)
                for xi, w_ref in zip(w_x, w_refs)]
        outs = epilogue(accs, [e[...] for e in e_refs])
        for o_ref, v in zip(o_refs, outs):
            o_ref[...] = v.astype(o_ref.dtype)

    res = _layer_call(kern, (m // tm, n // tn), in_specs, out_specs, out_shape, name=name)(
        layer, *xs, *ws, *[e[0] for e in extras])
    return res


def _group_sumsq(x, group_ones):
    sq = x * x
    hi = sq.astype(BF16)
    lo = (sq - hi.astype(F32)).astype(BF16)
    return (jnp.dot(hi, group_ones, preferred_element_type=F32)
            + jnp.dot(lo, group_ones, preferred_element_type=F32))


def _qk_prep_kernel(l_ref, q_ref, k_ref, v_ref, cos_ref, sin_ref, qw_ref, kw_ref, qo_ref, ko_ref, vo_ref):
    cos = cos_ref[...]
    sin = sin_ref[...]
    row = lax.broadcasted_iota(jnp.int32, (LANES, LANES), 0) // DIFF_HEAD_DIM
    col = lax.broadcasted_iota(jnp.int32, (LANES, LANES), 1) // DIFF_HEAD_DIM
    ones = (row == col).astype(BF16)
    lane = lax.broadcasted_iota(jnp.int32, cos.shape, 1)
    first = (lane % DIFF_HEAD_DIM) < (DIFF_HEAD_DIM // 2)
    half = DIFF_HEAD_DIM // 2

    def norm_rope(x, w, scale):
        ms = _group_sumsq(x, ones) * (1.0 / DIFF_HEAD_DIM)
        y = x * lax.rsqrt(ms + NORM_EPS) * w
        partner = jnp.where(first, pltpu.roll(y, LANES - half, 1), pltpu.roll(y, half, 1))
        return (y * cos + partner * sin) * scale

    for h in range(N_DIFF_HEADS):
        sl = slice(h * LANES, (h + 1) * LANES)
        qo_ref[:, sl] = norm_rope(q_ref[:, sl], qw_ref[0, :, sl], DIFF_HEAD_DIM ** -0.5).astype(BF16)
        ko_ref[:, sl] = norm_rope(k_ref[:, sl], kw_ref[0, :, sl], 1.0).astype(BF16)
    vo_ref[...] = v_ref[...].astype(BF16)


def _qk_prep(layer, proj, cos, sin, qn_w, kn_w):
    m = proj.shape[0]
    tm = min(512, m)
    w = DIFF_WIDTH
    rep = w // DIFF_HEAD_DIM
    depth = qn_w.shape[0]
    qw = jnp.tile(qn_w, (1, rep)).reshape(depth, 1, w)
    kw = jnp.tile(kn_w, (1, rep)).reshape(depth, 1, w)
    blk = lambda c: pl.BlockSpec((tm, w), lambda i, l, c=c: (i, c))
    tab = pl.BlockSpec((tm, LANES), lambda i, l: (i, 0))
    wsp = pl.BlockSpec((1, 1, w), lambda i, l: (l[0], 0, 0))
    osp = pl.BlockSpec((tm, w), lambda i, l: (i, 0))
    return _layer_call(
        _qk_prep_kernel, (m // tm,), [blk(0), blk(1), blk(2), tab, tab, wsp, wsp], [osp] * 3,
        [jax.ShapeDtypeStruct((m, w), BF16)] * 3, name="diff_qk_prep",
    )(layer, proj, proj, proj, cos, sin, qw, kw)


def _diff_attn_kernel(l_ref, sc_ref, q_ref, k_ref, v_ref, w_ref, o_ref):
    lam = sc_ref[0]
    out_scale = sc_ref[1]
    q = q_ref[...]
    k = k_ref[...]
    lane = lax.broadcasted_iota(jnp.int32, q.shape, 1)
    zero = jnp.zeros_like(q)
    q0 = jnp.where(lane < DIFF_HEAD_DIM, q, zero)
    q1 = jnp.where(lane >= DIFF_HEAD_DIM, q, zero)
    nt = (((1,), (1,)), ((), ()))

    def softmax_parts(qm):
        s = lax.dot_general(qm, k, nt, preferred_element_type=F32)
        p = jnp.exp(s - jnp.max(s, axis=-1, keepdims=True))
        return p, 1.0 / jnp.sum(p, axis=-1, keepdims=True)

    p0, r0 = softmax_parts(q0)
    p1, r1 = softmax_parts(q1)
    pd = (p0 * r0 - p1 * (lam * r1)).astype(BF16)
    o = jnp.dot(pd, v_ref[...], preferred_element_type=F32)
    ms = jnp.mean(o * o, axis=-1, keepdims=True)
    o_ref[...] = (o * lax.rsqrt(ms + NORM_EPS) * w_ref[0] * out_scale).astype(BF16)


def _diff_attention(layer, scalars, q, k, v, subln_w, batch, seq):
    m = q.shape[0]
    tq = min(256, seq)
    nq = seq // tq
    depth = subln_w.shape[0]
    return _layer_call(
        _diff_attn_kernel, (batch, N_DIFF_HEADS, nq),
        [
            pl.BlockSpec(memory_space=pltpu.SMEM),
            pl.BlockSpec((tq, LANES), lambda b, h, i, l: (b * nq + i, h)),
            pl.BlockSpec((seq, LANES), lambda b, h, i, l: (b, h)),
            pl.BlockSpec((seq, LANES), lambda b, h, i, l: (b, h)),
            pl.BlockSpec((1, 1, LANES), lambda b, h, i, l: (l[0], 0, 0)),
        ],
        pl.BlockSpec((tq, LANES), lambda b, h, i, l: (b * nq + i, h)),
        jax.ShapeDtypeStruct((m, DIFF_WIDTH), BF16), name="diff_attention",
    )(layer, scalars, q, k, v, subln_w.reshape(depth, 1, LANES))


def _split3(x):
    hi = x.astype(BF16)
    r1 = x - hi.astype(F32)
    mid = r1.astype(BF16)
    lo = (r1 - mid.astype(F32)).astype(BF16)
    return hi, mid, lo


def _gdn_prep_kernel(l_ref, main_ref, prev_ref, next_ref, small_ref, cw_ref, alog_ref, dtb_ref,
                     q_ref, k_ref, v_ref, beta_ref, gc_ref):
    t = pl.program_id(1)
    nt = pl.num_programs(1)
    ts = main_ref.shape[0]
    pad = CONV_WIDTH // 2
    main = main_ref[...]
    top = jnp.where(t > 0, prev_ref[SUBLANES - pad:, :], 0.0)
    bot = jnp.where(t < nt - 1, next_ref[:pad, :], 0.0)
    xe = jnp.concatenate([top, main, bot], axis=0)
    cw = cw_ref[0]
    acc = cw[0:1, :] * xe[0:ts, :]
    for j in range(1, CONV_WIDTH):
        acc = acc + cw[j:j + 1, :] * xe[j:j + ts, :]
    y = _silu(acc)
    for h in range(N_GDN_HEADS):
        sl = slice(h * LANES, (h + 1) * LANES)
        qh = y[:, sl]
        kh = y[:, GDN_WIDTH + h * LANES:GDN_WIDTH + (h + 1) * LANES]
        q_ref[:, sl] = qh * (lax.rsqrt(jnp.sum(qh * qh, axis=-1, keepdims=True) + NORM_EPS)
                             * GDN_HEAD_DIM ** -0.5)
        k_ref[:, sl] = kh * lax.rsqrt(jnp.sum(kh * kh, axis=-1, keepdims=True) + NORM_EPS)
    v_ref[...] = y[:, 2 * GDN_WIDTH:]

    nh2 = 2 * N_GDN_HEADS
    small = small_ref[...]
    beta_ref[...] = jax.nn.sigmoid(small[:, :nh2])
    g = -jnp.exp(alog_ref[0]) * jax.nn.softplus(small[:, nh2:2 * nh2] + dtb_ref[0])
    r = lax.broadcasted_iota(jnp.int32, (ts, ts), 0)
    c = lax.broadcasted_iota(jnp.int32, (ts, ts), 1)
    same = (r // CHUNK) == (c // CHUNK)
    pre = (same & (c <= r)).astype(BF16)
    suf = (same & (c >= r)).astype(BF16)
    parts = _split3(g)
    cs_f = sum(jnp.dot(pre, p, preferred_element_type=F32) for p in parts)
    cs_b = sum(jnp.dot(suf, p, preferred_element_type=F32) for p in parts)
    lane = lax.broadcasted_iota(jnp.int32, g.shape, 1)
    gc_ref[...] = jnp.where(lane < N_GDN_HEADS, cs_f, cs_b)


def _gdn_prep(layer, proj, small, conv_w, a_log, dt_bias, batch, seq):
    m = proj.shape[0]
    ts = min(256, seq)
    nts = seq // ts
    w3 = 3 * GDN_WIDTH
    cb = (3 * DIFF_WIDTH) // w3
    assert cb * w3 == 3 * DIFF_WIDTH
    depth = conv_w.shape[0]
    nh2 = 2 * N_GDN_HEADS
    hb = ts // SUBLANES
    last = m // SUBLANES - 1
    osp = pl.BlockSpec((ts, GDN_WIDTH), lambda b, t, l: (b * nts + t, 0))
    ssp = pl.BlockSpec((ts, nh2), lambda b, t, l: (b * nts + t, 0))
    return _layer_call(
        _gdn_prep_kernel, (batch, nts),
        [
            pl.BlockSpec((ts, w3), lambda b, t, l: (b * nts + t, cb)),
            pl.BlockSpec((SUBLANES, w3), lambda b, t, l: (jnp.maximum((b * nts + t) * hb - 1, 0), cb)),
            pl.BlockSpec((SUBLANES, w3), lambda b, t, l: (jnp.minimum((b * nts + t + 1) * hb, last), cb)),
            pl.BlockSpec((ts, LANES), lambda b, t, l: (b * nts + t, 0)),
            pl.BlockSpec((1, CONV_WIDTH, w3), lambda b, t, l: (l[0], 0, 0)),
            pl.BlockSpec((1, 1, nh2), lambda b, t, l: (l[0], 0, 0)),
            pl.BlockSpec((1, 1, nh2), lambda b, t, l: (l[0], 0, 0)),
        ],
        [osp, osp, osp, ssp, ssp],
        [jax.ShapeDtypeStruct((m, GDN_WIDTH), F32)] * 3 + [jax.ShapeDtypeStruct((m, nh2), F32)] * 2,
        name="gdn_prep",
    )(layer, proj, proj, proj, small, conv_w, a_log.reshape(depth, 1, nh2), dt_bias.reshape(depth, 1, nh2))


def _chunk_masks(backward):
    r = lax.broadcasted_iota(jnp.int32, (CHUNK, CHUNK), 0)
    c = lax.broadcasted_iota(jnp.int32, (CHUNK, CHUNK), 1)
    if backward:
        return r <= c, r < c
    return r >= c, r > c


def _gdn_l_kernel(l_ref, k_ref, beta_ref, gc_ref, gcrow_ref, lf_ref, lb_ref):
    h = pl.program_id(1)
    n_chunks = k_ref.shape[0] // CHUNK
    nt = (((1,), (1,)), ((), ()))
    lane16 = lax.broadcasted_iota(jnp.int32, (CHUNK, 2 * N_GDN_HEADS), 1)
    sub16 = lax.broadcasted_iota(jnp.int32, (2 * N_GDN_HEADS, CHUNK), 0)

    def body(n, carry):
        rows = pl.ds(pl.multiple_of(n * CHUNK, CHUNK), CHUNK)
        kb = k_ref[rows, :].astype(BF16)
        kk = lax.dot_general(kb, kb, nt, preferred_element_type=F32)
        beta = beta_ref[rows, :]
        gc = gc_ref[rows, :]
        gcr = gcrow_ref[0, n]
        for d, out_ref in ((0, lf_ref), (1, lb_ref)):
            col = d * N_GDN_HEADS + h
            bcol = jnp.sum(jnp.where(lane16 == col, beta, 0.0), axis=1, keepdims=True)
            gcol = jnp.sum(jnp.where(lane16 == col, gc, 0.0), axis=1, keepdims=True)
            grow = jnp.sum(jnp.where(sub16 == col, gcr, 0.0), axis=0, keepdims=True)
            _, strict = _chunk_masks(d == 1)
            dec = jnp.exp(jnp.where(strict, gcol - grow, NEG_BIG))
            out_ref[0, rows, :] = bcol * kk * dec
        return carry

    lax.fori_loop(0, n_chunks, body, 0)


def _gdn_build_l(layer, k, beta, gc, gc_rows, batch, seq):
    nh2 = 2 * N_GDN_HEADS
    n_chunks = seq // CHUNK
    osp = pl.BlockSpec((1, seq, CHUNK), lambda b, h, l: (b * N_GDN_HEADS + h, 0, 0))
    return _layer_call(
        _gdn_l_kernel, (batch, N_GDN_HEADS),
        [
            pl.BlockSpec((seq, LANES), lambda b, h, l: (b, h)),
            pl.BlockSpec((seq, nh2), lambda b, h, l: (b, 0)),
            pl.BlockSpec((seq, nh2), lambda b, h, l: (b, 0)),
            pl.BlockSpec((1, n_chunks, nh2, CHUNK), lambda b, h, l: (b, 0, 0, 0)),
        ],
        [osp, osp],
        [jax.ShapeDtypeStruct((batch * N_GDN_HEADS, seq, CHUNK), F32)] * 2, name="gdn_build_l",
    )(layer, k, beta, gc, gc_rows)


def _tri_inverse_kernel(l_ref, t_ref):
    sub = lax.broadcasted_iota(jnp.int32, (SUBLANES, LANES), 0)
    nblk = CHUNK // SUBLANES
    for i in range(CHUNK):
        live = i // SUBLANES + 1
        accs = [jnp.zeros((SUBLANES, LANES), F32) for _ in range(live)]
        for j in range(i):
            lij = l_ref[i, pl.ds(j, 1), :]
            for cb in range(j // SUBLANES + 1):
                accs[cb] = accs[cb] + lij * t_ref[j, cb * SUBLANES:(cb + 1) * SUBLANES, :]
        for cb in range(nblk):
            if cb < live - 1:
                val = -accs[cb]
            elif cb == live - 1:
                val = jnp.where(sub == i % SUBLANES, 1.0, 0.0) - accs[cb]
            else:
                val = jnp.zeros((SUBLANES, LANES), F32)
            t_ref[i, cb * SUBLANES:(cb + 1) * SUBLANES, :] = val


def _tri_inverse(l_all):
    g = l_all.shape[-1]
    spec = pl.BlockSpec((CHUNK, CHUNK, LANES), lambda i: (0, 0, i))
    return pl.pallas_call(
        _tri_inverse_kernel, grid=(g // LANES,), in_specs=[spec], out_specs=spec,
        out_shape=jax.ShapeDtypeStruct(l_all.shape, F32),
        compiler_params=_cparams(1), name="gdn_tri_inverse",
    )(l_all)


def _gdn_scan_kernel(l_ref, q_ref, k_ref, v_ref, z_ref, beta_ref, gc_ref, gcrow_ref, tf_ref, tb_ref, nw_ref,
                     o_ref, mq_ref, r_ref, ol_ref, gl_ref, acc_ref):
    h = pl.program_id(1)
    seq = q_ref.shape[0]
    n_chunks = seq // CHUNK
    dk = GDN_HEAD_DIM
    nt = (((1,), (1,)), ((), ()))
    tn = (((0,), (0,)), ((), ()))
    lane16 = lax.broadcasted_iota(jnp.int32, (CHUNK, 2 * N_GDN_HEADS), 1)
    sub16 = lax.broadcasted_iota(jnp.int32, (2 * N_GDN_HEADS, CHUNK), 0)

    def precompute(n, carry):
        rows = pl.ds(pl.multiple_of(n * CHUNK, CHUNK), CHUNK)
        qc = q_ref[rows, :]
        kc = k_ref[rows, :]
        vc = v_ref[rows, :]
        qk = lax.dot_general(qc.astype(BF16), kc.astype(BF16), nt, preferred_element_type=F32)
        beta = beta_ref[rows, :]
        gc = gc_ref[rows, :]
        gcr = gcrow_ref[0, n]
        for d, t_ref in ((0, tf_ref), (1, tb_ref)):
            col = d * N_GDN_HEADS + h
            bcol = jnp.sum(jnp.where(lane16 == col, beta, 0.0), axis=1, keepdims=True)
            gcol = jnp.sum(jnp.where(lane16 == col, gc, 0.0), axis=1, keepdims=True)
            grow = jnp.sum(jnp.where(sub16 == col, gcr, 0.0), axis=0, keepdims=True)
            glast = gcol[0:1, :] if d == 1 else gcol[CHUNK - 1:CHUNK, :]
            incl, _ = _chunk_masks(d == 1)
            attn = qk * jnp.exp(jnp.where(incl, gcol - grow, NEG_BIG))
            gam = jnp.exp(gcol)
            x = jnp.concatenate([bcol * vc, (bcol * gam) * kc], axis=1).astype(BF16)
            t = t_ref[0, rows, :]
            th = t.astype(BF16)
            tl = (t - th.astype(F32)).astype(BF16)
            uw = (jnp.dot(th, x, preferred_element_type=F32)
                  + jnp.dot(tl, x, preferred_element_type=F32)).astype(BF16)
            awu = jnp.dot(attn.astype(BF16), uw, preferred_element_type=F32)
            ol_ref[d, n] = awu[:, :dk]
            qt = gam * qc - awu[:, dk:]
            ktil = (kc * jnp.exp(glast - gcol)).astype(BF16)
            kwu = lax.dot_general(ktil, uw, tn, preferred_element_type=F32)
            r_ref[d, n] = kwu[:, :dk]
            mq_ref[d, n, 0:dk, :] = kwu[:, dk:].astype(BF16)
            mq_ref[d, n, dk:dk + CHUNK, :] = qt.astype(BF16)
            gl_ref[d, n] = jnp.broadcast_to(jnp.exp(glast), (SUBLANES, LANES))
        return carry

    lax.fori_loop(0, n_chunks, precompute, 0)
    acc_ref[...] = jnp.zeros_like(acc_ref)

    def step(s, states):
        new_states = []
        for d in (0, 1):
            n = (n_chunks - 1 - s) if d == 1 else s
            rows = pl.ds(pl.multiple_of(n * CHUNK, CHUNK), CHUNK)
            st = states[d]
            res = jnp.dot(mq_ref[d, n], st.astype(BF16), preferred_element_type=F32)
            acc_ref[rows, :] += res[dk:, :] + ol_ref[d, n]
            new_states.append(gl_ref[d, n][0:1, :] * st - res[:dk, :] + r_ref[d, n])
        return tuple(new_states)

    zero = jnp.zeros((dk, dk), F32)
    lax.fori_loop(0, n_chunks, step, (zero, zero))

    o = acc_ref[...]
    ms = jnp.mean(o * o, axis=-1, keepdims=True)
    o_ref[...] = (o * lax.rsqrt(ms + NORM_EPS) * nw_ref[0] * _silu(z_ref[...])).astype(BF16)


def _gdn_scan(layer, q, k, v, proj, beta, gc, gc_rows, t_f, t_b, norm_w, batch, seq):
    m = q.shape[0]
    nh2 = 2 * N_GDN_HEADS
    n_chunks = seq // CHUNK
    dk = GDN_HEAD_DIM
    depth = norm_w.shape[0]
    zcb = (3 * DIFF_WIDTH + 3 * GDN_WIDTH) // LANES
    hsp = pl.BlockSpec((seq, LANES), lambda b, h, l: (b, h))
    tsp = pl.BlockSpec((1, seq, CHUNK), lambda b, h, l: (b * N_GDN_HEADS + h, 0, 0))
    return _layer_call(
        _gdn_scan_kernel, (batch, N_GDN_HEADS),
        [
            hsp, hsp, hsp,
            pl.BlockSpec((seq, LANES), lambda b, h, l: (b, zcb + h)),
            pl.BlockSpec((seq, nh2), lambda b, h, l: (b, 0)),
            pl.BlockSpec((seq, nh2), lambda b, h, l: (b, 0)),
            pl.BlockSpec((1, n_chunks, nh2, CHUNK), lambda b, h, l: (b, 0, 0, 0)),
            tsp, tsp,
            pl.BlockSpec((1, 1, LANES), lambda b, h, l: (l[0], 0, 0)),
        ],
        hsp,
        jax.ShapeDtypeStruct((m, GDN_WIDTH), BF16),
        scratch_shapes=[
            pltpu.VMEM((2, n_chunks, dk + CHUNK, dk), BF16),
            pltpu.VMEM((2, n_chunks, dk, dk), F32),
            pltpu.VMEM((2, n_chunks, CHUNK, dk), F32),
            pltpu.VMEM((2, n_chunks, SUBLANES, LANES), F32),
            pltpu.VMEM((seq, dk), F32),
        ],
        name="gdn_scan",
    )(layer, q, k, v, proj, beta, gc, gc_rows, t_f, t_b, norm_w.reshape(depth, 1, LANES))


def _gated_deltanet(layer, proj, small, conv_w, a_log, dt_bias, norm_w, batch, seq):
    q, k, v, beta, gc = _gdn_prep(layer, proj, small, conv_w, a_log, dt_bias, batch, seq)
    n_chunks = seq // CHUNK
    nh2 = 2 * N_GDN_HEADS
    gc_rows = gc.reshape(batch, n_chunks, CHUNK, nh2).transpose(0, 1, 3, 2)
    l_f, l_b = _gdn_build_l(layer, k, beta, gc, gc_rows, batch, seq)
    g0 = batch * N_GDN_HEADS * n_chunks
    l_all = jnp.concatenate([l_f.reshape(g0, CHUNK, CHUNK).transpose(1, 2, 0),
                             l_b.reshape(g0, CHUNK, CHUNK).transpose(2, 1, 0)], axis=-1)
    t_all = _tri_inverse(l_all)
    t_f = t_all[:, :, :g0].transpose(2, 0, 1).reshape(batch * N_GDN_HEADS, seq, CHUNK)
    t_b = t_all[:, :, g0:].transpose(2, 1, 0).reshape(batch * N_GDN_HEADS, seq, CHUNK)
    return _gdn_scan(layer, q, k, v, proj, beta, gc, gc_rows, t_f, t_b, norm_w, batch, seq)


def _layer(l, x, cos, sin, mod, lam_inits, p, batch, seq):
    m, d = x.shape
    layer = jnp.reshape(l, (1,)).astype(jnp.int32)
    tm = min(1024, seq)
    ident = lambda accs, extras: accs

    h = _norm_mod(layer, x, p["norm_mix_w"], mod, 1, 0, seq)
    (proj,) = _matmul(layer, [h], [p["w_main"]], [0], [0], [], ident, [F32], MAIN_COLS, tm, 1024, "proj_main")
    (small,) = _matmul(layer, [h], [p["w_small"]], [0], [0], [], ident, [F32], LANES, tm, LANES, "proj_small")
    (gates,) = _matmul(layer, [h], [p["w_gates"]], [0], [0], [],
                       lambda accs, extras: [jax.nn.sigmoid(accs[0])], [F32], 2 * d, tm, 1024, "proj_gates")

    lam_init = lam_inits[l]
    lv = p["diff_lambda"][l].astype(F32)
    lam = jnp.exp(jnp.sum(lv[0] * lv[1])) - jnp.exp(jnp.sum(lv[2] * lv[3])) + lam_init
    scalars = jnp.stack([lam, 1.0 - lam_init]).astype(F32)
    dq, dk, dv = _qk_prep(layer, proj, cos, sin, p["diff_qn_w"], p["diff_kn_w"])
    y_diff = _diff_attention(layer, scalars, dq, dk, dv, p["diff_subln_w"], batch, seq)

    y_gdn = _gated_deltanet(layer, proj, small, p["gdn_conv_w"], p["gdn_a_log"], p["gdn_dt_bias"],
                            p["gdn_norm_w"], batch, seq)

    tn = 1024
    (merged,) = _matmul(
        layer, [y_diff, y_gdn], [p["w_branch_diff"], p["w_branch_gdn"]], [0, 1], [0, 0],
        [(gates, (tm, tn), lambda i, j, l: (i, j)), (gates, (tm, tn), lambda i, j, l: (i, j + d // tn))],
        lambda accs, extras: [extras[0] * accs[0] + extras[1] * accs[1]], [BF16], d, tm, tn, "branch_merge")

    def residual(gate_idx):
        return [(x, (tm, tn), lambda i, j, l: (i, j)),
                (mod, (1, 1, 1, 1, tn), lambda i, j, l: (l[0], gate_idx, i * tm // seq, 0, j))]

    res_epi = lambda accs, extras: [extras[0] + extras[1][0, 0, 0] * accs[0]]
    (x,) = _matmul(layer, [merged], [p["w_out"]], [0], [0], residual(2), res_epi, [F32], d, tm, tn, "mixer_out")

    h = _norm_mod(layer, x, p["norm_ffn_w"], mod, 4, 3, seq)
    f = p["ffn_w_down"].shape[1]
    tf = 512
    (act,) = _matmul(layer, [h], [p["ffn_w_up"], p["ffn_w_up"]], [0, 0], [0, f // tf], [],
                     lambda accs, extras: [_silu(accs[0]) * accs[1]], [BF16], f, tm, tf, "ffn_up")
    tm2, tn2 = min(512, seq), 512

    def residual2(gate_idx):
        return [(x, (tm2, tn2), lambda i, j, l: (i, j)),
                (mod, (1, 1, 1, 1, tn2), lambda i, j, l: (l[0], gate_idx, i * tm2 // seq, 0, j))]

    (x,) = _matmul(layer, [act], [p["ffn_w_down"]], [0], [0], residual2(5), res_epi, [F32], d, tm2, tn2, "ffn_down")
    return x


def kernel(x, c, positions, ada_w, ada_b, norm_mix_w, norm_ffn_w, w_in, diff_qn_w, diff_kn_w, diff_lambda,
           diff_subln_w, gdn_conv_w, gdn_a_log, gdn_dt_bias, gdn_norm_w, w_branch_diff, w_branch_gdn, w_out,
           ffn_w_up, ffn_w_down):
    batch, seq, d = x.shape
    depth = ada_w.shape[0]
    mod = _ada_modulation(c, ada_w, ada_b)
    cos, sin = _rope_tables(positions)
    lam_inits = jnp.asarray([0.8 - 0.6 * math.exp(-0.3 * i) for i in range(depth)], F32)
    small_w = jnp.pad(w_in[:, :, MAIN_COLS:MAIN_COLS + SMALL_COLS], ((0, 0), (0, 0), (0, LANES - SMALL_COLS)))
    p = {
        "norm_mix_w": norm_mix_w.reshape(depth, 1, d), "norm_ffn_w": norm_ffn_w.reshape(depth, 1, d),
        "w_main": w_in[:, :, :MAIN_COLS].astype(BF16),
        "w_small": small_w.astype(BF16),
        "w_gates": w_in[:, :, MAIN_COLS + SMALL_COLS:].astype(BF16),
        "diff_qn_w": diff_qn_w, "diff_kn_w": diff_kn_w, "diff_lambda": diff_lambda, "diff_subln_w": diff_subln_w,
        "gdn_conv_w": gdn_conv_w, "gdn_a_log": gdn_a_log, "gdn_dt_bias": gdn_dt_bias, "gdn_norm_w": gdn_norm_w,
        "w_branch_diff": w_branch_diff.astype(BF16), "w_branch_gdn": w_branch_gdn.astype(BF16),
        "w_out": w_out.astype(BF16), "ffn_w_up": ffn_w_up.astype(BF16), "ffn_w_down": ffn_w_down.astype(BF16),
    }
    body = lambda l, xc: _layer(l, xc, cos, sin, mod, lam_inits, p, batch, seq)
    out = lax.fori_loop(0, depth, body, x.reshape(batch * seq, d))
    return out.reshape(batch, seq, d)
```

```python
import functools
import math

import jax
import jax.numpy as jnp
from jax import lax
from jax.experimental import pallas as pl
from jax.experimental.pallas import tpu as pltpu

F32 = jnp.float32
BF16 = jnp.bfloat16

N_DIFF_HEADS = 8
DIFF_HEAD_DIM = 64
DIFF_WIDTH = N_DIFF_HEADS * 2 * DIFF_HEAD_DIM
N_GDN_HEADS = 8
GDN_HEAD_DIM = 128
GDN_WIDTH = N_GDN_HEADS * GDN_HEAD_DIM
CONV_WIDTH = 5
CHUNK = 64
ROPE_THETA = 10000.0
NORM_EPS = 1e-6
N_MOD = 6
LANES = 128
SUBLANES = 8
NEG_BIG = -1e30
GDN_HEADS_PER_STEP = 1
ATTN_Q_ROWS = 2048
ATTN_SUB_ROWS = 256

MAIN_COLS = 3 * DIFF_WIDTH + 4 * GDN_WIDTH
SMALL_COLS = 4 * N_GDN_HEADS
VMEM_LIMIT = 48 * 1024 * 1024


def _silu(x):
    return x * jax.nn.sigmoid(x)


def _cparams(n_axes, vmem=VMEM_LIMIT):
    return pltpu.CompilerParams(dimension_semantics=("arbitrary",) * n_axes, vmem_limit_bytes=vmem)


def _layer_call(kernel, grid, in_specs, out_specs, out_shape, scratch_shapes=(), name=None):
    single = not isinstance(out_shape, (tuple, list))
    return pl.pallas_call(
        kernel,
        grid_spec=pltpu.PrefetchScalarGridSpec(
            num_scalar_prefetch=1, grid=grid, in_specs=in_specs,
            out_specs=out_specs if not single else out_specs,
            scratch_shapes=scratch_shapes),
        out_shape=out_shape,
        compiler_params=_cparams(len(grid)),
        name=name,
    )


def _ada_kernel(c_ref, w_ref, b_ref, o_ref):
    c = c_ref[...]
    a = _silu(c).astype(BF16)
    o_ref[0, 0] = jnp.dot(a, w_ref[0].astype(BF16), preferred_element_type=F32) + b_ref[0]


def _ada_modulation(c, ada_w, ada_b):
    depth, d, n6 = ada_w.shape
    b = c.shape[0]
    rows = -(-b // SUBLANES) * SUBLANES
    c_pad = jnp.pad(c, ((0, rows - b), (0, 0)))
    tn = 1024
    per = d // tn
    out = pl.pallas_call(
        _ada_kernel,
        grid=(depth, n6 // tn),
        in_specs=[
            pl.BlockSpec((rows, d), lambda l, j: (0, 0)),
            pl.BlockSpec((1, d, tn), lambda l, j: (l, 0, j)),
            pl.BlockSpec((1, 1, tn), lambda l, j: (l, 0, j)),
        ],
        out_specs=pl.BlockSpec((1, 1, rows, tn), lambda l, j: (l, j // per, 0, j % per)),
        out_shape=jax.ShapeDtypeStruct((depth, N_MOD, rows, d), F32),
        compiler_params=_cparams(2),
        name="ada_modulation",
    )(c_pad, ada_w, ada_b.reshape(depth, 1, n6))
    return out[:, :, :b].reshape(depth, N_MOD, b, 1, d)


def _rope_kernel(ang_ref, cos_ref, sin_ref):
    ang = ang_ref[...]
    lane = lax.broadcasted_iota(jnp.int32, ang.shape, 1)
    first = (lane % DIFF_HEAD_DIM) < (DIFF_HEAD_DIM // 2)
    cos_ref[...] = jnp.cos(ang)
    s = jnp.sin(ang)
    sin_ref[...] = jnp.where(first, -s, s)


def _rope_tables(positions):
    m = positions.size
    half = DIFF_HEAD_DIM // 2
    inv_freq = ROPE_THETA ** (-jnp.arange(half, dtype=F32) * 2.0 / DIFF_HEAD_DIM)
    ang = positions.reshape(m, 1).astype(F32) * jnp.tile(inv_freq, LANES // half)[None, :]
    tm = min(1024, m)
    spec = pl.BlockSpec((tm, LANES), lambda i: (i, 0))
    return pl.pallas_call(
        _rope_kernel, grid=(m // tm,), in_specs=[spec], out_specs=[spec, spec],
        out_shape=[jax.ShapeDtypeStruct((m, LANES), F32)] * 2,
        compiler_params=_cparams(1), name="rope_tables",
    )(ang)


def _norm_mod_kernel(l_ref, x_ref, w_ref, sc_ref, sh_ref, o_ref):
    x = x_ref[...]
    ms = jnp.mean(x * x, axis=-1, keepdims=True)
    y = x * lax.rsqrt(ms + NORM_EPS) * w_ref[0]
    o_ref[...] = (y * (1.0 + sc_ref[0, 0, 0]) + sh_ref[0, 0, 0]).astype(BF16)


def _norm_mod(layer, x, norm_w, mod, scale_idx, shift_idx, seq):
    m, d = x.shape
    tm = min(512, seq)
    return _layer_call(
        _norm_mod_kernel, (m // tm,),
        [
            pl.BlockSpec((tm, d), lambda i, l: (i, 0)),
            pl.BlockSpec((1, 1, d), lambda i, l: (l[0], 0, 0)),
            pl.BlockSpec((1, 1, 1, 1, d), lambda i, l: (l[0], scale_idx, i * tm // seq, 0, 0)),
            pl.BlockSpec((1, 1, 1, 1, d), lambda i, l: (l[0], shift_idx, i * tm // seq, 0, 0)),
        ],
        pl.BlockSpec((tm, d), lambda i, l: (i, 0)),
        jax.ShapeDtypeStruct((m, d), BF16), name="norm_mod",
    )(layer, x, norm_w, mod, mod)


def _matmul(layer, xs, ws, w_x, w_off, extras, epilogue, out_dtypes, n, tm, tn, name):
    m = xs[0].shape[0]
    nx, nw, ne = len(xs), len(ws), len(extras)
    in_specs = [pl.BlockSpec((tm, x.shape[1]), lambda i, j, l: (i, 0)) for x in xs]
    for w, off in zip(ws, w_off):
        in_specs.append(pl.BlockSpec((1, w.shape[1], tn), lambda i, j, l, off=off: (l[0], 0, j + off)))
    for _, bs, imap in extras:
        in_specs.append(pl.BlockSpec(bs, imap))
    out_specs = [pl.BlockSpec((tm, tn), lambda i, j, l: (i, j)) for _ in out_dtypes]
    out_shape = [jax.ShapeDtypeStruct((m, n), dt) for dt in out_dtypes]

    def kern(l_ref, *refs):
        x_refs, w_refs = refs[:nx], refs[nx:nx + nw]
        e_refs, o_refs = refs[nx + nw:nx + nw + ne], refs[nx + nw + ne:]
        accs = [jnp.dot(x_refs[xi][...], w_ref[0], preferred_element_type=F32)
                for xi, w_ref in zip(w_x, w_refs)]
        outs = epilogue(accs, [e[...] for e in e_refs])
        for o_ref, v in zip(o_refs, outs):
            o_ref[...] = v.astype(o_ref.dtype)

    res = _layer_call(kern, (m // tm, n // tn), in_specs, out_specs, out_shape, name=name)(
        layer, *xs, *ws, *[e[0] for e in extras])
    return res


def _group_sumsq(x, group_ones):
    sq = x * x
    hi = sq.astype(BF16)
    lo = (sq - hi.astype(F32)).astype(BF16)
    return (jnp.dot(hi, group_ones, preferred_element_type=F32)
            + jnp.dot(lo, group_ones, preferred_element_type=F32))


def _qk_prep_kernel(l_ref, q_ref, k_ref, v_ref, cos_ref, sin_ref, qw_ref, kw_ref, qo_ref, ko_ref, vo_ref):
    cos = cos_ref[...]
    sin = sin_ref[...]
    row = lax.broadcasted_iota(jnp.int32, (LANES, LANES), 0) // DIFF_HEAD_DIM
    col = lax.broadcasted_iota(jnp.int32, (LANES, LANES), 1) // DIFF_HEAD_DIM
    ones = (row == col).astype(BF16)
    lane = lax.broadcasted_iota(jnp.int32, cos.shape, 1)
    first = (lane % DIFF_HEAD_DIM) < (DIFF_HEAD_DIM // 2)
    half = DIFF_HEAD_DIM // 2

    def norm_rope(x, w, scale):
        ms = _group_sumsq(x, ones) * (1.0 / DIFF_HEAD_DIM)
        y = x * lax.rsqrt(ms + NORM_EPS) * w
        partner = jnp.where(first, pltpu.roll(y, LANES - half, 1), pltpu.roll(y, half, 1))
        return (y * cos + partner * sin) * scale

    q_scale = DIFF_HEAD_DIM ** -0.5 * math.log2(math.e)
    ones_blk = jnp.ones((cos.shape[0], LANES), BF16)
    for h in range(N_DIFF_HEADS):
        sl = slice(h * LANES, (h + 1) * LANES)
        qo_ref[:, sl] = norm_rope(q_ref[:, sl], qw_ref[0, :, sl], q_scale).astype(BF16)
        ko_ref[:, sl] = norm_rope(k_ref[:, sl], kw_ref[0, :, sl], 1.0).astype(BF16)
        vo_ref[:, 2 * h * LANES:(2 * h + 1) * LANES] = v_ref[:, sl].astype(BF16)
        vo_ref[:, (2 * h + 1) * LANES:(2 * h + 2) * LANES] = ones_blk


def _qk_prep(layer, proj, cos, sin, qn_w, kn_w):
    m = proj.shape[0]
    tm = min(512, m)
    w = DIFF_WIDTH
    rep = w // DIFF_HEAD_DIM
    depth = qn_w.shape[0]
    qw = jnp.tile(qn_w, (1, rep)).reshape(depth, 1, w)
    kw = jnp.tile(kn_w, (1, rep)).reshape(depth, 1, w)
    blk = lambda c: pl.BlockSpec((tm, w), lambda i, l, c=c: (i, c))
    tab = pl.BlockSpec((tm, LANES), lambda i, l: (i, 0))
    wsp = pl.BlockSpec((1, 1, w), lambda i, l: (l[0], 0, 0))
    osp = pl.BlockSpec((tm, w), lambda i, l: (i, 0))
    vsp = pl.BlockSpec((tm, 2 * w), lambda i, l: (i, 0))
    return _layer_call(
        _qk_prep_kernel, (m // tm,), [blk(0), blk(1), blk(2), tab, tab, wsp, wsp], [osp, osp, vsp],
        [jax.ShapeDtypeStruct((m, w), BF16)] * 2 + [jax.ShapeDtypeStruct((m, 2 * w), BF16)], name="diff_qk_prep",
    )(layer, proj, proj, proj, cos, sin, qw, kw)


def _diff_attn_kernel(l_ref, sc_ref, q_ref, k_ref, v_ref, w_ref, o_ref):
    lam = sc_ref[0]
    out_scale = sc_ref[1]
    k = k_ref[...]
    nt = (((1,), (1,)), ((), ()))
    sub = min(ATTN_SUB_ROWS, q_ref.shape[0])

    def branch(qm):
        s = lax.dot_general(qm, k, nt, preferred_element_type=F32)
        p = jnp.exp2(s - jnp.max(s, axis=-1, keepdims=True)).astype(BF16)
        ov = jnp.dot(p, v_ref[...], preferred_element_type=F32)
        return ov[:, :LANES] / ov[:, LANES:]

    for r in range(q_ref.shape[0] // sub):
        rows = slice(r * sub, (r + 1) * sub)
        q = q_ref[rows, :]
        lane = lax.broadcasted_iota(jnp.int32, q.shape, 1)
        zero = jnp.zeros_like(q)
        q0 = jnp.where(lane < DIFF_HEAD_DIM, q, zero)
        q1 = jnp.where(lane >= DIFF_HEAD_DIM, q, zero)
        o = branch(q0) - lam * branch(q1)
        ms = jnp.mean(o * o, axis=-1, keepdims=True)
        o_ref[rows, :] = (o * lax.rsqrt(ms + NORM_EPS) * w_ref[0] * out_scale).astype(BF16)


def _diff_attention(layer, scalars, q, k, v, subln_w, batch, seq):
    m = q.shape[0]
    tq = min(ATTN_Q_ROWS, seq)
    nq = seq // tq
    depth = subln_w.shape[0]
    return _layer_call(
        _diff_attn_kernel, (batch, N_DIFF_HEADS, nq),
        [
            pl.BlockSpec(memory_space=pltpu.SMEM),
            pl.BlockSpec((tq, LANES), lambda b, h, i, l: (b * nq + i, h)),
            pl.BlockSpec((seq, LANES), lambda b, h, i, l: (b, h)),
            pl.BlockSpec((seq, 2 * LANES), lambda b, h, i, l: (b, h)),
            pl.BlockSpec((1, 1, LANES), lambda b, h, i, l: (l[0], 0, 0)),
        ],
        pl.BlockSpec((tq, LANES), lambda b, h, i, l: (b * nq + i, h)),
        jax.ShapeDtypeStruct((m, DIFF_WIDTH), BF16), name="diff_attention",
    )(layer, scalars, q, k, v, subln_w.reshape(depth, 1, LANES))


def _split3(x):
    hi = x.astype(BF16)
    r1 = x - hi.astype(F32)
    mid = r1.astype(BF16)
    lo = (r1 - mid.astype(F32)).astype(BF16)
    return hi, mid, lo


def _gdn_prep_kernel(l_ref, main_ref, prev_ref, next_ref, small_ref, cw_ref, alog_ref, dtb_ref,
                     q_ref, k_ref, v_ref, beta_ref, gc_ref):
    t = pl.program_id(1)
    nt = pl.num_programs(1)
    ts = main_ref.shape[0]
    pad = CONV_WIDTH // 2
    main = main_ref[...]
    top = jnp.where(t > 0, prev_ref[SUBLANES - pad:, :], 0.0)
    bot = jnp.where(t < nt - 1, next_ref[:pad, :], 0.0)
    xe = jnp.concatenate([top, main, bot], axis=0)
    cw = cw_ref[0]
    acc = cw[0:1, :] * xe[0:ts, :]
    for j in range(1, CONV_WIDTH):
        acc = acc + cw[j:j + 1, :] * xe[j:j + ts, :]
    y = _silu(acc)
    for h in range(N_GDN_HEADS):
        sl = slice(h * LANES, (h + 1) * LANES)
        qh = y[:, sl]
        kh = y[:, GDN_WIDTH + h * LANES:GDN_WIDTH + (h + 1) * LANES]
        q_ref[:, sl] = qh * (lax.rsqrt(jnp.sum(qh * qh, axis=-1, keepdims=True) + NORM_EPS)
                             * GDN_HEAD_DIM ** -0.5)
        k_ref[:, sl] = kh * lax.rsqrt(jnp.sum(kh * kh, axis=-1, keepdims=True) + NORM_EPS)
    v_ref[...] = y[:, 2 * GDN_WIDTH:]

    nh2 = 2 * N_GDN_HEADS
    small = small_ref[...]
    beta_ref[...] = jax.nn.sigmoid(small[:, :nh2])
    g = -jnp.exp(alog_ref[0]) * jax.nn.softplus(small[:, nh2:2 * nh2] + dtb_ref[0])
    r = lax.broadcasted_iota(jnp.int32, (ts, ts), 0)
    c = lax.broadcasted_iota(jnp.int32, (ts, ts), 1)
    same = (r // CHUNK) == (c // CHUNK)
    pre = (same & (c <= r)).astype(BF16)
    suf = (same & (c >= r)).astype(BF16)
    parts = _split3(g)
    cs_f = sum(jnp.dot(pre, p, preferred_element_type=F32) for p in parts)
    cs_b = sum(jnp.dot(suf, p, preferred_element_type=F32) for p in parts)
    lane = lax.broadcasted_iota(jnp.int32, g.shape, 1)
    gc_ref[...] = jnp.where(lane < N_GDN_HEADS, cs_f, cs_b)


def _gdn_prep(layer, proj, small, conv_w, a_log, dt_bias, batch, seq):
    m = proj.shape[0]
    ts = min(256, seq)
    nts = seq // ts
    w3 = 3 * GDN_WIDTH
    cb = (3 * DIFF_WIDTH) // w3
    assert cb * w3 == 3 * DIFF_WIDTH
    depth = conv_w.shape[0]
    nh2 = 2 * N_GDN_HEADS
    hb = ts // SUBLANES
    last = m // SUBLANES - 1
    osp = pl.BlockSpec((ts, GDN_WIDTH), lambda b, t, l: (b * nts + t, 0))
    ssp = pl.BlockSpec((ts, nh2), lambda b, t, l: (b * nts + t, 0))
    return _layer_call(
        _gdn_prep_kernel, (batch, nts),
        [
            pl.BlockSpec((ts, w3), lambda b, t, l: (b * nts + t, cb)),
            pl.BlockSpec((SUBLANES, w3), lambda b, t, l: (jnp.maximum((b * nts + t) * hb - 1, 0), cb)),
            pl.BlockSpec((SUBLANES, w3), lambda b, t, l: (jnp.minimum((b * nts + t + 1) * hb, last), cb)),
            pl.BlockSpec((ts, LANES), lambda b, t, l: (b * nts + t, 0)),
            pl.BlockSpec((1, CONV_WIDTH, w3), lambda b, t, l: (l[0], 0, 0)),
            pl.BlockSpec((1, 1, nh2), lambda b, t, l: (l[0], 0, 0)),
            pl.BlockSpec((1, 1, nh2), lambda b, t, l: (l[0], 0, 0)),
        ],
        [osp, osp, osp, ssp, ssp],
        [jax.ShapeDtypeStruct((m, GDN_WIDTH), F32)] * 3 + [jax.ShapeDtypeStruct((m, nh2), F32)] * 2,
        name="gdn_prep",
    )(layer, proj, proj, proj, small, conv_w, a_log.reshape(depth, 1, nh2), dt_bias.reshape(depth, 1, nh2))


def _chunk_masks(backward):
    r = lax.broadcasted_iota(jnp.int32, (CHUNK, CHUNK), 0)
    c = lax.broadcasted_iota(jnp.int32, (CHUNK, CHUNK), 1)
    if backward:
        return r <= c, r < c
    return r >= c, r > c


def _gdn_l_kernel(l_ref, k_ref, beta_ref, gc_ref, gcrow_ref, lf_ref, lb_ref):
    h = pl.program_id(1)
    n_chunks = k_ref.shape[0] // CHUNK
    nt = (((1,), (1,)), ((), ()))
    lane16 = lax.broadcasted_iota(jnp.int32, (CHUNK, 2 * N_GDN_HEADS), 1)
    sub16 = lax.broadcasted_iota(jnp.int32, (2 * N_GDN_HEADS, CHUNK), 0)

    def body(n, carry):
        rows = pl.ds(pl.multiple_of(n * CHUNK, CHUNK), CHUNK)
        kb = k_ref[rows, :].astype(BF16)
        kk = lax.dot_general(kb, kb, nt, preferred_element_type=F32)
        beta = beta_ref[rows, :]
        gc = gc_ref[rows, :]
        gcr = gcrow_ref[0, n]
        for d, out_ref in ((0, lf_ref), (1, lb_ref)):
            col = d * N_GDN_HEADS + h
            bcol = jnp.sum(jnp.where(lane16 == col, beta, 0.0), axis=1, keepdims=True)
            gcol = jnp.sum(jnp.where(lane16 == col, gc, 0.0), axis=1, keepdims=True)
            grow = jnp.sum(jnp.where(sub16 == col, gcr, 0.0), axis=0, keepdims=True)
            _, strict = _chunk_masks(d == 1)
            dec = jnp.exp(jnp.where(strict, gcol - grow, NEG_BIG))
            out_ref[0, rows, :] = bcol * kk * dec
        return carry

    lax.fori_loop(0, n_chunks, body, 0, unroll=4)


def _gdn_build_l(layer, k, beta, gc, gc_rows, batch, seq):
    nh2 = 2 * N_GDN_HEADS
    n_chunks = seq // CHUNK
    osp = pl.BlockSpec((1, seq, CHUNK), lambda b, h, l: (b * N_GDN_HEADS + h, 0, 0))
    return _layer_call(
        _gdn_l_kernel, (batch, N_GDN_HEADS),
        [
            pl.BlockSpec((seq, LANES), lambda b, h, l: (b, h)),
            pl.BlockSpec((seq, nh2), lambda b, h, l: (b, 0)),
            pl.BlockSpec((seq, nh2), lambda b, h, l: (b, 0)),
            pl.BlockSpec((1, n_chunks, nh2, CHUNK), lambda b, h, l: (b, 0, 0, 0)),
        ],
        [osp, osp],
        [jax.ShapeDtypeStruct((batch * N_GDN_HEADS, seq, CHUNK), F32)] * 2, name="gdn_build_l",
    )(layer, k, beta, gc, gc_rows)


def _tri_inverse_kernel(l_ref, t_ref):
    sub = lax.broadcasted_iota(jnp.int32, (SUBLANES, LANES), 0)
    nblk = CHUNK // SUBLANES
    for i in range(CHUNK):
        live = i // SUBLANES + 1
        accs = [jnp.zeros((SUBLANES, LANES), F32) for _ in range(live)]
        for j in range(i):
            lij = l_ref[i, pl.ds(j, 1), :]
            for cb in range(j // SUBLANES + 1):
                accs[cb] = accs[cb] + lij * t_ref[j, cb * SUBLANES:(cb + 1) * SUBLANES, :]
        for cb in range(nblk):
            if cb < live - 1:
                val = -accs[cb]
            elif cb == live - 1:
                val = jnp.where(sub == i % SUBLANES, 1.0, 0.0) - accs[cb]
            else:
                val = jnp.zeros((SUBLANES, LANES), F32)
            t_ref[i, cb * SUBLANES:(cb + 1) * SUBLANES, :] = val


def _tri_inverse(l_all):
    g = l_all.shape[-1]
    spec = pl.BlockSpec((CHUNK, CHUNK, LANES), lambda i: (0, 0, i))
    return pl.pallas_call(
        _tri_inverse_kernel, grid=(g // LANES,), in_specs=[spec], out_specs=spec,
        out_shape=jax.ShapeDtypeStruct(l_all.shape, F32),
        compiler_params=_cparams(1), name="gdn_tri_inverse",
    )(l_all)


def _gdn_scan_kernel(l_ref, q_ref, k_ref, v_ref, z_ref, beta_ref, gc_ref, gcrow_ref, tf_ref, tb_ref, nw_ref,
                     o_ref, accf_ref, accb_ref):
    h0 = pl.program_id(1) * GDN_HEADS_PER_STEP
    seq = q_ref.shape[0]
    n_chunks = seq // CHUNK
    dk = GDN_HEAD_DIM
    nt = (((1,), (1,)), ((), ()))
    tn = (((0,), (0,)), ((), ()))
    lane16 = lax.broadcasted_iota(jnp.int32, (CHUNK, 2 * N_GDN_HEADS), 1)
    sub16 = lax.broadcasted_iota(jnp.int32, (2 * N_GDN_HEADS, CHUNK), 0)
    t_refs = (tf_ref, tb_ref)
    acc_refs = (accf_ref, accb_ref)
    streams = [(hh, d) for hh in range(GDN_HEADS_PER_STEP) for d in (0, 1)]

    def chunk_rows(n):
        return pl.ds(pl.multiple_of(n * CHUNK, CHUNK), CHUNK)

    def chunk_of(s, d):
        return (n_chunks - 1 - s) if d == 1 else s

    def precompute(n, hh, d):
        rows = chunk_rows(n)
        hl = slice(hh * LANES, (hh + 1) * LANES)
        qc = q_ref[rows, hl]
        kc = k_ref[rows, hl]
        vc = v_ref[rows, hl]
        qk = lax.dot_general(qc.astype(BF16), kc.astype(BF16), nt, preferred_element_type=F32)
        col = d * N_GDN_HEADS + h0 + hh
        bcol = jnp.sum(jnp.where(lane16 == col, beta_ref[rows, :], 0.0), axis=1, keepdims=True)
        gcol = jnp.sum(jnp.where(lane16 == col, gc_ref[rows, :], 0.0), axis=1, keepdims=True)
        grow = jnp.sum(jnp.where(sub16 == col, gcrow_ref[0, n], 0.0), axis=0, keepdims=True)
        glast = gcol[0:1, :] if d == 1 else gcol[CHUNK - 1:CHUNK, :]
        incl, _ = _chunk_masks(d == 1)
        attn = qk * jnp.exp(jnp.where(incl, gcol - grow, NEG_BIG))
        gam = jnp.exp(gcol)
        x = jnp.concatenate([bcol * vc, (bcol * gam) * kc], axis=1).astype(BF16)
        t = t_refs[d][hh, rows, :]
        th = t.astype(BF16)
        tl = (t - th.astype(F32)).astype(BF16)
        uw = (jnp.dot(th, x, preferred_element_type=F32)
              + jnp.dot(tl, x, preferred_element_type=F32)).astype(BF16)
        awu = jnp.dot(attn.astype(BF16), uw, preferred_element_type=F32)
        qt = gam * qc - awu[:, dk:]
        ktil = (kc * jnp.exp(glast - gcol)).astype(BF16)
        kwu = lax.dot_general(ktil, uw, tn, preferred_element_type=F32)
        mq = jnp.concatenate([kwu[:, dk:], qt], axis=0).astype(BF16)
        return mq, kwu[:, :dk], awu[:, :dk], jnp.broadcast_to(jnp.exp(glast), (1, LANES))

    def body(s, carry):
        states, pre = carry
        nxt = jnp.minimum(s + 1, n_chunks - 1)
        new_pre = tuple(precompute(chunk_of(nxt, d), hh, d) for hh, d in streams)
        new_states = []
        for i, (hh, d) in enumerate(streams):
            mq, r, ol, gl = pre[i]
            st = states[i]
            res = jnp.dot(mq, st.astype(BF16), preferred_element_type=F32)
            acc_refs[d][chunk_rows(chunk_of(s, d)), hh * LANES:(hh + 1) * LANES] = res[dk:, :] + ol
            new_states.append(gl * st - res[:dk, :] + r)
        return tuple(new_states), new_pre

    zero = jnp.zeros((dk, dk), F32)
    first = tuple(precompute(chunk_of(0, d), hh, d) for hh, d in streams)
    lax.fori_loop(0, n_chunks, body, ((zero,) * len(streams), first), unroll=2)

    for hh in range(GDN_HEADS_PER_STEP):
        hl = slice(hh * LANES, (hh + 1) * LANES)
        o = accf_ref[:, hl] + accb_ref[:, hl]
        ms = jnp.mean(o * o, axis=-1, keepdims=True)
        o_ref[:, hl] = (o * lax.rsqrt(ms + NORM_EPS) * nw_ref[0] * _silu(z_ref[:, hl])).astype(BF16)


def _gdn_scan(layer, q, k, v, proj, beta, gc, gc_rows, t_f, t_b, norm_w, batch, seq):
    m = q.shape[0]
    nh2 = 2 * N_GDN_HEADS
    n_chunks = seq // CHUNK
    dk = GDN_HEAD_DIM
    depth = norm_w.shape[0]
    hps = GDN_HEADS_PER_STEP
    wblk = hps * LANES
    zcb = (3 * DIFF_WIDTH + 3 * GDN_WIDTH) // wblk
    hsp = pl.BlockSpec((seq, wblk), lambda b, h, l: (b, h))
    tsp = pl.BlockSpec((hps, seq, CHUNK), lambda b, h, l: (b * (N_GDN_HEADS // hps) + h, 0, 0))
    return _layer_call(
        _gdn_scan_kernel, (batch, N_GDN_HEADS // hps),
        [
            hsp, hsp, hsp,
            pl.BlockSpec((seq, wblk), lambda b, h, l: (b, zcb + h)),
            pl.BlockSpec((seq, nh2), lambda b, h, l: (b, 0)),
            pl.BlockSpec((seq, nh2), lambda b, h, l: (b, 0)),
            pl.BlockSpec((1, n_chunks, nh2, CHUNK), lambda b, h, l: (b, 0, 0, 0)),
            tsp, tsp,
            pl.BlockSpec((1, 1, LANES), lambda b, h, l: (l[0], 0, 0)),
        ],
        hsp,
        jax.ShapeDtypeStruct((m, GDN_WIDTH), BF16),
        scratch_shapes=[pltpu.VMEM((seq, wblk), F32), pltpu.VMEM((seq, wblk), F32)],
        name="gdn_scan",
    )(layer, q, k, v, proj, beta, gc, gc_rows, t_f, t_b, norm_w.reshape(depth, 1, LANES))


def _gated_deltanet(layer, proj, small, conv_w, a_log, dt_bias, norm_w, batch, seq):
    q, k, v, beta, gc = _gdn_prep(layer, proj, small, conv_w, a_log, dt_bias, batch, seq)
    n_chunks = seq // CHUNK
    nh2 = 2 * N_GDN_HEADS
    gc_rows = gc.reshape(batch, n_chunks, CHUNK, nh2).transpose(0, 1, 3, 2)
    l_f, l_b = _gdn_build_l(layer, k, beta, gc, gc_rows, batch, seq)
    g0 = batch * N_GDN_HEADS * n_chunks
    l_all = jnp.concatenate([l_f.reshape(g0, CHUNK, CHUNK).transpose(1, 2, 0),
                             l_b.reshape(g0, CHUNK, CHUNK).transpose(2, 1, 0)], axis=-1)
    t_all = _tri_inverse(l_all)
    t_f = t_all[:, :, :g0].transpose(2, 0, 1).reshape(batch * N_GDN_HEADS, seq, CHUNK)
    t_b = t_all[:, :, g0:].transpose(2, 1, 0).reshape(batch * N_GDN_HEADS, seq, CHUNK)
    return _gdn_scan(layer, q, k, v, proj, beta, gc, gc_rows, t_f, t_b, norm_w, batch, seq)


def _layer(l, x, cos, sin, mod, lam_inits, p, batch, seq):
    m, d = x.shape
    layer = jnp.reshape(l, (1,)).astype(jnp.int32)
    tm = min(1024, seq)
    ident = lambda accs, extras: accs

    h = _norm_mod(layer, x, p["norm_mix_w"], mod, 1, 0, seq)
    (proj,) = _matmul(layer, [h], [p["w_main"]], [0], [0], [], ident, [F32], MAIN_COLS, tm, 1024, "proj_main")
    (small,) = _matmul(layer, [h], [p["w_small"]], [0], [0], [], ident, [F32], LANES, tm, LANES, "proj_small")
    (gates,) = _matmul(layer, [h], [p["w_gates"]], [0], [0], [],
                       lambda accs, extras: [jax.nn.sigmoid(accs[0])], [F32], 2 * d, tm, 1024, "proj_gates")

    lam_init = lam_inits[l]
    lv = p["diff_lambda"][l].astype(F32)
    lam = jnp.exp(jnp.sum(lv[0] * lv[1])) - jnp.exp(jnp.sum(lv[2] * lv[3])) + lam_init
    scalars = jnp.stack([lam, 1.0 - lam_init]).astype(F32)
    dq, dk, dv = _qk_prep(layer, proj, cos, sin, p["diff_qn_w"], p["diff_kn_w"])
    y_diff = _diff_attention(layer, scalars, dq, dk, dv, p["diff_subln_w"], batch, seq)

    y_gdn = _gated_deltanet(layer, proj, small, p["gdn_conv_w"], p["gdn_a_log"], p["gdn_dt_bias"],
                            p["gdn_norm_w"], batch, seq)

    tn = 1024
    (merged,) = _matmul(
        layer, [y_diff, y_gdn], [p["w_branch_diff"], p["w_branch_gdn"]], [0, 1], [0, 0],
        [(gates, (tm, tn), lambda i, j, l: (i, j)), (gates, (tm, tn), lambda i, j, l: (i, j + d // tn))],
        lambda accs, extras: [extras[0] * accs[0] + extras[1] * accs[1]], [BF16], d, tm, tn, "branch_merge")

    def residual(gate_idx):
        return [(x, (tm, tn), lambda i, j, l: (i, j)),
                (mod, (1, 1, 1, 1, tn), lambda i, j, l: (l[0], gate_idx, i * tm // seq, 0, j))]

    res_epi = lambda accs, extras: [extras[0] + extras[1][0, 0, 0] * accs[0]]
    (x,) = _matmul(layer, [merged], [p["w_out"]], [0], [0], residual(2), res_epi, [F32], d, tm, tn, "mixer_out")

    h = _norm_mod(layer, x, p["norm_ffn_w"], mod, 4, 3, seq)
    f = p["ffn_w_down"].shape[1]
    tf = 512
    (act,) = _matmul(layer, [h], [p["ffn_w_up"], p["ffn_w_up"]], [0, 0], [0, f // tf], [],
                     lambda accs, extras: [_silu(accs[0]) * accs[1]], [BF16], f, tm, tf, "ffn_up")
    tm2, tn2 = min(512, seq), 512

    def residual2(gate_idx):
        return [(x, (tm2, tn2), lambda i, j, l: (i, j)),
                (mod, (1, 1, 1, 1, tn2), lambda i, j, l: (l[0], gate_idx, i * tm2 // seq, 0, j))]

    (x,) = _matmul(layer, [act], [p["ffn_w_down"]], [0], [0], residual2(5), res_epi, [F32], d, tm2, tn2, "ffn_down")
    return x


def kernel(x, c, positions, ada_w, ada_b, norm_mix_w, norm_ffn_w, w_in, diff_qn_w, diff_kn_w, diff_lambda,
           diff_subln_w, gdn_conv_w, gdn_a_log, gdn_dt_bias, gdn_norm_w, w_branch_diff, w_branch_gdn, w_out,
           ffn_w_up, ffn_w_down):
    batch, seq, d = x.shape
    depth = ada_w.shape[0]
    mod = _ada_modulation(c, ada_w, ada_b)
    cos, sin = _rope_tables(positions)
    lam_inits = jnp.asarray([0.8 - 0.6 * math.exp(-0.3 * i) for i in range(depth)], F32)
    small_w = jnp.pad(w_in[:, :, MAIN_COLS:MAIN_COLS + SMALL_COLS], ((0, 0), (0, 0), (0, LANES - SMALL_COLS)))
    p = {
        "norm_mix_w": norm_mix_w.reshape(depth, 1, d), "norm_ffn_w": norm_ffn_w.reshape(depth, 1, d),
        "w_main": w_in[:, :, :MAIN_COLS].astype(BF16),
        "w_small": small_w.astype(BF16),
        "w_gates": w_in[:, :, MAIN_COLS + SMALL_COLS:].astype(BF16),
        "diff_qn_w": diff_qn_w, "diff_kn_w": diff_kn_w, "diff_lambda": diff_lambda, "diff_subln_w": diff_subln_w,
        "gdn_conv_w": gdn_conv_w, "gdn_a_log": gdn_a_log, "gdn_dt_bias": gdn_dt_bias, "gdn_norm_w": gdn_norm_w,
        "w_branch_diff": w_branch_diff.astype(BF16), "w_branch_gdn": w_branch_gdn.astype(BF16),
        "w_out": w_out.astype(BF16), "ffn_w_up": ffn_w_up.astype(BF16), "ffn_w_down": ffn_w_down.astype(BF16),
    }
    body = lambda l, xc: _layer(l, xc, cos, sin, mod, lam_inits, p, batch, seq)
    out = lax.fori_loop(0, depth, body, x.reshape(batch * seq, d))
    return out.reshape(batch, seq, d)
```

```python
import functools
import math

import jax
import jax.numpy as jnp
from jax import lax
from jax.experimental import pallas as pl
from jax.experimental.pallas import tpu as pltpu

F32 = jnp.float32
BF16 = jnp.bfloat16

N_DIFF_HEADS = 8
DIFF_HEAD_DIM = 64
DIFF_WIDTH = N_DIFF_HEADS * 2 * DIFF_HEAD_DIM
N_GDN_HEADS = 8
GDN_HEAD_DIM = 128
GDN_WIDTH = N_GDN_HEADS * GDN_HEAD_DIM
CONV_WIDTH = 5
CHUNK = 64
ROPE_THETA = 10000.0
NORM_EPS = 1e-6
N_MOD = 6
LANES = 128
SUBLANES = 8
NEG_BIG = -1e30
GDN_PRE_CHUNKS = 8
GDN_SCAN_CHUNKS = 8
ATTN_Q_ROWS = 2048
ATTN_SUB_ROWS = 256

MAIN_COLS = 3 * DIFF_WIDTH + 4 * GDN_WIDTH
SMALL_COLS = 4 * N_GDN_HEADS
VMEM_LIMIT = 48 * 1024 * 1024


def _silu(x):
    return x * jax.nn.sigmoid(x)


def _cparams(n_axes, vmem=VMEM_LIMIT):
    return pltpu.CompilerParams(dimension_semantics=("arbitrary",) * n_axes, vmem_limit_bytes=vmem)


def _layer_call(kernel, grid, in_specs, out_specs, out_shape, scratch_shapes=(), name=None):
    return pl.pallas_call(
        kernel,
        grid_spec=pltpu.PrefetchScalarGridSpec(
            num_scalar_prefetch=1, grid=grid, in_specs=in_specs, out_specs=out_specs,
            scratch_shapes=scratch_shapes),
        out_shape=out_shape,
        compiler_params=_cparams(len(grid)),
        name=name,
    )


def _ada_kernel(c_ref, w_ref, b_ref, o_ref):
    c = c_ref[...]
    a = _silu(c).astype(BF16)
    o_ref[0, 0] = jnp.dot(a, w_ref[0].astype(BF16), preferred_element_type=F32) + b_ref[0]


def _ada_modulation(c, ada_w, ada_b):
    depth, d, n6 = ada_w.shape
    b = c.shape[0]
    rows = -(-b // SUBLANES) * SUBLANES
    c_pad = jnp.pad(c, ((0, rows - b), (0, 0)))
    tn = 1024
    per = d // tn
    out = pl.pallas_call(
        _ada_kernel,
        grid=(depth, n6 // tn),
        in_specs=[
            pl.BlockSpec((rows, d), lambda l, j: (0, 0)),
            pl.BlockSpec((1, d, tn), lambda l, j: (l, 0, j)),
            pl.BlockSpec((1, 1, tn), lambda l, j: (l, 0, j)),
        ],
        out_specs=pl.BlockSpec((1, 1, rows, tn), lambda l, j: (l, j // per, 0, j % per)),
        out_shape=jax.ShapeDtypeStruct((depth, N_MOD, rows, d), F32),
        compiler_params=_cparams(2),
        name="ada_modulation",
    )(c_pad, ada_w, ada_b.reshape(depth, 1, n6))
    return out[:, :, :b].reshape(depth, N_MOD, b, 1, d)


def _rope_kernel(ang_ref, cos_ref, sin_ref):
    ang = ang_ref[...]
    lane = lax.broadcasted_iota(jnp.int32, ang.shape, 1)
    first = (lane % DIFF_HEAD_DIM) < (DIFF_HEAD_DIM // 2)
    cos_ref[...] = jnp.cos(ang)
    s = jnp.sin(ang)
    sin_ref[...] = jnp.where(first, -s, s)


def _rope_tables(positions):
    m = positions.size
    half = DIFF_HEAD_DIM // 2
    inv_freq = ROPE_THETA ** (-jnp.arange(half, dtype=F32) * 2.0 / DIFF_HEAD_DIM)
    ang = positions.reshape(m, 1).astype(F32) * jnp.tile(inv_freq, LANES // half)[None, :]
    tm = min(1024, m)
    spec = pl.BlockSpec((tm, LANES), lambda i: (i, 0))
    return pl.pallas_call(
        _rope_kernel, grid=(m // tm,), in_specs=[spec], out_specs=[spec, spec],
        out_shape=[jax.ShapeDtypeStruct((m, LANES), F32)] * 2,
        compiler_params=_cparams(1), name="rope_tables",
    )(ang)


def _norm_mod_kernel(l_ref, x_ref, w_ref, sc_ref, sh_ref, o_ref):
    x = x_ref[...]
    ms = jnp.mean(x * x, axis=-1, keepdims=True)
    y = x * lax.rsqrt(ms + NORM_EPS) * w_ref[0]
    o_ref[...] = (y * (1.0 + sc_ref[0, 0, 0]) + sh_ref[0, 0, 0]).astype(BF16)


def _norm_mod(layer, x, norm_w, mod, scale_idx, shift_idx, seq):
    m, d = x.shape
    tm = min(512, seq)
    return _layer_call(
        _norm_mod_kernel, (m // tm,),
        [
            pl.BlockSpec((tm, d), lambda i, l: (i, 0)),
            pl.BlockSpec((1, 1, d), lambda i, l: (l[0], 0, 0)),
            pl.BlockSpec((1, 1, 1, 1, d), lambda i, l: (l[0], scale_idx, i * tm // seq, 0, 0)),
            pl.BlockSpec((1, 1, 1, 1, d), lambda i, l: (l[0], shift_idx, i * tm // seq, 0, 0)),
        ],
        pl.BlockSpec((tm, d), lambda i, l: (i, 0)),
        jax.ShapeDtypeStruct((m, d), BF16), name="norm_mod",
    )(layer, x, norm_w, mod, mod)


def _matmul(layer, xs, ws, w_x, w_off, extras, epilogue, out_dtypes, n, tm, tn, name):
    m = xs[0].shape[0]
    nx, nw, ne = len(xs), len(ws), len(extras)
    cast_w = ws[0].dtype == F32
    assert all((w.dtype == F32) == cast_w for w in ws)
    ij = (lambda a, b: (b, a)) if cast_w else (lambda a, b: (a, b))
    in_specs = [pl.BlockSpec((tm, x.shape[1]), lambda a, b, l: (ij(a, b)[0], 0)) for x in xs]
    for w, off in zip(ws, w_off):
        in_specs.append(pl.BlockSpec((1, w.shape[1], tn), lambda a, b, l, off=off: (l[0], 0, ij(a, b)[1] + off)))
    for _, bs, imap in extras:
        in_specs.append(pl.BlockSpec(bs, lambda a, b, l, imap=imap: imap(*ij(a, b), l)))
    out_specs = [pl.BlockSpec((tm, tn), lambda a, b, l: ij(a, b)) for _ in out_dtypes]
    out_shape = [jax.ShapeDtypeStruct((m, n), dt) for dt in out_dtypes]
    scratch = [pltpu.VMEM((w.shape[1], tn), BF16) for w in ws] if cast_w else []

    def kern(l_ref, *refs):
        x_refs, w_refs = refs[:nx], refs[nx:nx + nw]
        e_refs, o_refs = refs[nx + nw:nx + nw + ne], refs[nx + nw + ne:nx + nw + ne + len(out_dtypes)]
        if cast_w:
            wb_refs = refs[nx + nw + ne + len(out_dtypes):]

            @pl.when(pl.program_id(1) == 0)
            def _():
                for w_ref, wb_ref in zip(w_refs, wb_refs):
                    wb_ref[...] = w_ref[0].astype(BF16)

            w_tiles = [wb_ref[...] for wb_ref in wb_refs]
        else:
            w_tiles = [w_ref[0] for w_ref in w_refs]
        accs = [jnp.dot(x_refs[xi][...], w, preferred_element_type=F32) for xi, w in zip(w_x, w_tiles)]
        outs = epilogue(accs, [e[...] for e in e_refs])
        for o_ref, v in zip(o_refs, outs):
            o_ref[...] = v.astype(o_ref.dtype)

    grid = (n // tn, m // tm) if cast_w else (m // tm, n // tn)
    return _layer_call(kern, grid, in_specs, out_specs, out_shape, scratch_shapes=scratch, name=name)(
        layer, *xs, *ws, *[e[0] for e in extras])


def _group_sumsq(x, group_ones):
    sq = x * x
    hi = sq.astype(BF16)
    lo = (sq - hi.astype(F32)).astype(BF16)
    return (jnp.dot(hi, group_ones, preferred_element_type=F32)
            + jnp.dot(lo, group_ones, preferred_element_type=F32))


def _qk_prep_kernel(l_ref, q_ref, k_ref, v_ref, cos_ref, sin_ref, qw_ref, kw_ref, qo_ref, ko_ref, vo_ref):
    cos = cos_ref[...]
    sin = sin_ref[...]
    row = lax.broadcasted_iota(jnp.int32, (LANES, LANES), 0) // DIFF_HEAD_DIM
    col = lax.broadcasted_iota(jnp.int32, (LANES, LANES), 1) // DIFF_HEAD_DIM
    ones = (row == col).astype(BF16)
    lane = lax.broadcasted_iota(jnp.int32, cos.shape, 1)
    first = (lane % DIFF_HEAD_DIM) < (DIFF_HEAD_DIM // 2)
    half = DIFF_HEAD_DIM // 2

    def norm_rope(x, w, scale):
        ms = _group_sumsq(x, ones) * (1.0 / DIFF_HEAD_DIM)
        y = x * lax.rsqrt(ms + NORM_EPS) * w
        partner = jnp.where(first, pltpu.roll(y, LANES - half, 1), pltpu.roll(y, half, 1))
        return (y * cos + partner * sin) * scale

    q_scale = DIFF_HEAD_DIM ** -0.5 * math.log2(math.e)
    ones_blk = jnp.ones((cos.shape[0], LANES), BF16)
    for h in range(N_DIFF_HEADS):
        sl = slice(h * LANES, (h + 1) * LANES)
        qo_ref[:, sl] = norm_rope(q_ref[:, sl], qw_ref[0, :, sl], q_scale).astype(BF16)
        ko_ref[:, sl] = norm_rope(k_ref[:, sl], kw_ref[0, :, sl], 1.0).astype(BF16)
        vo_ref[:, 2 * h * LANES:(2 * h + 1) * LANES] = v_ref[:, sl].astype(BF16)
        vo_ref[:, (2 * h + 1) * LANES:(2 * h + 2) * LANES] = ones_blk


def _qk_prep(layer, proj, cos, sin, qn_w, kn_w):
    m = proj.shape[0]
    tm = min(512, m)
    w = DIFF_WIDTH
    rep = w // DIFF_HEAD_DIM
    depth = qn_w.shape[0]
    qw = jnp.tile(qn_w, (1, rep)).reshape(depth, 1, w)
    kw = jnp.tile(kn_w, (1, rep)).reshape(depth, 1, w)
    blk = lambda c: pl.BlockSpec((tm, w), lambda i, l, c=c: (i, c))
    tab = pl.BlockSpec((tm, LANES), lambda i, l: (i, 0))
    wsp = pl.BlockSpec((1, 1, w), lambda i, l: (l[0], 0, 0))
    osp = pl.BlockSpec((tm, w), lambda i, l: (i, 0))
    vsp = pl.BlockSpec((tm, 2 * w), lambda i, l: (i, 0))
    return _layer_call(
        _qk_prep_kernel, (m // tm,), [blk(0), blk(1), blk(2), tab, tab, wsp, wsp], [osp, osp, vsp],
        [jax.ShapeDtypeStruct((m, w), BF16)] * 2 + [jax.ShapeDtypeStruct((m, 2 * w), BF16)], name="diff_qk_prep",
    )(layer, proj, proj, proj, cos, sin, qw, kw)


def _diff_attn_kernel(l_ref, sc_ref, q_ref, k_ref, v_ref, w_ref, o_ref):
    lam = sc_ref[0]
    out_scale = sc_ref[1]
    k = k_ref[...]
    nt = (((1,), (1,)), ((), ()))
    sub = min(ATTN_SUB_ROWS, q_ref.shape[0])

    def branch(qm):
        s = lax.dot_general(qm, k, nt, preferred_element_type=F32)
        p = jnp.exp2(s - jnp.max(s, axis=-1, keepdims=True)).astype(BF16)
        ov = jnp.dot(p, v_ref[...], preferred_element_type=F32)
        return ov[:, :LANES] / ov[:, LANES:]

    for r in range(q_ref.shape[0] // sub):
        rows = slice(r * sub, (r + 1) * sub)
        q = q_ref[rows, :]
        lane = lax.broadcasted_iota(jnp.int32, q.shape, 1)
        zero = jnp.zeros_like(q)
        q0 = jnp.where(lane < DIFF_HEAD_DIM, q, zero)
        q1 = jnp.where(lane >= DIFF_HEAD_DIM, q, zero)
        o = branch(q0) - lam * branch(q1)
        ms = jnp.mean(o * o, axis=-1, keepdims=True)
        o_ref[rows, :] = (o * lax.rsqrt(ms + NORM_EPS) * w_ref[0] * out_scale).astype(BF16)


def _diff_attention(layer, scalars, q, k, v, subln_w, batch, seq):
    m = q.shape[0]
    tq = min(ATTN_Q_ROWS, seq)
    nq = seq // tq
    depth = subln_w.shape[0]
    return _layer_call(
        _diff_attn_kernel, (batch, N_DIFF_HEADS, nq),
        [
            pl.BlockSpec(memory_space=pltpu.SMEM),
            pl.BlockSpec((tq, LANES), lambda b, h, i, l: (b * nq + i, h)),
            pl.BlockSpec((seq, LANES), lambda b, h, i, l: (b, h)),
            pl.BlockSpec((seq, 2 * LANES), lambda b, h, i, l: (b, h)),
            pl.BlockSpec((1, 1, LANES), lambda b, h, i, l: (l[0], 0, 0)),
        ],
        pl.BlockSpec((tq, LANES), lambda b, h, i, l: (b * nq + i, h)),
        jax.ShapeDtypeStruct((m, DIFF_WIDTH), BF16), name="diff_attention",
    )(layer, scalars, q, k, v, subln_w.reshape(depth, 1, LANES))


def _split3(x):
    hi = x.astype(BF16)
    r1 = x - hi.astype(F32)
    mid = r1.astype(BF16)
    lo = (r1 - mid.astype(F32)).astype(BF16)
    return hi, mid, lo


def _gdn_prep_kernel(l_ref, main_ref, prev_ref, next_ref, small_ref, cw_ref, alog_ref, dtb_ref,
                     q_ref, k_ref, kt_ref, v_ref, beta_ref, gc_ref):
    t = pl.program_id(1)
    nt = pl.num_programs(1)
    ts = main_ref.shape[0]
    pad = CONV_WIDTH // 2
    main = main_ref[...]
    top = jnp.where(t > 0, prev_ref[SUBLANES - pad:, :], 0.0)
    bot = jnp.where(t < nt - 1, next_ref[:pad, :], 0.0)
    xe = jnp.concatenate([top, main, bot], axis=0)
    cw = cw_ref[0]
    acc = cw[0:1, :] * xe[0:ts, :]
    for j in range(1, CONV_WIDTH):
        acc = acc + cw[j:j + 1, :] * xe[j:j + ts, :]
    y = _silu(acc)
    for h in range(N_GDN_HEADS):
        sl = slice(h * LANES, (h + 1) * LANES)
        qh = y[:, sl]
        kh = y[:, GDN_WIDTH + h * LANES:GDN_WIDTH + (h + 1) * LANES]
        q_ref[:, sl] = qh * (lax.rsqrt(jnp.sum(qh * qh, axis=-1, keepdims=True) + NORM_EPS)
                             * GDN_HEAD_DIM ** -0.5)
        kn = kh * lax.rsqrt(jnp.sum(kh * kh, axis=-1, keepdims=True) + NORM_EPS)
        k_ref[:, sl] = kn
        kt_ref[sl, :] = kn.T
    v_ref[...] = y[:, 2 * GDN_WIDTH:]

    nh2 = 2 * N_GDN_HEADS
    small = small_ref[...]
    beta_ref[...] = jax.nn.sigmoid(small[:, :nh2])
    g = -jnp.exp(alog_ref[0]) * jax.nn.softplus(small[:, nh2:2 * nh2] + dtb_ref[0])
    r = lax.broadcasted_iota(jnp.int32, (ts, ts), 0)
    c = lax.broadcasted_iota(jnp.int32, (ts, ts), 1)
    same = (r // CHUNK) == (c // CHUNK)
    pre = (same & (c <= r)).astype(BF16)
    suf = (same & (c >= r)).astype(BF16)
    parts = _split3(g)
    cs_f = sum(jnp.dot(pre, p, preferred_element_type=F32) for p in parts)
    cs_b = sum(jnp.dot(suf, p, preferred_element_type=F32) for p in parts)
    lane = lax.broadcasted_iota(jnp.int32, g.shape, 1)
    gc_ref[...] = jnp.where(lane < N_GDN_HEADS, cs_f, cs_b)


def _gdn_prep(layer, proj, small, conv_w, a_log, dt_bias, batch, seq):
    m = proj.shape[0]
    ts = min(256, seq)
    nts = seq // ts
    w3 = 3 * GDN_WIDTH
    cb = (3 * DIFF_WIDTH) // w3
    assert cb * w3 == 3 * DIFF_WIDTH
    depth = conv_w.shape[0]
    nh2 = 2 * N_GDN_HEADS
    hb = ts // SUBLANES
    last = m // SUBLANES - 1
    osp = pl.BlockSpec((ts, GDN_WIDTH), lambda b, t, l: (b * nts + t, 0))
    ssp = pl.BlockSpec((ts, nh2), lambda b, t, l: (b * nts + t, 0))
    return _layer_call(
        _gdn_prep_kernel, (batch, nts),
        [
            pl.BlockSpec((ts, w3), lambda b, t, l: (b * nts + t, cb)),
            pl.BlockSpec((SUBLANES, w3), lambda b, t, l: (jnp.maximum((b * nts + t) * hb - 1, 0), cb)),
            pl.BlockSpec((SUBLANES, w3), lambda b, t, l: (jnp.minimum((b * nts + t + 1) * hb, last), cb)),
            pl.BlockSpec((ts, LANES), lambda b, t, l: (b * nts + t, 0)),
            pl.BlockSpec((1, CONV_WIDTH, w3), lambda b, t, l: (l[0], 0, 0)),
            pl.BlockSpec((1, 1, nh2), lambda b, t, l: (l[0], 0, 0)),
            pl.BlockSpec((1, 1, nh2), lambda b, t, l: (l[0], 0, 0)),
        ],
        [osp, osp, pl.BlockSpec((GDN_WIDTH, ts), lambda b, t, l: (0, b * nts + t)), osp, ssp, ssp],
        [jax.ShapeDtypeStruct((m, GDN_WIDTH), F32)] * 2 + [jax.ShapeDtypeStruct((GDN_WIDTH, m), F32)]
        + [jax.ShapeDtypeStruct((m, GDN_WIDTH), F32)] + [jax.ShapeDtypeStruct((m, nh2), F32)] * 2,
        name="gdn_prep",
    )(layer, proj, proj, proj, small, conv_w, a_log.reshape(depth, 1, nh2), dt_bias.reshape(depth, 1, nh2))


def _chunk_masks(backward):
    r = lax.broadcasted_iota(jnp.int32, (CHUNK, CHUNK), 0)
    c = lax.broadcasted_iota(jnp.int32, (CHUNK, CHUNK), 1)
    if backward:
        return r <= c, r < c
    return r >= c, r > c


def _gdn_l_kernel(l_ref, k_ref, beta_ref, gc_ref, gcrow_ref, lf_ref, lb_ref):
    h = pl.program_id(1)
    n_chunks = k_ref.shape[0] // CHUNK
    nt = (((1,), (1,)), ((), ()))
    lane16 = lax.broadcasted_iota(jnp.int32, (CHUNK, 2 * N_GDN_HEADS), 1)
    sub16 = lax.broadcasted_iota(jnp.int32, (2 * N_GDN_HEADS, CHUNK), 0)

    def body(n, carry):
        rows = pl.ds(pl.multiple_of(n * CHUNK, CHUNK), CHUNK)
        kb = k_ref[rows, :].astype(BF16)
        kk = lax.dot_general(kb, kb, nt, preferred_element_type=F32)
        beta = beta_ref[rows, :]
        gc = gc_ref[rows, :]
        gcr = gcrow_ref[0, n]
        for d, out_ref in ((0, lf_ref), (1, lb_ref)):
            col = d * N_GDN_HEADS + h
            bcol = jnp.sum(jnp.where(lane16 == col, beta, 0.0), axis=1, keepdims=True)
            gcol = jnp.sum(jnp.where(lane16 == col, gc, 0.0), axis=1, keepdims=True)
            grow = jnp.sum(jnp.where(sub16 == col, gcr, 0.0), axis=0, keepdims=True)
            _, strict = _chunk_masks(d == 1)
            dec = jnp.exp(jnp.where(strict, gcol - grow, NEG_BIG))
            out_ref[0, rows, :] = bcol * kk * dec
        return carry

    lax.fori_loop(0, n_chunks, body, 0, unroll=4)


def _gdn_build_l(layer, k, beta, gc, gc_rows, batch, seq):
    nh2 = 2 * N_GDN_HEADS
    n_chunks = seq // CHUNK
    osp = pl.BlockSpec((1, seq, CHUNK), lambda b, h, l: (b * N_GDN_HEADS + h, 0, 0))
    return _layer_call(
        _gdn_l_kernel, (batch, N_GDN_HEADS),
        [
            pl.BlockSpec((seq, LANES), lambda b, h, l: (b, h)),
            pl.BlockSpec((seq, nh2), lambda b, h, l: (b, 0)),
            pl.BlockSpec((seq, nh2), lambda b, h, l: (b, 0)),
            pl.BlockSpec((1, n_chunks, nh2, CHUNK), lambda b, h, l: (b, 0, 0, 0)),
        ],
        [osp, osp],
        [jax.ShapeDtypeStruct((batch * N_GDN_HEADS, seq, CHUNK), F32)] * 2, name="gdn_build_l",
    )(layer, k, beta, gc, gc_rows)


def _tri_inverse_kernel(l_ref, t_ref):
    sub = lax.broadcasted_iota(jnp.int32, (SUBLANES, LANES), 0)
    nblk = CHUNK // SUBLANES
    for i in range(CHUNK):
        live = i // SUBLANES + 1
        accs = [jnp.zeros((SUBLANES, LANES), F32) for _ in range(live)]
        for j in range(i):
            lij = l_ref[i, pl.ds(j, 1), :]
            for cb in range(j // SUBLANES + 1):
                accs[cb] = accs[cb] + lij * t_ref[j, cb * SUBLANES:(cb + 1) * SUBLANES, :]
        for cb in range(nblk):
            if cb < live - 1:
                val = -accs[cb]
            elif cb == live - 1:
                val = jnp.where(sub == i % SUBLANES, 1.0, 0.0) - accs[cb]
            else:
                val = jnp.zeros((SUBLANES, LANES), F32)
            t_ref[i, cb * SUBLANES:(cb + 1) * SUBLANES, :] = val


def _tri_inverse(l_all):
    g = l_all.shape[-1]
    spec = pl.BlockSpec((CHUNK, CHUNK, LANES), lambda i: (0, 0, i))
    return pl.pallas_call(
        _tri_inverse_kernel, grid=(g // LANES,), in_specs=[spec], out_specs=spec,
        out_shape=jax.ShapeDtypeStruct(l_all.shape, F32),
        compiler_params=_cparams(1), name="gdn_tri_inverse",
    )(l_all)


def _gdn_pre_kernel(l_ref, q_ref, k_ref, kt_ref, v_ref, beta_ref, gc_ref, gcrow_ref, gcpair_ref, tf_ref, tb_ref,
                    mqf_ref, mqb_ref, rf_ref, rb_ref, olf_ref, olb_ref, glf_ref, glb_ref):
    h = pl.program_id(1)
    n_local = q_ref.shape[0] // CHUNK
    dk = GDN_HEAD_DIM
    nt = (((1,), (1,)), ((), ()))
    lane16 = lax.broadcasted_iota(jnp.int32, (CHUNK, 2 * N_GDN_HEADS), 1)
    sub16 = lax.broadcasted_iota(jnp.int32, (2 * N_GDN_HEADS, CHUNK), 0)
    sub16p = lax.broadcasted_iota(jnp.int32, (2 * N_GDN_HEADS, LANES), 0)
    lane_half = lax.broadcasted_iota(jnp.int32, (1, LANES), 1) // CHUNK
    outs = ((tf_ref, mqf_ref, rf_ref, olf_ref, glf_ref), (tb_ref, mqb_ref, rb_ref, olb_ref, glb_ref))
    per_pair = LANES // CHUNK
    for c in range(n_local):
        rows = slice(c * CHUNK, (c + 1) * CHUNK)
        pair, half = c // per_pair, c % per_pair
        qc = q_ref[rows, :]
        kc = k_ref[rows, :]
        vc = v_ref[rows, :]
        kt_pair = kt_ref[:, pair * LANES:(pair + 1) * LANES]
        qk = lax.dot_general(qc.astype(BF16), kc.astype(BF16), nt, preferred_element_type=F32)
        for d, (t_ref, mq_ref, r_ref, ol_ref, gl_ref) in enumerate(outs):
            pos = (n_local - 1 - c) if d == 1 else c
            col = d * N_GDN_HEADS + h
            bcol = jnp.sum(jnp.where(lane16 == col, beta_ref[rows, :], 0.0), axis=1, keepdims=True)
            gcol = jnp.sum(jnp.where(lane16 == col, gc_ref[rows, :], 0.0), axis=1, keepdims=True)
            grow = jnp.sum(jnp.where(sub16 == col, gcrow_ref[0, c], 0.0), axis=0, keepdims=True)
            glast = gcol[0:1, :] if d == 1 else gcol[CHUNK - 1:CHUNK, :]
            incl, _ = _chunk_masks(d == 1)
            attn = qk * jnp.exp(jnp.where(incl, gcol - grow, NEG_BIG))
            gam = jnp.exp(gcol)
            x = jnp.concatenate([bcol * vc, (bcol * gam) * kc], axis=1).astype(BF16)
            t = t_ref[0, rows, :]
            th = t.astype(BF16)
            tl = (t - th.astype(F32)).astype(BF16)
            uw = (jnp.dot(th, x, preferred_element_type=F32)
                  + jnp.dot(tl, x, preferred_element_type=F32)).astype(BF16)
            awu = jnp.dot(attn.astype(BF16), uw, preferred_element_type=F32)
            qt = gam * qc - awu[:, dk:]
            grow_pair = jnp.sum(jnp.where(sub16p == col, gcpair_ref[0, pair], 0.0), axis=0, keepdims=True)
            tail = jnp.exp(jnp.where(lane_half == half, glast - grow_pair, NEG_BIG))
            ktil_t = (kt_pair * tail).astype(BF16)
            kwu = jnp.dot(ktil_t, jnp.concatenate([uw] * per_pair, axis=0), preferred_element_type=F32)
            mq_ref[0, pos, 0:dk, :] = kwu[:, dk:].astype(BF16)
            mq_ref[0, pos, dk:dk + CHUNK, :] = qt.astype(BF16)
            r_ref[0, pos] = kwu[:, :dk].astype(BF16)
            ol_ref[0, pos] = awu[:, :dk].astype(BF16)
            gl_ref[0, pos] = jnp.broadcast_to(jnp.exp(glast), (1, LANES))


def _gdn_pre(layer, q, k, k_t, v, beta, gc, gc_rows, gc_pairs, t_f, t_b, batch, seq):
    nh2 = 2 * N_GDN_HEADS
    n_chunks = seq // CHUNK
    cg = min(GDN_PRE_CHUNKS, n_chunks)
    ng = n_chunks // cg
    rows = cg * CHUNK
    dk = GDN_HEAD_DIM
    bh = batch * N_GDN_HEADS
    hsp = pl.BlockSpec((rows, LANES), lambda b, h, g, l: (b * ng + g, h))
    ssp = pl.BlockSpec((rows, nh2), lambda b, h, g, l: (b * ng + g, 0))
    tsp = pl.BlockSpec((1, rows, CHUNK), lambda b, h, g, l: (b * N_GDN_HEADS + h, g, 0))

    def osp(r, mirrored):
        if mirrored:
            return pl.BlockSpec((1, cg, r, LANES), lambda b, h, g, l: (b * N_GDN_HEADS + h, ng - 1 - g, 0, 0))
        return pl.BlockSpec((1, cg, r, LANES), lambda b, h, g, l: (b * N_GDN_HEADS + h, g, 0, 0))

    shapes = [((dk + CHUNK), BF16), (dk, BF16), (CHUNK, BF16), (1, F32)]
    out_specs, out_shape = [], []
    for r, dt in shapes:
        for mirrored in (False, True):
            out_specs.append(osp(r, mirrored))
            out_shape.append(jax.ShapeDtypeStruct((bh, n_chunks, r, LANES), dt))
    return _layer_call(
        _gdn_pre_kernel, (batch, N_GDN_HEADS, ng),
        [hsp, hsp, pl.BlockSpec((LANES, rows), lambda b, h, g, l: (h, b * ng + g)), hsp, ssp, ssp,
         pl.BlockSpec((1, cg, nh2, CHUNK), lambda b, h, g, l: (b, g, 0, 0)),
         pl.BlockSpec((1, rows // LANES, nh2, LANES), lambda b, h, g, l: (b, g, 0, 0)),
         tsp, tsp],
        out_specs, out_shape, name="gdn_chunk_pre",
    )(layer, q, k, k_t, v, beta, gc, gc_rows, gc_pairs, t_f, t_b)


def _gdn_state_kernel(l_ref, mqf_ref, mqb_ref, rf_ref, rb_ref, olf_ref, olb_ref, glf_ref, glb_ref,
                      of_ref, ob_ref, st_ref):
    g = pl.program_id(1)
    n_heads, n_local = mqf_ref.shape[0], mqf_ref.shape[1]
    dk = GDN_HEAD_DIM

    @pl.when(g == 0)
    def _():
        st_ref[...] = jnp.zeros_like(st_ref)

    zero = jnp.zeros((dk, dk), BF16)
    states = [(st_ref[2 * hh], st_ref[2 * hh + 1]) for hh in range(n_heads)]
    for c in range(n_local):
        for hh in range(n_heads):
            sf, sb = states[hh]
            mq = jnp.concatenate([mqf_ref[hh, c], mqb_ref[hh, c]], axis=1)
            bd = jnp.concatenate([jnp.concatenate([sf.astype(BF16), zero], axis=1),
                                  jnp.concatenate([zero, sb.astype(BF16)], axis=1)], axis=0)
            res = jnp.dot(mq, bd, preferred_element_type=F32)
            hl = slice(hh * LANES, (hh + 1) * LANES)
            of_ref[c * CHUNK:(c + 1) * CHUNK, hl] = res[dk:, :dk] + olf_ref[hh, c].astype(F32)
            cb = n_local - 1 - c
            ob_ref[cb * CHUNK:(cb + 1) * CHUNK, hl] = res[dk:, dk:] + olb_ref[hh, c].astype(F32)
            states[hh] = (glf_ref[hh, c] * sf - res[:dk, :dk] + rf_ref[hh, c].astype(F32),
                          glb_ref[hh, c] * sb - res[:dk, dk:] + rb_ref[hh, c].astype(F32))
    for hh in range(n_heads):
        st_ref[2 * hh] = states[hh][0]
        st_ref[2 * hh + 1] = states[hh][1]


def _gdn_state(layer, pre, batch, seq):
    n_chunks = seq // CHUNK
    cg = min(GDN_SCAN_CHUNKS, n_chunks)
    ng = n_chunks // cg
    nh = N_GDN_HEADS
    dk = GDN_HEAD_DIM
    m = batch * seq
    in_specs = [pl.BlockSpec((nh, cg) + a.shape[2:], lambda b, g, l: (b, g, 0, 0)) for a in pre]
    return _layer_call(
        _gdn_state_kernel, (batch, ng), in_specs,
        [pl.BlockSpec((cg * CHUNK, GDN_WIDTH), lambda b, g, l: (b * ng + g, 0)),
         pl.BlockSpec((cg * CHUNK, GDN_WIDTH), lambda b, g, l: (b * ng + ng - 1 - g, 0))],
        [jax.ShapeDtypeStruct((m, GDN_WIDTH), F32)] * 2,
        scratch_shapes=[pltpu.VMEM((2 * nh, dk, dk), F32)], name="gdn_state_scan",
    )(layer, *pre)


def _gdn_out_kernel(l_ref, of_ref, ob_ref, z_ref, nw_ref, y_ref):
    for h in range(N_GDN_HEADS):
        hl = slice(h * LANES, (h + 1) * LANES)
        o = of_ref[:, hl] + ob_ref[:, hl]
        ms = jnp.mean(o * o, axis=-1, keepdims=True)
        y_ref[:, hl] = (o * lax.rsqrt(ms + NORM_EPS) * nw_ref[0] * _silu(z_ref[:, hl])).astype(BF16)


def _gdn_out(layer, o_f, o_b, proj, norm_w):
    m = o_f.shape[0]
    tm = min(512, m)
    depth = norm_w.shape[0]
    zcb = (3 * DIFF_WIDTH + 3 * GDN_WIDTH) // GDN_WIDTH
    osp = pl.BlockSpec((tm, GDN_WIDTH), lambda i, l: (i, 0))
    return _layer_call(
        _gdn_out_kernel, (m // tm,),
        [osp, osp, pl.BlockSpec((tm, GDN_WIDTH), lambda i, l: (i, zcb)),
         pl.BlockSpec((1, 1, LANES), lambda i, l: (l[0], 0, 0))],
        osp, jax.ShapeDtypeStruct((m, GDN_WIDTH), BF16), name="gdn_out",
    )(layer, o_f, o_b, proj, norm_w.reshape(depth, 1, LANES))


def _gated_deltanet(layer, proj, small, conv_w, a_log, dt_bias, norm_w, batch, seq):
    q, k, k_t, v, beta, gc = _gdn_prep(layer, proj, small, conv_w, a_log, dt_bias, batch, seq)
    n_chunks = seq // CHUNK
    nh2 = 2 * N_GDN_HEADS
    gc_rows = gc.reshape(batch, n_chunks, CHUNK, nh2).transpose(0, 1, 3, 2)
    gc_pairs = gc.reshape(batch, seq // LANES, LANES, nh2).transpose(0, 1, 3, 2)
    l_f, l_b = _gdn_build_l(layer, k, beta, gc, gc_rows, batch, seq)
    g0 = batch * N_GDN_HEADS * n_chunks
    l_all = jnp.concatenate([l_f.reshape(g0, CHUNK, CHUNK).transpose(1, 2, 0),
                             l_b.reshape(g0, CHUNK, CHUNK).transpose(2, 1, 0)], axis=-1)
    t_all = _tri_inverse(l_all)
    t_f = t_all[:, :, :g0].transpose(2, 0, 1).reshape(batch * N_GDN_HEADS, seq, CHUNK)
    t_b = t_all[:, :, g0:].transpose(2, 1, 0).reshape(batch * N_GDN_HEADS, seq, CHUNK)
    pre = _gdn_pre(layer, q, k, k_t, v, beta, gc, gc_rows, gc_pairs, t_f, t_b, batch, seq)
    o_f, o_b = _gdn_state(layer, pre, batch, seq)
    return _gdn_out(layer, o_f, o_b, proj, norm_w)


def _layer(l, x, cos, sin, mod, lam_inits, p, batch, seq):
    m, d = x.shape
    layer = jnp.reshape(l, (1,)).astype(jnp.int32)
    tm = min(1024, seq)
    ident = lambda accs, extras: accs

    h = _norm_mod(layer, x, p["norm_mix_w"], mod, 1, 0, seq)
    (proj,) = _matmul(layer, [h], [p["w_main"]], [0], [0], [], ident, [F32], MAIN_COLS, tm, 1024, "proj_main")
    (small,) = _matmul(layer, [h], [p["w_small"]], [0], [0], [], ident, [F32], LANES, tm, LANES, "proj_small")
    (gates,) = _matmul(layer, [h], [p["w_gates"]], [0], [0], [],
                       lambda accs, extras: [jax.nn.sigmoid(accs[0])], [BF16], 2 * d, tm, 1024, "proj_gates")

    lam_init = lam_inits[l]
    lv = p["diff_lambda"][l].astype(F32)
    lam = jnp.exp(jnp.sum(lv[0] * lv[1])) - jnp.exp(jnp.sum(lv[2] * lv[3])) + lam_init
    scalars = jnp.stack([lam, 1.0 - lam_init]).astype(F32)
    dq, dk, dv = _qk_prep(layer, proj, cos, sin, p["diff_qn_w"], p["diff_kn_w"])
    y_diff = _diff_attention(layer, scalars, dq, dk, dv, p["diff_subln_w"], batch, seq)

    y_gdn = _gated_deltanet(layer, proj, small, p["gdn_conv_w"], p["gdn_a_log"], p["gdn_dt_bias"],
                            p["gdn_norm_w"], batch, seq)

    tn = 512
    (merged,) = _matmul(
        layer, [y_diff, y_gdn], [p["w_branch_diff"], p["w_branch_gdn"]], [0, 1], [0, 0],
        [(gates, (tm, tn), lambda i, j, l: (i, j)), (gates, (tm, tn), lambda i, j, l: (i, j + d // tn))],
        lambda accs, extras: [extras[0] * accs[0] + extras[1] * accs[1]], [BF16], d, tm, tn, "branch_merge")

    def residual(gate_idx):
        return [(x, (tm, tn), lambda i, j, l: (i, j)),
                (mod, (1, 1, 1, 1, tn), lambda i, j, l: (l[0], gate_idx, i * tm // seq, 0, j))]

    res_epi = lambda accs, extras: [extras[0] + extras[1][0, 0, 0] * accs[0]]
    (x,) = _matmul(layer, [merged], [p["w_out"]], [0], [0], residual(2), res_epi, [F32], d, tm, tn, "mixer_out")

    h = _norm_mod(layer, x, p["norm_ffn_w"], mod, 4, 3, seq)
    f = p["ffn_w_down"].shape[1]
    tf = 512
    (act,) = _matmul(layer, [h], [p["ffn_w_up"], p["ffn_w_up"]], [0, 0], [0, f // tf], [],
                     lambda accs, extras: [_silu(accs[0]) * accs[1]], [BF16], f, tm, tf, "ffn_up")
    tm2, tn2 = min(512, seq), 512

    def residual2(gate_idx):
        return [(x, (tm2, tn2), lambda i, j, l: (i, j)),
                (mod, (1, 1, 1, 1, tn2), lambda i, j, l: (l[0], gate_idx, i * tm2 // seq, 0, j))]

    (x,) = _matmul(layer, [act], [p["ffn_w_down"]], [0], [0], residual2(5), res_epi, [F32], d, tm2, tn2, "ffn_down")
    return x


def kernel(x, c, positions, ada_w, ada_b, norm_mix_w, norm_ffn_w, w_in, diff_qn_w, diff_kn_w, diff_lambda,
           diff_subln_w, gdn_conv_w, gdn_a_log, gdn_dt_bias, gdn_norm_w, w_branch_diff, w_branch_gdn, w_out,
           ffn_w_up, ffn_w_down):
    batch, seq, d = x.shape
    depth = ada_w.shape[0]
    mod = _ada_modulation(c, ada_w, ada_b)
    cos, sin = _rope_tables(positions)
    lam_inits = jnp.asarray([0.8 - 0.6 * math.exp(-0.3 * i) for i in range(depth)], F32)
    small_w = jnp.pad(w_in[:, :, MAIN_COLS:MAIN_COLS + SMALL_COLS], ((0, 0), (0, 0), (0, LANES - SMALL_COLS)))
    p = {
        "norm_mix_w": norm_mix_w.reshape(depth, 1, d), "norm_ffn_w": norm_ffn_w.reshape(depth, 1, d),
        "w_main": w_in,
        "w_small": small_w.astype(BF16),
        "w_gates": w_in[:, :, MAIN_COLS + SMALL_COLS:].astype(BF16),
        "diff_qn_w": diff_qn_w, "diff_kn_w": diff_kn_w, "diff_lambda": diff_lambda, "diff_subln_w": diff_subln_w,
        "gdn_conv_w": gdn_conv_w, "gdn_a_log": gdn_a_log, "gdn_dt_bias": gdn_dt_bias, "gdn_norm_w": gdn_norm_w,
        "w_branch_diff": w_branch_diff, "w_branch_gdn": w_branch_gdn,
        "w_out": w_out, "ffn_w_up": ffn_w_up, "ffn_w_down": ffn_w_down.astype(BF16),
    }
    body = lambda l, xc: _layer(l, xc, cos, sin, mod, lam_inits, p, batch, seq)
    out = lax.fori_loop(0, depth, body, x.reshape(batch * seq, d))
    return out.reshape(batch, seq, d)
```

```python
import functools
import math

import jax
import jax.numpy as jnp
from jax import lax
from jax.experimental import pallas as pl
from jax.experimental.pallas import tpu as pltpu

F32 = jnp.float32
BF16 = jnp.bfloat16

N_DIFF_HEADS = 8
DIFF_HEAD_DIM = 64
DIFF_WIDTH = N_DIFF_HEADS * 2 * DIFF_HEAD_DIM
N_GDN_HEADS = 8
GDN_HEAD_DIM = 128
GDN_WIDTH = N_GDN_HEADS * GDN_HEAD_DIM
CONV_WIDTH = 5
CHUNK = 64
ROPE_THETA = 10000.0
NORM_EPS = 1e-6
N_MOD = 6
LANES = 128
SUBLANES = 8
NEG_BIG = -1e30
GDN_PRE_CHUNKS = 8
GDN_SCAN_CHUNKS = 8
ATTN_Q_ROWS = 2048
ATTN_SUB_ROWS = 256

MAIN_COLS = 3 * DIFF_WIDTH + 4 * GDN_WIDTH
SMALL_COLS = 4 * N_GDN_HEADS
VMEM_LIMIT = 48 * 1024 * 1024


def _silu(x):
    return x * jax.nn.sigmoid(x)


def _cparams(n_axes, vmem=VMEM_LIMIT):
    return pltpu.CompilerParams(dimension_semantics=("arbitrary",) * n_axes, vmem_limit_bytes=vmem)


def _layer_call(kernel, grid, in_specs, out_specs, out_shape, scratch_shapes=(), name=None):
    return pl.pallas_call(
        kernel,
        grid_spec=pltpu.PrefetchScalarGridSpec(
            num_scalar_prefetch=1, grid=grid, in_specs=in_specs, out_specs=out_specs,
            scratch_shapes=scratch_shapes),
        out_shape=out_shape,
        compiler_params=_cparams(len(grid)),
        name=name,
    )


def _ada_kernel(c_ref, w_ref, b_ref, o_ref):
    c = c_ref[...]
    a = _silu(c).astype(BF16)
    o_ref[0, 0] = jnp.dot(a, w_ref[0].astype(BF16), preferred_element_type=F32) + b_ref[0]


def _ada_modulation(c, ada_w, ada_b):
    depth, d, n6 = ada_w.shape
    b = c.shape[0]
    rows = -(-b // SUBLANES) * SUBLANES
    c_pad = jnp.pad(c, ((0, rows - b), (0, 0)))
    tn = 1024
    per = d // tn
    out = pl.pallas_call(
        _ada_kernel,
        grid=(depth, n6 // tn),
        in_specs=[
            pl.BlockSpec((rows, d), lambda l, j: (0, 0)),
            pl.BlockSpec((1, d, tn), lambda l, j: (l, 0, j)),
            pl.BlockSpec((1, 1, tn), lambda l, j: (l, 0, j)),
        ],
        out_specs=pl.BlockSpec((1, 1, rows, tn), lambda l, j: (l, j // per, 0, j % per)),
        out_shape=jax.ShapeDtypeStruct((depth, N_MOD, rows, d), F32),
        compiler_params=_cparams(2),
        name="ada_modulation",
    )(c_pad, ada_w, ada_b.reshape(depth, 1, n6))
    return out[:, :, :b].reshape(depth, N_MOD, b, 1, d)


def _rope_kernel(ang_ref, cos_ref, sin_ref):
    ang = ang_ref[...]
    lane = lax.broadcasted_iota(jnp.int32, ang.shape, 1)
    first = (lane % DIFF_HEAD_DIM) < (DIFF_HEAD_DIM // 2)
    cos_ref[...] = jnp.cos(ang)
    s = jnp.sin(ang)
    sin_ref[...] = jnp.where(first, -s, s)


def _rope_tables(positions):
    m = positions.size
    half = DIFF_HEAD_DIM // 2
    inv_freq = ROPE_THETA ** (-jnp.arange(half, dtype=F32) * 2.0 / DIFF_HEAD_DIM)
    ang = positions.reshape(m, 1).astype(F32) * jnp.tile(inv_freq, LANES // half)[None, :]
    tm = min(1024, m)
    spec = pl.BlockSpec((tm, LANES), lambda i: (i, 0))
    return pl.pallas_call(
        _rope_kernel, grid=(m // tm,), in_specs=[spec], out_specs=[spec, spec],
        out_shape=[jax.ShapeDtypeStruct((m, LANES), F32)] * 2,
        compiler_params=_cparams(1), name="rope_tables",
    )(ang)


def _norm_mod_kernel(l_ref, x_ref, w_ref, sc_ref, sh_ref, o_ref):
    x = x_ref[...]
    ms = jnp.mean(x * x, axis=-1, keepdims=True)
    y = x * lax.rsqrt(ms + NORM_EPS) * w_ref[0]
    o_ref[...] = (y * (1.0 + sc_ref[0, 0, 0]) + sh_ref[0, 0, 0]).astype(BF16)


def _norm_mod(layer, x, norm_w, mod, scale_idx, shift_idx, seq):
    m, d = x.shape
    tm = min(512, seq)
    return _layer_call(
        _norm_mod_kernel, (m // tm,),
        [
            pl.BlockSpec((tm, d), lambda i, l: (i, 0)),
            pl.BlockSpec((1, 1, d), lambda i, l: (l[0], 0, 0)),
            pl.BlockSpec((1, 1, 1, 1, d), lambda i, l: (l[0], scale_idx, i * tm // seq, 0, 0)),
            pl.BlockSpec((1, 1, 1, 1, d), lambda i, l: (l[0], shift_idx, i * tm // seq, 0, 0)),
        ],
        pl.BlockSpec((tm, d), lambda i, l: (i, 0)),
        jax.ShapeDtypeStruct((m, d), BF16), name="norm_mod",
    )(layer, x, norm_w, mod, mod)


def _matmul(layer, xs, ws, w_x, w_off, extras, epilogue, out_dtypes, n, tm, tn, name):
    m = xs[0].shape[0]
    nx, nw, ne = len(xs), len(ws), len(extras)
    cast_w = ws[0].dtype == F32
    assert all((w.dtype == F32) == cast_w for w in ws)
    ij = (lambda a, b: (b, a)) if cast_w else (lambda a, b: (a, b))
    in_specs = [pl.BlockSpec((tm, x.shape[1]), lambda a, b, l: (ij(a, b)[0], 0)) for x in xs]
    for w, off in zip(ws, w_off):
        in_specs.append(pl.BlockSpec((1, w.shape[1], tn), lambda a, b, l, off=off: (l[0], 0, ij(a, b)[1] + off)))
    for _, bs, imap in extras:
        in_specs.append(pl.BlockSpec(bs, lambda a, b, l, imap=imap: imap(*ij(a, b), l)))
    out_specs = [pl.BlockSpec((tm, tn), lambda a, b, l: ij(a, b)) for _ in out_dtypes]
    out_shape = [jax.ShapeDtypeStruct((m, n), dt) for dt in out_dtypes]
    scratch = [pltpu.VMEM((w.shape[1], tn), BF16) for w in ws] if cast_w else []

    def kern(l_ref, *refs):
        x_refs, w_refs = refs[:nx], refs[nx:nx + nw]
        e_refs, o_refs = refs[nx + nw:nx + nw + ne], refs[nx + nw + ne:nx + nw + ne + len(out_dtypes)]
        if cast_w:
            wb_refs = refs[nx + nw + ne + len(out_dtypes):]

            @pl.when(pl.program_id(1) == 0)
            def _():
                for w_ref, wb_ref in zip(w_refs, wb_refs):
                    wb_ref[...] = w_ref[0].astype(BF16)

            w_tiles = [wb_ref[...] for wb_ref in wb_refs]
        else:
            w_tiles = [w_ref[0] for w_ref in w_refs]
        accs = [jnp.dot(x_refs[xi][...], w, preferred_element_type=F32) for xi, w in zip(w_x, w_tiles)]
        outs = epilogue(accs, [e[...] for e in e_refs])
        for o_ref, v in zip(o_refs, outs):
            o_ref[...] = v.astype(o_ref.dtype)

    grid = (n // tn, m // tm) if cast_w else (m // tm, n // tn)
    return _layer_call(kern, grid, in_specs, out_specs, out_shape, scratch_shapes=scratch, name=name)(
        layer, *xs, *ws, *[e[0] for e in extras])


def _residual_norm_kernel(l_ref, a_ref, w_ref, x_ref, gate_ref, nw_ref, sc_ref, sh_ref, xo_ref, ho_ref):
    acc = jnp.dot(a_ref[...], w_ref[0], preferred_element_type=F32)
    xn = x_ref[...] + gate_ref[0, 0, 0] * acc
    xo_ref[...] = xn
    ms = jnp.mean(xn * xn, axis=-1, keepdims=True)
    y = xn * lax.rsqrt(ms + NORM_EPS) * nw_ref[0]
    ho_ref[...] = (y * (1.0 + sc_ref[0, 0, 0]) + sh_ref[0, 0, 0]).astype(BF16)


def _matmul_residual_norm(layer, a, w, x, mod, gate_idx, norm_w, scale_idx, shift_idx, next_layer, seq, tm, name):
    m, k = a.shape
    depth, _, n = w.shape
    nl = (lambda l: jnp.minimum(l[0] + 1, depth - 1)) if next_layer else (lambda l: l[0])
    row = lambda i: i * tm // seq
    return _layer_call(
        _residual_norm_kernel, (m // tm,),
        [
            pl.BlockSpec((tm, k), lambda i, l: (i, 0)),
            pl.BlockSpec((1, k, n), lambda i, l: (l[0], 0, 0), pipeline_mode=pl.Buffered(1)),
            pl.BlockSpec((tm, n), lambda i, l: (i, 0)),
            pl.BlockSpec((1, 1, 1, 1, n), lambda i, l: (l[0], gate_idx, row(i), 0, 0)),
            pl.BlockSpec((1, 1, n), lambda i, l: (nl(l), 0, 0)),
            pl.BlockSpec((1, 1, 1, 1, n), lambda i, l: (nl(l), scale_idx, row(i), 0, 0)),
            pl.BlockSpec((1, 1, 1, 1, n), lambda i, l: (nl(l), shift_idx, row(i), 0, 0)),
        ],
        [pl.BlockSpec((tm, n), lambda i, l: (i, 0))] * 2,
        [jax.ShapeDtypeStruct((m, n), F32), jax.ShapeDtypeStruct((m, n), BF16)], name=name,
    )(layer, a, w, x, mod, norm_w, mod, mod)


def _group_sumsq(x, group_ones):
    sq = x * x
    hi = sq.astype(BF16)
    lo = (sq - hi.astype(F32)).astype(BF16)
    return (jnp.dot(hi, group_ones, preferred_element_type=F32)
            + jnp.dot(lo, group_ones, preferred_element_type=F32))


def _qk_prep_kernel(l_ref, q_ref, k_ref, v_ref, cos_ref, sin_ref, qw_ref, kw_ref, qo_ref, ko_ref, vo_ref):
    cos = cos_ref[...]
    sin = sin_ref[...]
    row = lax.broadcasted_iota(jnp.int32, (LANES, LANES), 0) // DIFF_HEAD_DIM
    col = lax.broadcasted_iota(jnp.int32, (LANES, LANES), 1) // DIFF_HEAD_DIM
    ones = (row == col).astype(BF16)
    lane = lax.broadcasted_iota(jnp.int32, cos.shape, 1)
    first = (lane % DIFF_HEAD_DIM) < (DIFF_HEAD_DIM // 2)
    half = DIFF_HEAD_DIM // 2

    def norm_rope(x, w, scale):
        ms = _group_sumsq(x, ones) * (1.0 / DIFF_HEAD_DIM)
        y = x * lax.rsqrt(ms + NORM_EPS) * w
        partner = jnp.where(first, pltpu.roll(y, LANES - half, 1), pltpu.roll(y, half, 1))
        return (y * cos + partner * sin) * scale

    q_scale = DIFF_HEAD_DIM ** -0.5 * math.log2(math.e)
    ones_blk = jnp.ones((cos.shape[0], LANES), BF16)
    for h in range(N_DIFF_HEADS):
        sl = slice(h * LANES, (h + 1) * LANES)
        qo_ref[:, sl] = norm_rope(q_ref[:, sl], qw_ref[0, :, sl], q_scale).astype(BF16)
        ko_ref[:, sl] = norm_rope(k_ref[:, sl], kw_ref[0, :, sl], 1.0).astype(BF16)
        vo_ref[:, 2 * h * LANES:(2 * h + 1) * LANES] = v_ref[:, sl].astype(BF16)
        vo_ref[:, (2 * h + 1) * LANES:(2 * h + 2) * LANES] = ones_blk


def _qk_prep(layer, proj, cos, sin, qn_w, kn_w):
    m = proj.shape[0]
    tm = min(512, m)
    w = DIFF_WIDTH
    rep = w // DIFF_HEAD_DIM
    depth = qn_w.shape[0]
    qw = jnp.tile(qn_w, (1, rep)).reshape(depth, 1, w)
    kw = jnp.tile(kn_w, (1, rep)).reshape(depth, 1, w)
    blk = lambda c: pl.BlockSpec((tm, w), lambda i, l, c=c: (i, c))
    tab = pl.BlockSpec((tm, LANES), lambda i, l: (i, 0))
    wsp = pl.BlockSpec((1, 1, w), lambda i, l: (l[0], 0, 0))
    osp = pl.BlockSpec((tm, w), lambda i, l: (i, 0))
    vsp = pl.BlockSpec((tm, 2 * w), lambda i, l: (i, 0))
    return _layer_call(
        _qk_prep_kernel, (m // tm,), [blk(0), blk(1), blk(2), tab, tab, wsp, wsp], [osp, osp, vsp],
        [jax.ShapeDtypeStruct((m, w), BF16)] * 2 + [jax.ShapeDtypeStruct((m, 2 * w), BF16)], name="diff_qk_prep",
    )(layer, proj, proj, proj, cos, sin, qw, kw)


def _diff_attn_kernel(l_ref, sc_ref, q_ref, k_ref, v_ref, w_ref, o_ref):
    lam = sc_ref[0]
    out_scale = sc_ref[1]
    k = k_ref[...]
    nt = (((1,), (1,)), ((), ()))
    sub = min(ATTN_SUB_ROWS, q_ref.shape[0])

    def branch(qm):
        s = lax.dot_general(qm, k, nt, preferred_element_type=F32)
        p = jnp.exp2(s - jnp.max(s, axis=-1, keepdims=True)).astype(BF16)
        ov = jnp.dot(p, v_ref[...], preferred_element_type=F32)
        return ov[:, :LANES] / ov[:, LANES:]

    for r in range(q_ref.shape[0] // sub):
        rows = slice(r * sub, (r + 1) * sub)
        q = q_ref[rows, :]
        lane = lax.broadcasted_iota(jnp.int32, q.shape, 1)
        zero = jnp.zeros_like(q)
        q0 = jnp.where(lane < DIFF_HEAD_DIM, q, zero)
        q1 = jnp.where(lane >= DIFF_HEAD_DIM, q, zero)
        o = branch(q0) - lam * branch(q1)
        ms = jnp.mean(o * o, axis=-1, keepdims=True)
        o_ref[rows, :] = (o * lax.rsqrt(ms + NORM_EPS) * w_ref[0] * out_scale).astype(BF16)


def _diff_attention(layer, scalars, q, k, v, subln_w, batch, seq):
    m = q.shape[0]
    tq = min(ATTN_Q_ROWS, seq)
    nq = seq // tq
    depth = subln_w.shape[0]
    return _layer_call(
        _diff_attn_kernel, (batch, N_DIFF_HEADS, nq),
        [
            pl.BlockSpec(memory_space=pltpu.SMEM),
            pl.BlockSpec((tq, LANES), lambda b, h, i, l: (b * nq + i, h)),
            pl.BlockSpec((seq, LANES), lambda b, h, i, l: (b, h)),
            pl.BlockSpec((seq, 2 * LANES), lambda b, h, i, l: (b, h)),
            pl.BlockSpec((1, 1, LANES), lambda b, h, i, l: (l[0], 0, 0)),
        ],
        pl.BlockSpec((tq, LANES), lambda b, h, i, l: (b * nq + i, h)),
        jax.ShapeDtypeStruct((m, DIFF_WIDTH), BF16), name="diff_attention",
    )(layer, scalars, q, k, v, subln_w.reshape(depth, 1, LANES))


def _split3(x):
    hi = x.astype(BF16)
    r1 = x - hi.astype(F32)
    mid = r1.astype(BF16)
    lo = (r1 - mid.astype(F32)).astype(BF16)
    return hi, mid, lo


def _gdn_prep_kernel(l_ref, main_ref, prev_ref, next_ref, small_ref, cw_ref, alog_ref, dtb_ref,
                     q_ref, k_ref, kt_ref, v_ref, beta_ref, gc_ref, xe_ref):
    t = pl.program_id(1)
    nt = pl.num_programs(1)
    ts = main_ref.shape[0]
    pad = CONV_WIDTH // 2
    xe_ref[0:SUBLANES, :] = jnp.where(t > 0, prev_ref[...], 0.0)
    xe_ref[SUBLANES:SUBLANES + ts, :] = main_ref[...]
    xe_ref[SUBLANES + ts:, :] = jnp.where(t < nt - 1, next_ref[...], 0.0)
    cw = cw_ref[0]
    first = SUBLANES - pad
    acc = cw[0:1, :] * xe_ref[first:first + ts, :]
    for j in range(1, CONV_WIDTH):
        acc = acc + cw[j:j + 1, :] * xe_ref[first + j:first + j + ts, :]
    y = _silu(acc)
    for h in range(N_GDN_HEADS):
        sl = slice(h * LANES, (h + 1) * LANES)
        qh = y[:, sl]
        kh = y[:, GDN_WIDTH + h * LANES:GDN_WIDTH + (h + 1) * LANES]
        q_ref[:, sl] = qh * (lax.rsqrt(jnp.sum(qh * qh, axis=-1, keepdims=True) + NORM_EPS)
                             * GDN_HEAD_DIM ** -0.5)
        kn = kh * lax.rsqrt(jnp.sum(kh * kh, axis=-1, keepdims=True) + NORM_EPS)
        k_ref[:, sl] = kn
        kt_ref[sl, :] = kn.T
    v_ref[...] = y[:, 2 * GDN_WIDTH:]

    nh2 = 2 * N_GDN_HEADS
    small = small_ref[...]
    beta_ref[...] = jax.nn.sigmoid(small[:, :nh2])
    g = -jnp.exp(alog_ref[0]) * jax.nn.softplus(small[:, nh2:2 * nh2] + dtb_ref[0])
    r = lax.broadcasted_iota(jnp.int32, (ts, ts), 0)
    c = lax.broadcasted_iota(jnp.int32, (ts, ts), 1)
    same = (r // CHUNK) == (c // CHUNK)
    pre = (same & (c <= r)).astype(BF16)
    suf = (same & (c >= r)).astype(BF16)
    parts = _split3(g)
    cs_f = sum(jnp.dot(pre, p, preferred_element_type=F32) for p in parts)
    cs_b = sum(jnp.dot(suf, p, preferred_element_type=F32) for p in parts)
    lane = lax.broadcasted_iota(jnp.int32, g.shape, 1)
    gc_ref[...] = jnp.where(lane < N_GDN_HEADS, cs_f, cs_b)


def _gdn_prep(layer, proj, small, conv_w, a_log, dt_bias, batch, seq):
    m = proj.shape[0]
    ts = min(256, seq)
    nts = seq // ts
    w3 = 3 * GDN_WIDTH
    cb = (3 * DIFF_WIDTH) // w3
    assert cb * w3 == 3 * DIFF_WIDTH
    depth = conv_w.shape[0]
    nh2 = 2 * N_GDN_HEADS
    hb = ts // SUBLANES
    last = m // SUBLANES - 1
    osp = pl.BlockSpec((ts, GDN_WIDTH), lambda b, t, l: (b * nts + t, 0))
    ssp = pl.BlockSpec((ts, nh2), lambda b, t, l: (b * nts + t, 0))
    return _layer_call(
        _gdn_prep_kernel, (batch, nts),
        [
            pl.BlockSpec((ts, w3), lambda b, t, l: (b * nts + t, cb)),
            pl.BlockSpec((SUBLANES, w3), lambda b, t, l: (jnp.maximum((b * nts + t) * hb - 1, 0), cb)),
            pl.BlockSpec((SUBLANES, w3), lambda b, t, l: (jnp.minimum((b * nts + t + 1) * hb, last), cb)),
            pl.BlockSpec((ts, LANES), lambda b, t, l: (b * nts + t, 0)),
            pl.BlockSpec((1, CONV_WIDTH, w3), lambda b, t, l: (l[0], 0, 0)),
            pl.BlockSpec((1, 1, nh2), lambda b, t, l: (l[0], 0, 0)),
            pl.BlockSpec((1, 1, nh2), lambda b, t, l: (l[0], 0, 0)),
        ],
        [osp, osp, pl.BlockSpec((GDN_WIDTH, ts), lambda b, t, l: (0, b * nts + t)), osp, ssp, ssp],
        [jax.ShapeDtypeStruct((m, GDN_WIDTH), F32)] * 2 + [jax.ShapeDtypeStruct((GDN_WIDTH, m), F32)]
        + [jax.ShapeDtypeStruct((m, GDN_WIDTH), F32)] + [jax.ShapeDtypeStruct((m, nh2), F32)] * 2,
        scratch_shapes=[pltpu.VMEM((ts + 2 * SUBLANES, w3), F32)], name="gdn_prep",
    )(layer, proj, proj, proj, small, conv_w, a_log.reshape(depth, 1, nh2), dt_bias.reshape(depth, 1, nh2))


def _chunk_masks(backward):
    r = lax.broadcasted_iota(jnp.int32, (CHUNK, CHUNK), 0)
    c = lax.broadcasted_iota(jnp.int32, (CHUNK, CHUNK), 1)
    if backward:
        return r <= c, r < c
    return r >= c, r > c


def _gdn_l_kernel(l_ref, k_ref, beta_ref, gc_ref, gcrow_ref, lf_ref, lb_ref):
    h = pl.program_id(1)
    n_chunks = k_ref.shape[0] // CHUNK
    nt = (((1,), (1,)), ((), ()))
    lane16 = lax.broadcasted_iota(jnp.int32, (CHUNK, 2 * N_GDN_HEADS), 1)
    sub16 = lax.broadcasted_iota(jnp.int32, (2 * N_GDN_HEADS, CHUNK), 0)

    def body(n, carry):
        rows = pl.ds(pl.multiple_of(n * CHUNK, CHUNK), CHUNK)
        kb = k_ref[rows, :].astype(BF16)
        kk = lax.dot_general(kb, kb, nt, preferred_element_type=F32)
        beta = beta_ref[rows, :]
        gc = gc_ref[rows, :]
        gcr = gcrow_ref[0, n]
        for d, out_ref in ((0, lf_ref), (1, lb_ref)):
            col = d * N_GDN_HEADS + h
            bcol = jnp.sum(jnp.where(lane16 == col, beta, 0.0), axis=1, keepdims=True)
            gcol = jnp.sum(jnp.where(lane16 == col, gc, 0.0), axis=1, keepdims=True)
            grow = jnp.sum(jnp.where(sub16 == col, gcr, 0.0), axis=0, keepdims=True)
            _, strict = _chunk_masks(d == 1)
            dec = jnp.exp(jnp.where(strict, gcol - grow, NEG_BIG))
            out_ref[0, rows, :] = bcol * kk * dec
        return carry

    lax.fori_loop(0, n_chunks, body, 0, unroll=4)


def _gdn_build_l(layer, k, beta, gc, gc_rows, batch, seq):
    nh2 = 2 * N_GDN_HEADS
    n_chunks = seq // CHUNK
    osp = pl.BlockSpec((1, seq, CHUNK), lambda b, h, l: (b * N_GDN_HEADS + h, 0, 0))
    return _layer_call(
        _gdn_l_kernel, (batch, N_GDN_HEADS),
        [
            pl.BlockSpec((seq, LANES), lambda b, h, l: (b, h)),
            pl.BlockSpec((seq, nh2), lambda b, h, l: (b, 0)),
            pl.BlockSpec((seq, nh2), lambda b, h, l: (b, 0)),
            pl.BlockSpec((1, n_chunks, nh2, CHUNK), lambda b, h, l: (b, 0, 0, 0)),
        ],
        [osp, osp],
        [jax.ShapeDtypeStruct((batch * N_GDN_HEADS, seq, CHUNK), F32)] * 2, name="gdn_build_l",
    )(layer, k, beta, gc, gc_rows)


def _tri_inverse_kernel(l_ref, t_ref):
    sub = lax.broadcasted_iota(jnp.int32, (SUBLANES, LANES), 0)
    nblk = CHUNK // SUBLANES
    for i in range(CHUNK):
        live = i // SUBLANES + 1
        accs = [jnp.zeros((SUBLANES, LANES), F32) for _ in range(live)]
        for j in range(i):
            lij = l_ref[i, pl.ds(j, 1), :]
            for cb in range(j // SUBLANES + 1):
                accs[cb] = accs[cb] + lij * t_ref[j, cb * SUBLANES:(cb + 1) * SUBLANES, :]
        for cb in range(nblk):
            if cb < live - 1:
                val = -accs[cb]
            elif cb == live - 1:
                val = jnp.where(sub == i % SUBLANES, 1.0, 0.0) - accs[cb]
            else:
                val = jnp.zeros((SUBLANES, LANES), F32)
            t_ref[i, cb * SUBLANES:(cb + 1) * SUBLANES, :] = val


def _tri_inverse(l_all):
    g = l_all.shape[-1]
    spec = pl.BlockSpec((CHUNK, CHUNK, LANES), lambda i: (0, 0, i))
    return pl.pallas_call(
        _tri_inverse_kernel, grid=(g // LANES,), in_specs=[spec], out_specs=spec,
        out_shape=jax.ShapeDtypeStruct(l_all.shape, F32),
        compiler_params=_cparams(1), name="gdn_tri_inverse",
    )(l_all)


def _gdn_pre_kernel(l_ref, q_ref, k_ref, kt_ref, v_ref, beta_ref, gc_ref, gcrow_ref, gcpair_ref, tf_ref, tb_ref,
                    mqf_ref, mqb_ref, rf_ref, rb_ref, glf_ref, glb_ref, ol_ref):
    h = pl.program_id(1)
    n_local = q_ref.shape[0] // CHUNK
    dk = GDN_HEAD_DIM
    nt = (((1,), (1,)), ((), ()))
    lane16 = lax.broadcasted_iota(jnp.int32, (CHUNK, 2 * N_GDN_HEADS), 1)
    sub16 = lax.broadcasted_iota(jnp.int32, (2 * N_GDN_HEADS, CHUNK), 0)
    sub16p = lax.broadcasted_iota(jnp.int32, (2 * N_GDN_HEADS, LANES), 0)
    lane_half = lax.broadcasted_iota(jnp.int32, (1, LANES), 1) // CHUNK
    outs = ((tf_ref, mqf_ref, rf_ref, glf_ref), (tb_ref, mqb_ref, rb_ref, glb_ref))
    per_pair = LANES // CHUNK
    stage1 = []
    for c in range(n_local):
        rows = slice(c * CHUNK, (c + 1) * CHUNK)
        pair, half = c // per_pair, c % per_pair
        qc = q_ref[rows, :]
        kc = k_ref[rows, :]
        vc = v_ref[rows, :]
        kt_pair = kt_ref[:, pair * LANES:(pair + 1) * LANES]
        qk = lax.dot_general(qc.astype(BF16), kc.astype(BF16), nt, preferred_element_type=F32)
        for d in (0, 1):
            col = d * N_GDN_HEADS + h
            bcol = jnp.sum(jnp.where(lane16 == col, beta_ref[rows, :], 0.0), axis=1, keepdims=True)
            gcol = jnp.sum(jnp.where(lane16 == col, gc_ref[rows, :], 0.0), axis=1, keepdims=True)
            grow = jnp.sum(jnp.where(sub16 == col, gcrow_ref[0, c], 0.0), axis=0, keepdims=True)
            glast = gcol[0:1, :] if d == 1 else gcol[CHUNK - 1:CHUNK, :]
            incl, _ = _chunk_masks(d == 1)
            attn = (qk * jnp.exp(jnp.where(incl, gcol - grow, NEG_BIG))).astype(BF16)
            gam = jnp.exp(gcol)
            x = jnp.concatenate([bcol * vc, (bcol * gam) * kc], axis=1).astype(BF16)
            t = outs[d][0][0, rows, :]
            th = t.astype(BF16)
            tl = (t - th.astype(F32)).astype(BF16)
            uw = (jnp.dot(th, x, preferred_element_type=F32)
                  + jnp.dot(tl, x, preferred_element_type=F32)).astype(BF16)
            grow_pair = jnp.sum(jnp.where(sub16p == col, gcpair_ref[0, pair], 0.0), axis=0, keepdims=True)
            tail = jnp.exp(jnp.where(lane_half == half, glast - grow_pair, NEG_BIG))
            ktil_t = (kt_pair * tail).astype(BF16)
            stage1.append((c, d, attn, uw, ktil_t, gam * qc, jnp.exp(glast)))
    o_local = {}
    for c, d, attn, uw, ktil_t, gq, gl in stage1:
        _, mq_ref, r_ref, gl_ref = outs[d]
        pos = (n_local - 1 - c) if d == 1 else c
        awu = jnp.dot(attn, uw, preferred_element_type=F32)
        kwu = jnp.dot(ktil_t, jnp.concatenate([uw] * per_pair, axis=0), preferred_element_type=F32)
        mq_ref[0, pos, 0:dk, :] = kwu[:, dk:].astype(BF16)
        mq_ref[0, pos, dk:dk + CHUNK, :] = (gq - awu[:, dk:]).astype(BF16)
        r_ref[0, pos] = kwu[:, :dk].astype(BF16)
        gl_ref[0, pos] = jnp.broadcast_to(gl, (1, LANES))
        o_local[c] = o_local[c] + awu[:, :dk] if c in o_local else awu[:, :dk]
    for c, val in o_local.items():
        ol_ref[c * CHUNK:(c + 1) * CHUNK, :] = val


def _gdn_pre(layer, q, k, k_t, v, beta, gc, gc_rows, gc_pairs, t_f, t_b, batch, seq):
    nh2 = 2 * N_GDN_HEADS
    n_chunks = seq // CHUNK
    cg = min(GDN_PRE_CHUNKS, n_chunks)
    ng = n_chunks // cg
    rows = cg * CHUNK
    dk = GDN_HEAD_DIM
    bh = batch * N_GDN_HEADS
    hsp = pl.BlockSpec((rows, LANES), lambda b, h, g, l: (b * ng + g, h))
    ssp = pl.BlockSpec((rows, nh2), lambda b, h, g, l: (b * ng + g, 0))
    tsp = pl.BlockSpec((1, rows, CHUNK), lambda b, h, g, l: (b * N_GDN_HEADS + h, g, 0))

    def osp(r, mirrored):
        if mirrored:
            return pl.BlockSpec((1, cg, r, LANES), lambda b, h, g, l: (b * N_GDN_HEADS + h, ng - 1 - g, 0, 0))
        return pl.BlockSpec((1, cg, r, LANES), lambda b, h, g, l: (b * N_GDN_HEADS + h, g, 0, 0))

    shapes = [((dk + CHUNK), BF16), (dk, BF16), (1, F32)]
    out_specs, out_shape = [], []
    for r, dt in shapes:
        for mirrored in (False, True):
            out_specs.append(osp(r, mirrored))
            out_shape.append(jax.ShapeDtypeStruct((bh, n_chunks, r, LANES), dt))
    return _layer_call(
        _gdn_pre_kernel, (batch, N_GDN_HEADS, ng),
        [hsp, hsp, pl.BlockSpec((LANES, rows), lambda b, h, g, l: (h, b * ng + g)), hsp, ssp, ssp,
         pl.BlockSpec((1, cg, nh2, CHUNK), lambda b, h, g, l: (b, g, 0, 0)),
         pl.BlockSpec((1, rows // LANES, nh2, LANES), lambda b, h, g, l: (b, g, 0, 0)),
         tsp, tsp],
        out_specs + [hsp], out_shape + [jax.ShapeDtypeStruct(q.shape, F32)], name="gdn_chunk_pre",
    )(layer, q, k, k_t, v, beta, gc, gc_rows, gc_pairs, t_f, t_b)


def _gdn_state_kernel(l_ref, mqf_ref, mqb_ref, rf_ref, rb_ref, glf_ref, glb_ref, of_ref, ob_ref, st_ref):
    g = pl.program_id(1)
    n_heads, n_local = mqf_ref.shape[0], mqf_ref.shape[1]
    dk = GDN_HEAD_DIM

    @pl.when(g == 0)
    def _():
        st_ref[...] = jnp.zeros_like(st_ref)

    zero = jnp.zeros((dk, dk), BF16)
    states = [(st_ref[2 * hh], st_ref[2 * hh + 1]) for hh in range(n_heads)]
    for c in range(n_local):
        for hh in range(n_heads):
            sf, sb = states[hh]
            mq = jnp.concatenate([mqf_ref[hh, c], mqb_ref[hh, c]], axis=1)
            bd = jnp.concatenate([jnp.concatenate([sf.astype(BF16), zero], axis=1),
                                  jnp.concatenate([zero, sb.astype(BF16)], axis=1)], axis=0)
            res = jnp.dot(mq, bd, preferred_element_type=F32)
            hl = slice(hh * LANES, (hh + 1) * LANES)
            of_ref[c * CHUNK:(c + 1) * CHUNK, hl] = res[dk:, :dk]
            cb = n_local - 1 - c
            ob_ref[cb * CHUNK:(cb + 1) * CHUNK, hl] = res[dk:, dk:]
            states[hh] = (glf_ref[hh, c] * sf - res[:dk, :dk] + rf_ref[hh, c].astype(F32),
                          glb_ref[hh, c] * sb - res[:dk, dk:] + rb_ref[hh, c].astype(F32))
    for hh in range(n_heads):
        st_ref[2 * hh] = states[hh][0]
        st_ref[2 * hh + 1] = states[hh][1]


def _gdn_state(layer, pre, batch, seq):
    n_chunks = seq // CHUNK
    cg = min(GDN_SCAN_CHUNKS, n_chunks)
    ng = n_chunks // cg
    nh = N_GDN_HEADS
    dk = GDN_HEAD_DIM
    m = batch * seq
    in_specs = [pl.BlockSpec((nh, cg) + a.shape[2:], lambda b, g, l: (b, g, 0, 0)) for a in pre]
    return _layer_call(
        _gdn_state_kernel, (batch, ng), in_specs,
        [pl.BlockSpec((cg * CHUNK, GDN_WIDTH), lambda b, g, l: (b * ng + g, 0)),
         pl.BlockSpec((cg * CHUNK, GDN_WIDTH), lambda b, g, l: (b * ng + ng - 1 - g, 0))],
        [jax.ShapeDtypeStruct((m, GDN_WIDTH), F32)] * 2,
        scratch_shapes=[pltpu.VMEM((2 * nh, dk, dk), F32)], name="gdn_state_scan",
    )(layer, *pre)


def _gdn_out_kernel(l_ref, of_ref, ob_ref, ol_ref, z_ref, nw_ref, y_ref):
    for h in range(N_GDN_HEADS):
        hl = slice(h * LANES, (h + 1) * LANES)
        o = of_ref[:, hl] + ob_ref[:, hl] + ol_ref[:, hl]
        ms = jnp.mean(o * o, axis=-1, keepdims=True)
        y_ref[:, hl] = (o * lax.rsqrt(ms + NORM_EPS) * nw_ref[0] * _silu(z_ref[:, hl])).astype(BF16)


def _gdn_out(layer, o_f, o_b, o_local, proj, norm_w):
    m = o_f.shape[0]
    tm = min(512, m)
    depth = norm_w.shape[0]
    zcb = (3 * DIFF_WIDTH + 3 * GDN_WIDTH) // GDN_WIDTH
    osp = pl.BlockSpec((tm, GDN_WIDTH), lambda i, l: (i, 0))
    return _layer_call(
        _gdn_out_kernel, (m // tm,),
        [osp, osp, osp, pl.BlockSpec((tm, GDN_WIDTH), lambda i, l: (i, zcb)),
         pl.BlockSpec((1, 1, LANES), lambda i, l: (l[0], 0, 0))],
        osp, jax.ShapeDtypeStruct((m, GDN_WIDTH), BF16), name="gdn_out",
    )(layer, o_f, o_b, o_local, proj, norm_w.reshape(depth, 1, LANES))


def _gated_deltanet(layer, proj, small, conv_w, a_log, dt_bias, norm_w, batch, seq):
    q, k, k_t, v, beta, gc = _gdn_prep(layer, proj, small, conv_w, a_log, dt_bias, batch, seq)
    n_chunks = seq // CHUNK
    nh2 = 2 * N_GDN_HEADS
    gc_rows = gc.reshape(batch, n_chunks, CHUNK, nh2).transpose(0, 1, 3, 2)
    gc_pairs = gc.reshape(batch, seq // LANES, LANES, nh2).transpose(0, 1, 3, 2)
    l_f, l_b = _gdn_build_l(layer, k, beta, gc, gc_rows, batch, seq)
    g0 = batch * N_GDN_HEADS * n_chunks
    l_all = jnp.concatenate([l_f.reshape(g0, CHUNK, CHUNK).transpose(1, 2, 0),
                             l_b.reshape(g0, CHUNK, CHUNK).transpose(2, 1, 0)], axis=-1)
    t_all = _tri_inverse(l_all)
    t_f = t_all[:, :, :g0].transpose(2, 0, 1).reshape(batch * N_GDN_HEADS, seq, CHUNK)
    t_b = t_all[:, :, g0:].transpose(2, 1, 0).reshape(batch * N_GDN_HEADS, seq, CHUNK)
    *pre, o_local = _gdn_pre(layer, q, k, k_t, v, beta, gc, gc_rows, gc_pairs, t_f, t_b, batch, seq)
    o_f, o_b = _gdn_state(layer, pre, batch, seq)
    return _gdn_out(layer, o_f, o_b, o_local, proj, norm_w)


def _layer(l, x, h, cos, sin, mod, lam_inits, p, batch, seq):
    m, d = x.shape
    layer = jnp.reshape(l, (1,)).astype(jnp.int32)
    tm = min(1024, seq)
    ident = lambda accs, extras: accs

    (proj,) = _matmul(layer, [h], [p["w_main"]], [0], [0], [], ident, [F32], MAIN_COLS, tm, 1024, "proj_main")
    (small,) = _matmul(layer, [h], [p["w_small"]], [0], [0], [], ident, [F32], LANES, tm, LANES, "proj_small")
    (gates,) = _matmul(layer, [h], [p["w_gates"]], [0], [0], [],
                       lambda accs, extras: [jax.nn.sigmoid(accs[0])], [BF16], 2 * d, tm, 1024, "proj_gates")

    lam_init = lam_inits[l]
    lv = p["diff_lambda"][l].astype(F32)
    lam = jnp.exp(jnp.sum(lv[0] * lv[1])) - jnp.exp(jnp.sum(lv[2] * lv[3])) + lam_init
    scalars = jnp.stack([lam, 1.0 - lam_init]).astype(F32)
    dq, dk, dv = _qk_prep(layer, proj, cos, sin, p["diff_qn_w"], p["diff_kn_w"])
    y_diff = _diff_attention(layer, scalars, dq, dk, dv, p["diff_subln_w"], batch, seq)

    y_gdn = _gated_deltanet(layer, proj, small, p["gdn_conv_w"], p["gdn_a_log"], p["gdn_dt_bias"],
                            p["gdn_norm_w"], batch, seq)

    tn = 512
    (merged,) = _matmul(
        layer, [y_diff, y_gdn], [p["w_branch_diff"], p["w_branch_gdn"]], [0, 1], [0, 0],
        [(gates, (tm, tn), lambda i, j, l: (i, j)), (gates, (tm, tn), lambda i, j, l: (i, j + d // tn))],
        lambda accs, extras: [extras[0] * accs[0] + extras[1] * accs[1]], [BF16], d, tm, tn, "branch_merge")

    x, h = _matmul_residual_norm(layer, merged, p["w_out"], x, mod, 2, p["norm_ffn_w"], 4, 3, False, seq,
                                 min(512, seq), "mixer_out")
    f = p["ffn_w_down"].shape[1]
    tf = 512
    (act,) = _matmul(layer, [h], [p["ffn_w_up"], p["ffn_w_up"]], [0, 0], [0, f // tf], [],
                     lambda accs, extras: [_silu(accs[0]) * accs[1]], [BF16], f, tm, tf, "ffn_up")
    return tuple(_matmul_residual_norm(layer, act, p["ffn_w_down"], x, mod, 5, p["norm_mix_w"], 1, 0, True, seq,
                                       min(256, seq), "ffn_down"))


def kernel(x, c, positions, ada_w, ada_b, norm_mix_w, norm_ffn_w, w_in, diff_qn_w, diff_kn_w, diff_lambda,
           diff_subln_w, gdn_conv_w, gdn_a_log, gdn_dt_bias, gdn_norm_w, w_branch_diff, w_branch_gdn, w_out,
           ffn_w_up, ffn_w_down):
    batch, seq, d = x.shape
    depth = ada_w.shape[0]
    mod = _ada_modulation(c, ada_w, ada_b)
    cos, sin = _rope_tables(positions)
    lam_inits = jnp.asarray([0.8 - 0.6 * math.exp(-0.3 * i) for i in range(depth)], F32)
    small_w = jnp.pad(w_in[:, :, MAIN_COLS:MAIN_COLS + SMALL_COLS], ((0, 0), (0, 0), (0, LANES - SMALL_COLS)))
    p = {
        "norm_mix_w": norm_mix_w.reshape(depth, 1, d), "norm_ffn_w": norm_ffn_w.reshape(depth, 1, d),
        "w_main": w_in,
        "w_small": small_w.astype(BF16),
        "w_gates": w_in[:, :, MAIN_COLS + SMALL_COLS:].astype(BF16),
        "diff_qn_w": diff_qn_w, "diff_kn_w": diff_kn_w, "diff_lambda": diff_lambda, "diff_subln_w": diff_subln_w,
        "gdn_conv_w": gdn_conv_w, "gdn_a_log": gdn_a_log, "gdn_dt_bias": gdn_dt_bias, "gdn_norm_w": gdn_norm_w,
        "w_branch_diff": w_branch_diff, "w_branch_gdn": w_branch_gdn,
        "w_out": w_out.astype(BF16), "ffn_w_up": ffn_w_up, "ffn_w_down": ffn_w_down.astype(BF16),
    }
    x0 = x.reshape(batch * seq, d)
    h0 = _norm_mod(jnp.zeros((1,), jnp.int32), x0, p["norm_mix_w"], mod, 1, 0, seq)
    body = lambda l, carry: _layer(l, carry[0], carry[1], cos, sin, mod, lam_inits, p, batch, seq)
    out, _ = lax.fori_loop(0, depth, body, (x0, h0))
    return out.reshape(batch, seq, d)
```

```python
import functools
import math

import jax
import jax.numpy as jnp
from jax import lax
from jax.experimental import pallas as pl
from jax.experimental.pallas import tpu as pltpu

F32 = jnp.float32
BF16 = jnp.bfloat16

N_DIFF_HEADS = 8
DIFF_HEAD_DIM = 64
DIFF_WIDTH = N_DIFF_HEADS * 2 * DIFF_HEAD_DIM
N_GDN_HEADS = 8
GDN_HEAD_DIM = 128
GDN_WIDTH = N_GDN_HEADS * GDN_HEAD_DIM
CONV_WIDTH = 5
CHUNK = 64
ROPE_THETA = 10000.0
NORM_EPS = 1e-6
N_MOD = 6
LANES = 128
SUBLANES = 8
NEG_BIG = -1e30
GDN_PRE_CHUNKS = 16
GDN_SCAN_CHUNKS = 8
ATTN_Q_ROWS = 2048
ATTN_SUB_ROWS = 256

MAIN_COLS = 3 * DIFF_WIDTH + 4 * GDN_WIDTH
SMALL_COLS = 4 * N_GDN_HEADS
VMEM_LIMIT = 48 * 1024 * 1024


def _silu(x):
    return x * jax.nn.sigmoid(x)


def _cparams(n_axes, vmem=VMEM_LIMIT):
    return pltpu.CompilerParams(dimension_semantics=("arbitrary",) * n_axes, vmem_limit_bytes=vmem)


def _layer_call(kernel, grid, in_specs, out_specs, out_shape, scratch_shapes=(), name=None):
    return pl.pallas_call(
        kernel,
        grid_spec=pltpu.PrefetchScalarGridSpec(
            num_scalar_prefetch=1, grid=grid, in_specs=in_specs, out_specs=out_specs,
            scratch_shapes=scratch_shapes),
        out_shape=out_shape,
        compiler_params=_cparams(len(grid)),
        name=name,
    )


def _ada_kernel(c_ref, w_ref, b_ref, o_ref):
    c = c_ref[...]
    a = _silu(c).astype(BF16)
    o_ref[0, 0] = jnp.dot(a, w_ref[0].astype(BF16), preferred_element_type=F32) + b_ref[0]


def _ada_modulation(c, ada_w, ada_b):
    depth, d, n6 = ada_w.shape
    b = c.shape[0]
    rows = -(-b // SUBLANES) * SUBLANES
    c_pad = jnp.pad(c, ((0, rows - b), (0, 0)))
    tn = 1024
    per = d // tn
    out = pl.pallas_call(
        _ada_kernel,
        grid=(depth, n6 // tn),
        in_specs=[
            pl.BlockSpec((rows, d), lambda l, j: (0, 0)),
            pl.BlockSpec((1, d, tn), lambda l, j: (l, 0, j)),
            pl.BlockSpec((1, 1, tn), lambda l, j: (l, 0, j)),
        ],
        out_specs=pl.BlockSpec((1, 1, rows, tn), lambda l, j: (l, j // per, 0, j % per)),
        out_shape=jax.ShapeDtypeStruct((depth, N_MOD, rows, d), F32),
        compiler_params=_cparams(2),
        name="ada_modulation",
    )(c_pad, ada_w, ada_b.reshape(depth, 1, n6))
    return out[:, :, :b].reshape(depth, N_MOD, b, 1, d)


def _rope_kernel(ang_ref, cos_ref, sin_ref):
    ang = ang_ref[...]
    lane = lax.broadcasted_iota(jnp.int32, ang.shape, 1)
    first = (lane % DIFF_HEAD_DIM) < (DIFF_HEAD_DIM // 2)
    cos_ref[...] = jnp.cos(ang)
    s = jnp.sin(ang)
    sin_ref[...] = jnp.where(first, -s, s)


def _rope_tables(positions):
    m = positions.size
    half = DIFF_HEAD_DIM // 2
    inv_freq = ROPE_THETA ** (-jnp.arange(half, dtype=F32) * 2.0 / DIFF_HEAD_DIM)
    ang = positions.reshape(m, 1).astype(F32) * jnp.tile(inv_freq, LANES // half)[None, :]
    tm = min(1024, m)
    spec = pl.BlockSpec((tm, LANES), lambda i: (i, 0))
    return pl.pallas_call(
        _rope_kernel, grid=(m // tm,), in_specs=[spec], out_specs=[spec, spec],
        out_shape=[jax.ShapeDtypeStruct((m, LANES), F32)] * 2,
        compiler_params=_cparams(1), name="rope_tables",
    )(ang)


def _norm_mod_kernel(l_ref, x_ref, w_ref, sc_ref, sh_ref, o_ref):
    x = x_ref[...]
    ms = jnp.mean(x * x, axis=-1, keepdims=True)
    y = x * lax.rsqrt(ms + NORM_EPS) * w_ref[0]
    o_ref[...] = (y * (1.0 + sc_ref[0, 0, 0]) + sh_ref[0, 0, 0]).astype(BF16)


def _norm_mod(layer, x, norm_w, mod, scale_idx, shift_idx, seq):
    m, d = x.shape
    tm = min(512, seq)
    return _layer_call(
        _norm_mod_kernel, (m // tm,),
        [
            pl.BlockSpec((tm, d), lambda i, l: (i, 0)),
            pl.BlockSpec((1, 1, d), lambda i, l: (l[0], 0, 0)),
            pl.BlockSpec((1, 1, 1, 1, d), lambda i, l: (l[0], scale_idx, i * tm // seq, 0, 0)),
            pl.BlockSpec((1, 1, 1, 1, d), lambda i, l: (l[0], shift_idx, i * tm // seq, 0, 0)),
        ],
        pl.BlockSpec((tm, d), lambda i, l: (i, 0)),
        jax.ShapeDtypeStruct((m, d), BF16), name="norm_mod",
    )(layer, x, norm_w, mod, mod)


def _matmul(layer, xs, ws, w_x, w_off, extras, epilogue, out_dtypes, n, tm, tn, name):
    m = xs[0].shape[0]
    nx, nw, ne = len(xs), len(ws), len(extras)
    cast_w = ws[0].dtype == F32
    assert all((w.dtype == F32) == cast_w for w in ws)
    ij = (lambda a, b: (b, a)) if cast_w else (lambda a, b: (a, b))
    in_specs = [pl.BlockSpec((tm, x.shape[1]), lambda a, b, l: (ij(a, b)[0], 0)) for x in xs]
    for w, off in zip(ws, w_off):
        in_specs.append(pl.BlockSpec((1, w.shape[1], tn), lambda a, b, l, off=off: (l[0], 0, ij(a, b)[1] + off)))
    for _, bs, imap in extras:
        in_specs.append(pl.BlockSpec(bs, lambda a, b, l, imap=imap: imap(*ij(a, b), l)))
    out_specs = [pl.BlockSpec((tm, tn), lambda a, b, l: ij(a, b)) for _ in out_dtypes]
    out_shape = [jax.ShapeDtypeStruct((m, n), dt) for dt in out_dtypes]
    scratch = [pltpu.VMEM((w.shape[1], tn), BF16) for w in ws] if cast_w else []

    def kern(l_ref, *refs):
        x_refs, w_refs = refs[:nx], refs[nx:nx + nw]
        e_refs, o_refs = refs[nx + nw:nx + nw + ne], refs[nx + nw + ne:nx + nw + ne + len(out_dtypes)]
        if cast_w:
            wb_refs = refs[nx + nw + ne + len(out_dtypes):]

            @pl.when(pl.program_id(1) == 0)
            def _():
                for w_ref, wb_ref in zip(w_refs, wb_refs):
                    wb_ref[...] = w_ref[0].astype(BF16)

            w_tiles = [wb_ref[...] for wb_ref in wb_refs]
        else:
            w_tiles = [w_ref[0] for w_ref in w_refs]
        accs = [jnp.dot(x_refs[xi][...], w, preferred_element_type=F32) for xi, w in zip(w_x, w_tiles)]
        outs = epilogue(accs, [e[...] for e in e_refs])
        for o_ref, v in zip(o_refs, outs):
            o_ref[...] = v.astype(o_ref.dtype)

    grid = (n // tn, m // tm) if cast_w else (m // tm, n // tn)
    return _layer_call(kern, grid, in_specs, out_specs, out_shape, scratch_shapes=scratch, name=name)(
        layer, *xs, *ws, *[e[0] for e in extras])


def _residual_norm_kernel(l_ref, a_ref, w_ref, x_ref, gate_ref, nw_ref, sc_ref, sh_ref, xo_ref, ho_ref):
    acc = jnp.dot(a_ref[...], w_ref[0], preferred_element_type=F32)
    xn = x_ref[...] + gate_ref[0, 0, 0] * acc
    xo_ref[...] = xn
    ms = jnp.mean(xn * xn, axis=-1, keepdims=True)
    y = xn * lax.rsqrt(ms + NORM_EPS) * nw_ref[0]
    ho_ref[...] = (y * (1.0 + sc_ref[0, 0, 0]) + sh_ref[0, 0, 0]).astype(BF16)


def _matmul_residual_norm(layer, a, w, x, mod, gate_idx, norm_w, scale_idx, shift_idx, next_layer, seq, tm, name):
    m, k = a.shape
    depth, _, n = w.shape
    nl = (lambda l: jnp.minimum(l[0] + 1, depth - 1)) if next_layer else (lambda l: l[0])
    row = lambda i: i * tm // seq
    return _layer_call(
        _residual_norm_kernel, (m // tm,),
        [
            pl.BlockSpec((tm, k), lambda i, l: (i, 0)),
            pl.BlockSpec((1, k, n), lambda i, l: (l[0], 0, 0), pipeline_mode=pl.Buffered(1)),
            pl.BlockSpec((tm, n), lambda i, l: (i, 0)),
            pl.BlockSpec((1, 1, 1, 1, n), lambda i, l: (l[0], gate_idx, row(i), 0, 0)),
            pl.BlockSpec((1, 1, n), lambda i, l: (nl(l), 0, 0)),
            pl.BlockSpec((1, 1, 1, 1, n), lambda i, l: (nl(l), scale_idx, row(i), 0, 0)),
            pl.BlockSpec((1, 1, 1, 1, n), lambda i, l: (nl(l), shift_idx, row(i), 0, 0)),
        ],
        [pl.BlockSpec((tm, n), lambda i, l: (i, 0))] * 2,
        [jax.ShapeDtypeStruct((m, n), F32), jax.ShapeDtypeStruct((m, n), BF16)], name=name,
    )(layer, a, w, x, mod, norm_w, mod, mod)


def _mixer_out_kernel(l_ref, yd_ref, yg_ref, gd_ref, gg_ref, wd_ref, wg_ref, wo_ref, x_ref, gate_ref, nw_ref,
                      sc_ref, sh_ref, xo_ref, ho_ref):
    merged = (gd_ref[...] * jnp.dot(yd_ref[...], wd_ref[0], preferred_element_type=F32)
              + gg_ref[...] * jnp.dot(yg_ref[...], wg_ref[0], preferred_element_type=F32)).astype(BF16)
    acc = jnp.dot(merged, wo_ref[0], preferred_element_type=F32)
    xn = x_ref[...] + gate_ref[0, 0, 0] * acc
    xo_ref[...] = xn
    ms = jnp.mean(xn * xn, axis=-1, keepdims=True)
    y = xn * lax.rsqrt(ms + NORM_EPS) * nw_ref[0]
    ho_ref[...] = (y * (1.0 + sc_ref[0, 0, 0]) + sh_ref[0, 0, 0]).astype(BF16)


def _mixer_out(layer, y_diff, y_gdn, gates, w_d, w_g, w_o, x, mod, norm_w, seq):
    m, d = x.shape
    tm = min(256, seq)
    kd, kg = y_diff.shape[1], y_gdn.shape[1]
    row = lambda i: i * tm // seq
    resident = lambda k, n: pl.BlockSpec((1, k, n), lambda i, l: (l[0], 0, 0), pipeline_mode=pl.Buffered(1))
    modsp = lambda idx: pl.BlockSpec((1, 1, 1, 1, d), lambda i, l: (l[0], idx, row(i), 0, 0))
    return _layer_call(
        _mixer_out_kernel, (m // tm,),
        [
            pl.BlockSpec((tm, kd), lambda i, l: (i, 0)),
            pl.BlockSpec((tm, kg), lambda i, l: (i, 0)),
            pl.BlockSpec((tm, d), lambda i, l: (i, 0)),
            pl.BlockSpec((tm, d), lambda i, l: (i, 1)),
            resident(kd, d), resident(kg, d), resident(d, d),
            pl.BlockSpec((tm, d), lambda i, l: (i, 0)),
            modsp(2),
            pl.BlockSpec((1, 1, d), lambda i, l: (l[0], 0, 0)),
            modsp(4), modsp(3),
        ],
        [pl.BlockSpec((tm, d), lambda i, l: (i, 0))] * 2,
        [jax.ShapeDtypeStruct((m, d), F32), jax.ShapeDtypeStruct((m, d), BF16)], name="mixer_out",
    )(layer, y_diff, y_gdn, gates, gates, w_d, w_g, w_o, x, mod, norm_w, mod, mod)


def _group_sumsq(x, group_ones):
    sq = x * x
    hi = sq.astype(BF16)
    lo = (sq - hi.astype(F32)).astype(BF16)
    return (jnp.dot(hi, group_ones, preferred_element_type=F32)
            + jnp.dot(lo, group_ones, preferred_element_type=F32))


def _qk_prep_kernel(l_ref, q_ref, k_ref, v_ref, cos_ref, sin_ref, qw_ref, kw_ref, qo_ref, ko_ref, vo_ref):
    cos = cos_ref[...]
    sin = sin_ref[...]
    row = lax.broadcasted_iota(jnp.int32, (LANES, LANES), 0) // DIFF_HEAD_DIM
    col = lax.broadcasted_iota(jnp.int32, (LANES, LANES), 1) // DIFF_HEAD_DIM
    ones = (row == col).astype(BF16)
    lane = lax.broadcasted_iota(jnp.int32, cos.shape, 1)
    first = (lane % DIFF_HEAD_DIM) < (DIFF_HEAD_DIM // 2)
    half = DIFF_HEAD_DIM // 2

    def norm_rope(x, w, scale):
        ms = _group_sumsq(x, ones) * (1.0 / DIFF_HEAD_DIM)
        y = x * lax.rsqrt(ms + NORM_EPS) * w
        partner = jnp.where(first, pltpu.roll(y, LANES - half, 1), pltpu.roll(y, half, 1))
        return (y * cos + partner * sin) * scale

    q_scale = DIFF_HEAD_DIM ** -0.5 * math.log2(math.e)
    ones_blk = jnp.ones((cos.shape[0], LANES), BF16)
    for h in range(N_DIFF_HEADS):
        sl = slice(h * LANES, (h + 1) * LANES)
        qo_ref[:, sl] = norm_rope(q_ref[:, sl], qw_ref[0, :, sl], q_scale).astype(BF16)
        ko_ref[:, sl] = norm_rope(k_ref[:, sl], kw_ref[0, :, sl], 1.0).astype(BF16)
        vo_ref[:, 2 * h * LANES:(2 * h + 1) * LANES] = v_ref[:, sl].astype(BF16)
        vo_ref[:, (2 * h + 1) * LANES:(2 * h + 2) * LANES] = ones_blk


def _qk_prep(layer, proj, cos, sin, qn_w, kn_w):
    m = proj.shape[0]
    tm = min(512, m)
    w = DIFF_WIDTH
    rep = w // DIFF_HEAD_DIM
    depth = qn_w.shape[0]
    qw = jnp.tile(qn_w, (1, rep)).reshape(depth, 1, w)
    kw = jnp.tile(kn_w, (1, rep)).reshape(depth, 1, w)
    blk = lambda c: pl.BlockSpec((tm, w), lambda i, l, c=c: (i, c))
    tab = pl.BlockSpec((tm, LANES), lambda i, l: (i, 0))
    wsp = pl.BlockSpec((1, 1, w), lambda i, l: (l[0], 0, 0))
    osp = pl.BlockSpec((tm, w), lambda i, l: (i, 0))
    vsp = pl.BlockSpec((tm, 2 * w), lambda i, l: (i, 0))
    return _layer_call(
        _qk_prep_kernel, (m // tm,), [blk(0), blk(1), blk(2), tab, tab, wsp, wsp], [osp, osp, vsp],
        [jax.ShapeDtypeStruct((m, w), BF16)] * 2 + [jax.ShapeDtypeStruct((m, 2 * w), BF16)], name="diff_qk_prep",
    )(layer, proj, proj, proj, cos, sin, qw, kw)


def _diff_attn_kernel(l_ref, sc_ref, q_ref, k_ref, v_ref, w_ref, o_ref):
    lam = sc_ref[0]
    out_scale = sc_ref[1]
    k = k_ref[...]
    nt = (((1,), (1,)), ((), ()))
    sub = min(ATTN_SUB_ROWS, q_ref.shape[0])

    def branch(qm):
        s = lax.dot_general(qm, k, nt, preferred_element_type=F32)
        p = jnp.exp2(s - jnp.max(s, axis=-1, keepdims=True)).astype(BF16)
        ov = jnp.dot(p, v_ref[...], preferred_element_type=F32)
        return ov[:, :LANES] / ov[:, LANES:]

    for r in range(q_ref.shape[0] // sub):
        rows = slice(r * sub, (r + 1) * sub)
        q = q_ref[rows, :]
        lane = lax.broadcasted_iota(jnp.int32, q.shape, 1)
        zero = jnp.zeros_like(q)
        q0 = jnp.where(lane < DIFF_HEAD_DIM, q, zero)
        q1 = jnp.where(lane >= DIFF_HEAD_DIM, q, zero)
        o = branch(q0) - lam * branch(q1)
        ms = jnp.mean(o * o, axis=-1, keepdims=True)
        o_ref[rows, :] = (o * lax.rsqrt(ms + NORM_EPS) * w_ref[0] * out_scale).astype(BF16)


def _diff_attention(layer, scalars, q, k, v, subln_w, batch, seq):
    m = q.shape[0]
    tq = min(ATTN_Q_ROWS, seq)
    nq = seq // tq
    depth = subln_w.shape[0]
    return _layer_call(
        _diff_attn_kernel, (batch, N_DIFF_HEADS, nq),
        [
            pl.BlockSpec(memory_space=pltpu.SMEM),
            pl.BlockSpec((tq, LANES), lambda b, h, i, l: (b * nq + i, h)),
            pl.BlockSpec((seq, LANES), lambda b, h, i, l: (b, h)),
            pl.BlockSpec((seq, 2 * LANES), lambda b, h, i, l: (b, h)),
            pl.BlockSpec((1, 1, LANES), lambda b, h, i, l: (l[0], 0, 0)),
        ],
        pl.BlockSpec((tq, LANES), lambda b, h, i, l: (b * nq + i, h)),
        jax.ShapeDtypeStruct((m, DIFF_WIDTH), BF16), name="diff_attention",
    )(layer, scalars, q, k, v, subln_w.reshape(depth, 1, LANES))


def _split3(x):
    hi = x.astype(BF16)
    r1 = x - hi.astype(F32)
    mid = r1.astype(BF16)
    lo = (r1 - mid.astype(F32)).astype(BF16)
    return hi, mid, lo


def _gdn_prep_kernel(l_ref, main_ref, prev_ref, next_ref, small_ref, cw_ref, alog_ref, dtb_ref,
                     q_ref, k_ref, kt_ref, v_ref, beta_ref, gc_ref, xe_ref):
    t = pl.program_id(1)
    nt = pl.num_programs(1)
    ts = main_ref.shape[0]
    pad = CONV_WIDTH // 2
    xe_ref[0:SUBLANES, :] = jnp.where(t > 0, prev_ref[...], 0.0)
    xe_ref[SUBLANES:SUBLANES + ts, :] = main_ref[...]
    xe_ref[SUBLANES + ts:, :] = jnp.where(t < nt - 1, next_ref[...], 0.0)
    cw = cw_ref[0]
    first = SUBLANES - pad
    acc = cw[0:1, :] * xe_ref[first:first + ts, :]
    for j in range(1, CONV_WIDTH):
        acc = acc + cw[j:j + 1, :] * xe_ref[first + j:first + j + ts, :]
    y = _silu(acc)
    for h in range(N_GDN_HEADS):
        sl = slice(h * LANES, (h + 1) * LANES)
        qh = y[:, sl]
        kh = y[:, GDN_WIDTH + h * LANES:GDN_WIDTH + (h + 1) * LANES]
        q_ref[:, sl] = qh * (lax.rsqrt(jnp.sum(qh * qh, axis=-1, keepdims=True) + NORM_EPS)
                             * GDN_HEAD_DIM ** -0.5)
        kn = kh * lax.rsqrt(jnp.sum(kh * kh, axis=-1, keepdims=True) + NORM_EPS)
        k_ref[:, sl] = kn
        kt_ref[sl, :] = kn.T
    v_ref[...] = y[:, 2 * GDN_WIDTH:]

    nh2 = 2 * N_GDN_HEADS
    small = small_ref[...]
    beta_ref[...] = jax.nn.sigmoid(small[:, :nh2])
    g = -jnp.exp(alog_ref[0]) * jax.nn.softplus(small[:, nh2:2 * nh2] + dtb_ref[0])
    r = lax.broadcasted_iota(jnp.int32, (ts, ts), 0)
    c = lax.broadcasted_iota(jnp.int32, (ts, ts), 1)
    same = (r // CHUNK) == (c // CHUNK)
    pre = (same & (c <= r)).astype(BF16)
    suf = (same & (c >= r)).astype(BF16)
    parts = _split3(g)
    cs_f = sum(jnp.dot(pre, p, preferred_element_type=F32) for p in parts)
    cs_b = sum(jnp.dot(suf, p, preferred_element_type=F32) for p in parts)
    lane = lax.broadcasted_iota(jnp.int32, g.shape, 1)
    gc_ref[...] = jnp.where(lane < N_GDN_HEADS, cs_f, cs_b)


def _gdn_prep(layer, proj, small, conv_w, a_log, dt_bias, batch, seq):
    m = proj.shape[0]
    ts = min(256, seq)
    nts = seq // ts
    w3 = 3 * GDN_WIDTH
    cb = (3 * DIFF_WIDTH) // w3
    assert cb * w3 == 3 * DIFF_WIDTH
    depth = conv_w.shape[0]
    nh2 = 2 * N_GDN_HEADS
    hb = ts // SUBLANES
    last = m // SUBLANES - 1
    osp = pl.BlockSpec((ts, GDN_WIDTH), lambda b, t, l: (b * nts + t, 0))
    ssp = pl.BlockSpec((ts, nh2), lambda b, t, l: (b * nts + t, 0))
    return _layer_call(
        _gdn_prep_kernel, (batch, nts),
        [
            pl.BlockSpec((ts, w3), lambda b, t, l: (b * nts + t, cb)),
            pl.BlockSpec((SUBLANES, w3), lambda b, t, l: (jnp.maximum((b * nts + t) * hb - 1, 0), cb)),
            pl.BlockSpec((SUBLANES, w3), lambda b, t, l: (jnp.minimum((b * nts + t + 1) * hb, last), cb)),
            pl.BlockSpec((ts, LANES), lambda b, t, l: (b * nts + t, 0)),
            pl.BlockSpec((1, CONV_WIDTH, w3), lambda b, t, l: (l[0], 0, 0)),
            pl.BlockSpec((1, 1, nh2), lambda b, t, l: (l[0], 0, 0)),
            pl.BlockSpec((1, 1, nh2), lambda b, t, l: (l[0], 0, 0)),
        ],
        [osp, osp, pl.BlockSpec((GDN_WIDTH, ts), lambda b, t, l: (0, b * nts + t)), osp, ssp, ssp],
        [jax.ShapeDtypeStruct((m, GDN_WIDTH), F32)] * 2 + [jax.ShapeDtypeStruct((GDN_WIDTH, m), F32)]
        + [jax.ShapeDtypeStruct((m, GDN_WIDTH), F32)] + [jax.ShapeDtypeStruct((m, nh2), F32)] * 2,
        scratch_shapes=[pltpu.VMEM((ts + 2 * SUBLANES, w3), F32)], name="gdn_prep",
    )(layer, proj, proj, proj, small, conv_w, a_log.reshape(depth, 1, nh2), dt_bias.reshape(depth, 1, nh2))


def _chunk_masks(backward):
    r = lax.broadcasted_iota(jnp.int32, (CHUNK, CHUNK), 0)
    c = lax.broadcasted_iota(jnp.int32, (CHUNK, CHUNK), 1)
    if backward:
        return r <= c, r < c
    return r >= c, r > c


def _gdn_l_kernel(l_ref, k_ref, beta_ref, gc_ref, gcrow_ref, lf_ref, lb_ref):
    h = pl.program_id(1)
    n_chunks = k_ref.shape[0] // CHUNK
    nt = (((1,), (1,)), ((), ()))
    lane16 = lax.broadcasted_iota(jnp.int32, (CHUNK, 2 * N_GDN_HEADS), 1)
    sub16 = lax.broadcasted_iota(jnp.int32, (2 * N_GDN_HEADS, CHUNK), 0)

    def body(n, carry):
        rows = pl.ds(pl.multiple_of(n * CHUNK, CHUNK), CHUNK)
        kb = k_ref[rows, :].astype(BF16)
        kk = lax.dot_general(kb, kb, nt, preferred_element_type=F32)
        beta = beta_ref[rows, :]
        gc = gc_ref[rows, :]
        gcr = gcrow_ref[0, n]
        for d, out_ref in ((0, lf_ref), (1, lb_ref)):
            col = d * N_GDN_HEADS + h
            bcol = jnp.sum(jnp.where(lane16 == col, beta, 0.0), axis=1, keepdims=True)
            gcol = jnp.sum(jnp.where(lane16 == col, gc, 0.0), axis=1, keepdims=True)
            grow = jnp.sum(jnp.where(sub16 == col, gcr, 0.0), axis=0, keepdims=True)
            _, strict = _chunk_masks(d == 1)
            dec = jnp.exp(jnp.where(strict, gcol - grow, NEG_BIG))
            out_ref[0, rows, :] = bcol * kk * dec
        return carry

    lax.fori_loop(0, n_chunks, body, 0, unroll=4)


def _gdn_build_l(layer, k, beta, gc, gc_rows, batch, seq):
    nh2 = 2 * N_GDN_HEADS
    n_chunks = seq // CHUNK
    osp = pl.BlockSpec((1, seq, CHUNK), lambda b, h, l: (b * N_GDN_HEADS + h, 0, 0))
    return _layer_call(
        _gdn_l_kernel, (batch, N_GDN_HEADS),
        [
            pl.BlockSpec((seq, LANES), lambda b, h, l: (b, h)),
            pl.BlockSpec((seq, nh2), lambda b, h, l: (b, 0)),
            pl.BlockSpec((seq, nh2), lambda b, h, l: (b, 0)),
            pl.BlockSpec((1, n_chunks, nh2, CHUNK), lambda b, h, l: (b, 0, 0, 0)),
        ],
        [osp, osp],
        [jax.ShapeDtypeStruct((batch * N_GDN_HEADS, seq, CHUNK), F32)] * 2, name="gdn_build_l",
    )(layer, k, beta, gc, gc_rows)


def _tri_inverse_kernel(l_ref, t_ref):
    sub = lax.broadcasted_iota(jnp.int32, (SUBLANES, LANES), 0)
    nblk = CHUNK // SUBLANES
    for i in range(CHUNK):
        live = i // SUBLANES + 1
        accs = [jnp.zeros((SUBLANES, LANES), F32) for _ in range(live)]
        for j in range(i):
            lij = l_ref[i, pl.ds(j, 1), :]
            for cb in range(j // SUBLANES + 1):
                accs[cb] = accs[cb] + lij * t_ref[j, cb * SUBLANES:(cb + 1) * SUBLANES, :]
        for cb in range(nblk):
            if cb < live - 1:
                val = -accs[cb]
            elif cb == live - 1:
                val = jnp.where(sub == i % SUBLANES, 1.0, 0.0) - accs[cb]
            else:
                val = jnp.zeros((SUBLANES, LANES), F32)
            t_ref[i, cb * SUBLANES:(cb + 1) * SUBLANES, :] = val


def _tri_inverse(l_all):
    g = l_all.shape[-1]
    spec = pl.BlockSpec((CHUNK, CHUNK, LANES), lambda i: (0, 0, i))
    return pl.pallas_call(
        _tri_inverse_kernel, grid=(g // LANES,), in_specs=[spec], out_specs=spec,
        out_shape=jax.ShapeDtypeStruct(l_all.shape, F32),
        compiler_params=_cparams(1), name="gdn_tri_inverse",
    )(l_all)


def _gdn_pre_kernel(l_ref, q_ref, k_ref, kt_ref, v_ref, beta_ref, gc_ref, gcrow_ref, gcpair_ref, tf_ref, tb_ref,
                    mqf_ref, mqb_ref, rf_ref, rb_ref, glf_ref, glb_ref, ol_ref):
    h = pl.program_id(1)
    n_local = q_ref.shape[0] // CHUNK
    dk = GDN_HEAD_DIM
    nt = (((1,), (1,)), ((), ()))
    lane16 = lax.broadcasted_iota(jnp.int32, (CHUNK, 2 * N_GDN_HEADS), 1)
    sub16 = lax.broadcasted_iota(jnp.int32, (2 * N_GDN_HEADS, CHUNK), 0)
    sub16p = lax.broadcasted_iota(jnp.int32, (2 * N_GDN_HEADS, LANES), 0)
    lane_half = lax.broadcasted_iota(jnp.int32, (1, LANES), 1) // CHUNK
    outs = ((tf_ref, mqf_ref, rf_ref, glf_ref), (tb_ref, mqb_ref, rb_ref, glb_ref))
    per_pair = LANES // CHUNK
    stage1 = []
    for c in range(n_local):
        rows = slice(c * CHUNK, (c + 1) * CHUNK)
        pair, half = c // per_pair, c % per_pair
        qc = q_ref[rows, :]
        kc = k_ref[rows, :]
        vc = v_ref[rows, :]
        kt_pair = kt_ref[:, pair * LANES:(pair + 1) * LANES]
        qk = lax.dot_general(qc.astype(BF16), kc.astype(BF16), nt, preferred_element_type=F32)
        for d in (0, 1):
            col = d * N_GDN_HEADS + h
            bcol = jnp.sum(jnp.where(lane16 == col, beta_ref[rows, :], 0.0), axis=1, keepdims=True)
            gcol = jnp.sum(jnp.where(lane16 == col, gc_ref[rows, :], 0.0), axis=1, keepdims=True)
            grow = jnp.sum(jnp.where(sub16 == col, gcrow_ref[0, c], 0.0), axis=0, keepdims=True)
            glast = gcol[0:1, :] if d == 1 else gcol[CHUNK - 1:CHUNK, :]
            incl, _ = _chunk_masks(d == 1)
            attn = (qk * jnp.exp(jnp.where(incl, gcol - grow, NEG_BIG))).astype(BF16)
            gam = jnp.exp(gcol)
            x = jnp.concatenate([bcol * vc, (bcol * gam) * kc], axis=1).astype(BF16)
            t = outs[d][0][0, rows, :]
            th = t.astype(BF16)
            tl = (t - th.astype(F32)).astype(BF16)
            uw = (jnp.dot(th, x, preferred_element_type=F32)
                  + jnp.dot(tl, x, preferred_element_type=F32)).astype(BF16)
            grow_pair = jnp.sum(jnp.where(sub16p == col, gcpair_ref[0, pair], 0.0), axis=0, keepdims=True)
            tail = jnp.exp(jnp.where(lane_half == half, glast - grow_pair, NEG_BIG))
            ktil_t = (kt_pair * tail).astype(BF16)
            stage1.append((c, d, attn, uw, ktil_t, gam * qc, jnp.exp(glast)))
    o_local = {}
    for c, d, attn, uw, ktil_t, gq, gl in stage1:
        _, mq_ref, r_ref, gl_ref = outs[d]
        pos = (n_local - 1 - c) if d == 1 else c
        awu = jnp.dot(attn, uw, preferred_element_type=F32)
        kwu = jnp.dot(ktil_t, jnp.concatenate([uw] * per_pair, axis=0), preferred_element_type=F32)
        mq_ref[0, pos, 0:dk, :] = kwu[:, dk:].astype(BF16)
        mq_ref[0, pos, dk:dk + CHUNK, :] = (gq - awu[:, dk:]).astype(BF16)
        r_ref[0, pos] = kwu[:, :dk].astype(BF16)
        gl_ref[0, pos] = jnp.broadcast_to(gl, (1, LANES))
        o_local[c] = o_local[c] + awu[:, :dk] if c in o_local else awu[:, :dk]
    for c, val in o_local.items():
        ol_ref[c * CHUNK:(c + 1) * CHUNK, :] = val


def _gdn_pre(layer, q, k, k_t, v, beta, gc, gc_rows, gc_pairs, t_f, t_b, batch, seq):
    nh2 = 2 * N_GDN_HEADS
    n_chunks = seq // CHUNK
    cg = min(GDN_PRE_CHUNKS, n_chunks)
    ng = n_chunks // cg
    rows = cg * CHUNK
    dk = GDN_HEAD_DIM
    bh = batch * N_GDN_HEADS
    hsp = pl.BlockSpec((rows, LANES), lambda b, h, g, l: (b * ng + g, h))
    ssp = pl.BlockSpec((rows, nh2), lambda b, h, g, l: (b * ng + g, 0))
    tsp = pl.BlockSpec((1, rows, CHUNK), lambda b, h, g, l: (b * N_GDN_HEADS + h, g, 0))

    def osp(r, mirrored):
        if mirrored:
            return pl.BlockSpec((1, cg, r, LANES), lambda b, h, g, l: (b * N_GDN_HEADS + h, ng - 1 - g, 0, 0))
        return pl.BlockSpec((1, cg, r, LANES), lambda b, h, g, l: (b * N_GDN_HEADS + h, g, 0, 0))

    shapes = [((dk + CHUNK), BF16), (dk, BF16), (1, F32)]
    out_specs, out_shape = [], []
    for r, dt in shapes:
        for mirrored in (False, True):
            out_specs.append(osp(r, mirrored))
            out_shape.append(jax.ShapeDtypeStruct((bh, n_chunks, r, LANES), dt))
    return _layer_call(
        _gdn_pre_kernel, (batch, N_GDN_HEADS, ng),
        [hsp, hsp, pl.BlockSpec((LANES, rows), lambda b, h, g, l: (h, b * ng + g)), hsp, ssp, ssp,
         pl.BlockSpec((1, cg, nh2, CHUNK), lambda b, h, g, l: (b, g, 0, 0)),
         pl.BlockSpec((1, rows // LANES, nh2, LANES), lambda b, h, g, l: (b, g, 0, 0)),
         tsp, tsp],
        out_specs + [hsp], out_shape + [jax.ShapeDtypeStruct(q.shape, F32)], name="gdn_chunk_pre",
    )(layer, q, k, k_t, v, beta, gc, gc_rows, gc_pairs, t_f, t_b)


def _gdn_state_kernel(l_ref, mqf_ref, mqb_ref, rf_ref, rb_ref, glf_ref, glb_ref, of_ref, ob_ref, st_ref):
    g = pl.program_id(1)
    n_heads, n_local = mqf_ref.shape[0], mqf_ref.shape[1]
    dk = GDN_HEAD_DIM

    @pl.when(g == 0)
    def _():
        st_ref[...] = jnp.zeros_like(st_ref)

    zero = jnp.zeros((dk, dk), BF16)
    states = [(st_ref[2 * hh], st_ref[2 * hh + 1]) for hh in range(n_heads)]
    for c in range(n_local):
        for hh in range(n_heads):
            sf, sb = states[hh]
            mq = jnp.concatenate([mqf_ref[hh, c], mqb_ref[hh, c]], axis=1)
            bd = jnp.concatenate([jnp.concatenate([sf.astype(BF16), zero], axis=1),
                                  jnp.concatenate([zero, sb.astype(BF16)], axis=1)], axis=0)
            res = jnp.dot(mq, bd, preferred_element_type=F32)
            hl = slice(hh * LANES, (hh + 1) * LANES)
            of_ref[c * CHUNK:(c + 1) * CHUNK, hl] = res[dk:, :dk]
            cb = n_local - 1 - c
            ob_ref[cb * CHUNK:(cb + 1) * CHUNK, hl] = res[dk:, dk:]
            states[hh] = (glf_ref[hh, c] * sf - res[:dk, :dk] + rf_ref[hh, c].astype(F32),
                          glb_ref[hh, c] * sb - res[:dk, dk:] + rb_ref[hh, c].astype(F32))
    for hh in range(n_heads):
        st_ref[2 * hh] = states[hh][0]
        st_ref[2 * hh + 1] = states[hh][1]


def _gdn_state(layer, pre, batch, seq):
    n_chunks = seq // CHUNK
    cg = min(GDN_SCAN_CHUNKS, n_chunks)
    ng = n_chunks // cg
    nh = N_GDN_HEADS
    dk = GDN_HEAD_DIM
    m = batch * seq
    in_specs = [pl.BlockSpec((nh, cg) + a.shape[2:], lambda b, g, l: (b, g, 0, 0)) for a in pre]
    return _layer_call(
        _gdn_state_kernel, (batch, ng), in_specs,
        [pl.BlockSpec((cg * CHUNK, GDN_WIDTH), lambda b, g, l: (b * ng + g, 0)),
         pl.BlockSpec((cg * CHUNK, GDN_WIDTH), lambda b, g, l: (b * ng + ng - 1 - g, 0))],
        [jax.ShapeDtypeStruct((m, GDN_WIDTH), F32)] * 2,
        scratch_shapes=[pltpu.VMEM((2 * nh, dk, dk), F32)], name="gdn_state_scan",
    )(layer, *pre)


def _gdn_out_kernel(l_ref, of_ref, ob_ref, ol_ref, z_ref, nw_ref, y_ref):
    for h in range(N_GDN_HEADS):
        hl = slice(h * LANES, (h + 1) * LANES)
        o = of_ref[:, hl] + ob_ref[:, hl] + ol_ref[:, hl]
        ms = jnp.mean(o * o, axis=-1, keepdims=True)
        y_ref[:, hl] = (o * lax.rsqrt(ms + NORM_EPS) * nw_ref[0] * _silu(z_ref[:, hl])).astype(BF16)


def _gdn_out(layer, o_f, o_b, o_local, proj, norm_w):
    m = o_f.shape[0]
    tm = min(512, m)
    depth = norm_w.shape[0]
    zcb = (3 * DIFF_WIDTH + 3 * GDN_WIDTH) // GDN_WIDTH
    osp = pl.BlockSpec((tm, GDN_WIDTH), lambda i, l: (i, 0))
    return _layer_call(
        _gdn_out_kernel, (m // tm,),
        [osp, osp, osp, pl.BlockSpec((tm, GDN_WIDTH), lambda i, l: (i, zcb)),
         pl.BlockSpec((1, 1, LANES), lambda i, l: (l[0], 0, 0))],
        osp, jax.ShapeDtypeStruct((m, GDN_WIDTH), BF16), name="gdn_out",
    )(layer, o_f, o_b, o_local, proj, norm_w.reshape(depth, 1, LANES))


def _gated_deltanet(layer, proj, small, conv_w, a_log, dt_bias, norm_w, batch, seq):
    q, k, k_t, v, beta, gc = _gdn_prep(layer, proj, small, conv_w, a_log, dt_bias, batch, seq)
    n_chunks = seq // CHUNK
    nh2 = 2 * N_GDN_HEADS
    gc_rows = gc.reshape(batch, n_chunks, CHUNK, nh2).transpose(0, 1, 3, 2)
    gc_pairs = gc.reshape(batch, seq // LANES, LANES, nh2).transpose(0, 1, 3, 2)
    l_f, l_b = _gdn_build_l(layer, k, beta, gc, gc_rows, batch, seq)
    g0 = batch * N_GDN_HEADS * n_chunks
    t_f = _tri_inverse(l_f.reshape(g0, CHUNK, CHUNK).transpose(1, 2, 0))
    t_b = _tri_inverse(l_b.reshape(g0, CHUNK, CHUNK).transpose(2, 1, 0))
    t_f = t_f.transpose(2, 0, 1).reshape(batch * N_GDN_HEADS, seq, CHUNK)
    t_b = t_b.transpose(2, 1, 0).reshape(batch * N_GDN_HEADS, seq, CHUNK)
    *pre, o_local = _gdn_pre(layer, q, k, k_t, v, beta, gc, gc_rows, gc_pairs, t_f, t_b, batch, seq)
    o_f, o_b = _gdn_state(layer, pre, batch, seq)
    return _gdn_out(layer, o_f, o_b, o_local, proj, norm_w)


def _layer(l, x, h, cos, sin, mod, lam_inits, p, batch, seq):
    m, d = x.shape
    layer = jnp.reshape(l, (1,)).astype(jnp.int32)
    tm = min(1024, seq)
    ident = lambda accs, extras: accs

    (proj,) = _matmul(layer, [h], [p["w_main"]], [0], [0], [], ident, [F32], MAIN_COLS, tm, 1024, "proj_main")
    (small,) = _matmul(layer, [h], [p["w_small"]], [0], [0], [], ident, [F32], LANES, tm, LANES, "proj_small")
    (gates,) = _matmul(layer, [h], [p["w_gates"]], [0], [0], [],
                       lambda accs, extras: [jax.nn.sigmoid(accs[0])], [BF16], 2 * d, tm, 1024, "proj_gates")

    lam_init = lam_inits[l]
    lv = p["diff_lambda"][l].astype(F32)
    lam = jnp.exp(jnp.sum(lv[0] * lv[1])) - jnp.exp(jnp.sum(lv[2] * lv[3])) + lam_init
    scalars = jnp.stack([lam, 1.0 - lam_init]).astype(F32)
    dq, dk, dv = _qk_prep(layer, proj, cos, sin, p["diff_qn_w"], p["diff_kn_w"])
    y_diff = _diff_attention(layer, scalars, dq, dk, dv, p["diff_subln_w"], batch, seq)

    y_gdn = _gated_deltanet(layer, proj, small, p["gdn_conv_w"], p["gdn_a_log"], p["gdn_dt_bias"],
                            p["gdn_norm_w"], batch, seq)

    x, h = _mixer_out(layer, y_diff, y_gdn, gates, p["w_branch_diff"], p["w_branch_gdn"], p["w_out"], x, mod,
                      p["norm_ffn_w"], seq)
    f = p["ffn_w_down"].shape[1]
    tf = 512
    (act,) = _matmul(layer, [h], [p["ffn_w_up"], p["ffn_w_up"]], [0, 0], [0, f // tf], [],
                     lambda accs, extras: [_silu(accs[0]) * accs[1]], [BF16], f, tm, tf, "ffn_up")
    return tuple(_matmul_residual_norm(layer, act, p["ffn_w_down"], x, mod, 5, p["norm_mix_w"], 1, 0, True, seq,
                                       min(256, seq), "ffn_down"))


def kernel(x, c, positions, ada_w, ada_b, norm_mix_w, norm_ffn_w, w_in, diff_qn_w, diff_kn_w, diff_lambda,
           diff_subln_w, gdn_conv_w, gdn_a_log, gdn_dt_bias, gdn_norm_w, w_branch_diff, w_branch_gdn, w_out,
           ffn_w_up, ffn_w_down):
    batch, seq, d = x.shape
    depth = ada_w.shape[0]
    mod = _ada_modulation(c, ada_w, ada_b)
    cos, sin = _rope_tables(positions)
    lam_inits = jnp.asarray([0.8 - 0.6 * math.exp(-0.3 * i) for i in range(depth)], F32)
    small_w = jnp.pad(w_in[:, :, MAIN_COLS:MAIN_COLS + SMALL_COLS], ((0, 0), (0, 0), (0, LANES - SMALL_COLS)))
    p = {
        "norm_mix_w": norm_mix_w.reshape(depth, 1, d), "norm_ffn_w": norm_ffn_w.reshape(depth, 1, d),
        "w_main": w_in,
        "w_small": small_w.astype(BF16),
        "w_gates": w_in[:, :, MAIN_COLS + SMALL_COLS:].astype(BF16),
        "diff_qn_w": diff_qn_w, "diff_kn_w": diff_kn_w, "diff_lambda": diff_lambda, "diff_subln_w": diff_subln_w,
        "gdn_conv_w": gdn_conv_w, "gdn_a_log": gdn_a_log, "gdn_dt_bias": gdn_dt_bias, "gdn_norm_w": gdn_norm_w,
        "w_branch_diff": w_branch_diff.astype(BF16), "w_branch_gdn": w_branch_gdn.astype(BF16),
        "w_out": w_out.astype(BF16), "ffn_w_up": ffn_w_up, "ffn_w_down": ffn_w_down.astype(BF16),
    }
    x0 = x.reshape(batch * seq, d)
    h0 = _norm_mod(jnp.zeros((1,), jnp.int32), x0, p["norm_mix_w"], mod, 1, 0, seq)
    body = lambda l, carry: _layer(l, carry[0], carry[1], cos, sin, mod, lam_inits, p, batch, seq)
    out, _ = lax.fori_loop(0, depth, body, (x0, h0))
    return out.reshape(batch, seq, d)
```

```python
import functools
import math

import jax
import jax.numpy as jnp
from jax import lax
from jax.experimental import pallas as pl
from jax.experimental.pallas import tpu as pltpu

F32 = jnp.float32
BF16 = jnp.bfloat16

N_DIFF_HEADS = 8
DIFF_HEAD_DIM = 64
DIFF_WIDTH = N_DIFF_HEADS * 2 * DIFF_HEAD_DIM
N_GDN_HEADS = 8
GDN_HEAD_DIM = 128
GDN_WIDTH = N_GDN_HEADS * GDN_HEAD_DIM
CONV_WIDTH = 5
CHUNK = 64
ROPE_THETA = 10000.0
NORM_EPS = 1e-6
N_MOD = 6
LANES = 128
SUBLANES = 8
NEG_BIG = -1e30
GDN_PRE_CHUNKS = 16
GDN_SCAN_CHUNKS = 8
ATTN_Q_ROWS = 2048
ATTN_SUB_ROWS = 256

MAIN_COLS = 3 * DIFF_WIDTH + 4 * GDN_WIDTH
SMALL_COLS = 4 * N_GDN_HEADS
VMEM_LIMIT = 48 * 1024 * 1024


def _silu(x):
    return x * jax.nn.sigmoid(x)


def _cparams(n_axes, vmem=VMEM_LIMIT):
    return pltpu.CompilerParams(dimension_semantics=("arbitrary",) * n_axes, vmem_limit_bytes=vmem)


def _layer_call(kernel, grid, in_specs, out_specs, out_shape, scratch_shapes=(), name=None):
    return pl.pallas_call(
        kernel,
        grid_spec=pltpu.PrefetchScalarGridSpec(
            num_scalar_prefetch=1, grid=grid, in_specs=in_specs, out_specs=out_specs,
            scratch_shapes=scratch_shapes),
        out_shape=out_shape,
        compiler_params=_cparams(len(grid)),
        name=name,
    )


def _ada_kernel(c_ref, w_ref, b_ref, o_ref):
    @pl.when(pl.program_id(1) == 0)
    def _():
        o_ref[0] = jnp.broadcast_to(b_ref[0], o_ref.shape[1:])

    a = _silu(c_ref[...]).astype(BF16)
    o_ref[0] += jnp.dot(a, w_ref[0].astype(BF16), preferred_element_type=F32)


def _ada_modulation(c, ada_w, ada_b):
    depth, d, n6 = ada_w.shape
    b = c.shape[0]
    rows = -(-b // SUBLANES) * SUBLANES
    c_pad = jnp.pad(c, ((0, rows - b), (0, 0)))
    tk = 128
    out = pl.pallas_call(
        _ada_kernel,
        grid=(depth, d // tk),
        in_specs=[
            pl.BlockSpec((rows, tk), lambda l, k: (0, k)),
            pl.BlockSpec((1, tk, n6), lambda l, k: (l, k, 0)),
            pl.BlockSpec((1, 1, n6), lambda l, k: (l, 0, 0)),
        ],
        out_specs=pl.BlockSpec((1, rows, n6), lambda l, k: (l, 0, 0)),
        out_shape=jax.ShapeDtypeStruct((depth, rows, n6), F32),
        compiler_params=_cparams(2),
        name="ada_modulation",
    )(c_pad, ada_w, ada_b.reshape(depth, 1, n6))
    return out[:, :b].reshape(depth, b, N_MOD, d).transpose(0, 2, 1, 3).reshape(depth, N_MOD, b, 1, d)


def _rope_kernel(ang_ref, cos_ref, sin_ref):
    ang = ang_ref[...]
    lane = lax.broadcasted_iota(jnp.int32, ang.shape, 1)
    first = (lane % DIFF_HEAD_DIM) < (DIFF_HEAD_DIM // 2)
    cos_ref[...] = jnp.cos(ang)
    s = jnp.sin(ang)
    sin_ref[...] = jnp.where(first, -s, s)


def _rope_tables(positions):
    m = positions.size
    half = DIFF_HEAD_DIM // 2
    inv_freq = ROPE_THETA ** (-jnp.arange(half, dtype=F32) * 2.0 / DIFF_HEAD_DIM)
    ang = positions.reshape(m, 1).astype(F32) * jnp.tile(inv_freq, LANES // half)[None, :]
    tm = min(1024, m)
    spec = pl.BlockSpec((tm, LANES), lambda i: (i, 0))
    return pl.pallas_call(
        _rope_kernel, grid=(m // tm,), in_specs=[spec], out_specs=[spec, spec],
        out_shape=[jax.ShapeDtypeStruct((m, LANES), F32)] * 2,
        compiler_params=_cparams(1), name="rope_tables",
    )(ang)


def _norm_mod_kernel(l_ref, x_ref, w_ref, sc_ref, sh_ref, o_ref):
    x = x_ref[...]
    ms = jnp.mean(x * x, axis=-1, keepdims=True)
    y = x * lax.rsqrt(ms + NORM_EPS) * w_ref[0]
    o_ref[...] = (y * (1.0 + sc_ref[0, 0, 0]) + sh_ref[0, 0, 0]).astype(BF16)


def _norm_mod(layer, x, norm_w, mod, scale_idx, shift_idx, seq):
    m, d = x.shape
    tm = min(512, seq)
    return _layer_call(
        _norm_mod_kernel, (m // tm,),
        [
            pl.BlockSpec((tm, d), lambda i, l: (i, 0)),
            pl.BlockSpec((1, 1, d), lambda i, l: (l[0], 0, 0)),
            pl.BlockSpec((1, 1, 1, 1, d), lambda i, l: (l[0], scale_idx, i * tm // seq, 0, 0)),
            pl.BlockSpec((1, 1, 1, 1, d), lambda i, l: (l[0], shift_idx, i * tm // seq, 0, 0)),
        ],
        pl.BlockSpec((tm, d), lambda i, l: (i, 0)),
        jax.ShapeDtypeStruct((m, d), BF16), name="norm_mod",
    )(layer, x, norm_w, mod, mod)


def _matmul(layer, xs, ws, w_x, w_off, extras, epilogue, out_dtypes, n, tm, tn, name):
    m = xs[0].shape[0]
    nx, nw, ne = len(xs), len(ws), len(extras)
    cast_w = ws[0].dtype == F32
    assert all((w.dtype == F32) == cast_w for w in ws)
    ij = (lambda a, b: (b, a)) if cast_w else (lambda a, b: (a, b))
    in_specs = [pl.BlockSpec((tm, x.shape[1]), lambda a, b, l: (ij(a, b)[0], 0)) for x in xs]
    for w, off in zip(ws, w_off):
        in_specs.append(pl.BlockSpec((1, w.shape[1], tn), lambda a, b, l, off=off: (l[0], 0, ij(a, b)[1] + off)))
    for _, bs, imap in extras:
        in_specs.append(pl.BlockSpec(bs, lambda a, b, l, imap=imap: imap(*ij(a, b), l)))
    out_specs = [pl.BlockSpec((tm, tn), lambda a, b, l: ij(a, b)) for _ in out_dtypes]
    out_shape = [jax.ShapeDtypeStruct((m, n), dt) for dt in out_dtypes]
    scratch = [pltpu.VMEM((w.shape[1], tn), BF16) for w in ws] if cast_w else []

    def kern(l_ref, *refs):
        x_refs, w_refs = refs[:nx], refs[nx:nx + nw]
        e_refs, o_refs = refs[nx + nw:nx + nw + ne], refs[nx + nw + ne:nx + nw + ne + len(out_dtypes)]
        if cast_w:
            wb_refs = refs[nx + nw + ne + len(out_dtypes):]

            @pl.when(pl.program_id(1) == 0)
            def _():
                for w_ref, wb_ref in zip(w_refs, wb_refs):
                    wb_ref[...] = w_ref[0].astype(BF16)

            w_tiles = [wb_ref[...] for wb_ref in wb_refs]
        else:
            w_tiles = [w_ref[0] for w_ref in w_refs]
        accs = [jnp.dot(x_refs[xi][...], w, preferred_element_type=F32) for xi, w in zip(w_x, w_tiles)]
        outs = epilogue(accs, [e[...] for e in e_refs])
        for o_ref, v in zip(o_refs, outs):
            o_ref[...] = v.astype(o_ref.dtype)

    grid = (n // tn, m // tm) if cast_w else (m // tm, n // tn)
    return _layer_call(kern, grid, in_specs, out_specs, out_shape, scratch_shapes=scratch, name=name)(
        layer, *xs, *ws, *[e[0] for e in extras])


def _residual_norm_kernel(l_ref, a_ref, w_ref, x_ref, gate_ref, nw_ref, sc_ref, sh_ref, xo_ref, ho_ref):
    acc = jnp.dot(a_ref[...], w_ref[0], preferred_element_type=F32)
    xn = x_ref[...] + gate_ref[0, 0, 0] * acc
    xo_ref[...] = xn
    ms = jnp.mean(xn * xn, axis=-1, keepdims=True)
    y = xn * lax.rsqrt(ms + NORM_EPS) * nw_ref[0]
    ho_ref[...] = (y * (1.0 + sc_ref[0, 0, 0]) + sh_ref[0, 0, 0]).astype(BF16)


def _matmul_residual_norm(layer, a, w, x, mod, gate_idx, norm_w, scale_idx, shift_idx, next_layer, seq, tm, name):
    m, k = a.shape
    depth, _, n = w.shape
    nl = (lambda l: jnp.minimum(l[0] + 1, depth - 1)) if next_layer else (lambda l: l[0])
    row = lambda i: i * tm // seq
    return _layer_call(
        _residual_norm_kernel, (m // tm,),
        [
            pl.BlockSpec((tm, k), lambda i, l: (i, 0)),
            pl.BlockSpec((1, k, n), lambda i, l: (l[0], 0, 0), pipeline_mode=pl.Buffered(1)),
            pl.BlockSpec((tm, n), lambda i, l: (i, 0)),
            pl.BlockSpec((1, 1, 1, 1, n), lambda i, l: (l[0], gate_idx, row(i), 0, 0)),
            pl.BlockSpec((1, 1, n), lambda i, l: (nl(l), 0, 0)),
            pl.BlockSpec((1, 1, 1, 1, n), lambda i, l: (nl(l), scale_idx, row(i), 0, 0)),
            pl.BlockSpec((1, 1, 1, 1, n), lambda i, l: (nl(l), shift_idx, row(i), 0, 0)),
        ],
        [pl.BlockSpec((tm, n), lambda i, l: (i, 0))] * 2,
        [jax.ShapeDtypeStruct((m, n), F32), jax.ShapeDtypeStruct((m, n), BF16)], name=name,
    )(layer, a, w, x, mod, norm_w, mod, mod)


def _mixer_out_kernel(l_ref, yd_ref, yg_ref, gd_ref, gg_ref, wd_ref, wg_ref, wo_ref, x_ref, gate_ref, nw_ref,
                      sc_ref, sh_ref, xo_ref, ho_ref):
    merged = (gd_ref[...] * jnp.dot(yd_ref[...], wd_ref[0], preferred_element_type=F32)
              + gg_ref[...] * jnp.dot(yg_ref[...], wg_ref[0], preferred_element_type=F32)).astype(BF16)
    acc = jnp.dot(merged, wo_ref[0], preferred_element_type=F32)
    xn = x_ref[...] + gate_ref[0, 0, 0] * acc
    xo_ref[...] = xn
    ms = jnp.mean(xn * xn, axis=-1, keepdims=True)
    y = xn * lax.rsqrt(ms + NORM_EPS) * nw_ref[0]
    ho_ref[...] = (y * (1.0 + sc_ref[0, 0, 0]) + sh_ref[0, 0, 0]).astype(BF16)


def _mixer_out(layer, y_diff, y_gdn, gates, w_d, w_g, w_o, x, mod, norm_w, seq):
    m, d = x.shape
    tm = min(256, seq)
    kd, kg = y_diff.shape[1], y_gdn.shape[1]
    row = lambda i: i * tm // seq
    resident = lambda k, n: pl.BlockSpec((1, k, n), lambda i, l: (l[0], 0, 0), pipeline_mode=pl.Buffered(1))
    modsp = lambda idx: pl.BlockSpec((1, 1, 1, 1, d), lambda i, l: (l[0], idx, row(i), 0, 0))
    return _layer_call(
        _mixer_out_kernel, (m // tm,),
        [
            pl.BlockSpec((tm, kd), lambda i, l: (i, 0)),
            pl.BlockSpec((tm, kg), lambda i, l: (i, 0)),
            pl.BlockSpec((tm, d), lambda i, l: (i, 0)),
            pl.BlockSpec((tm, d), lambda i, l: (i, 1)),
            resident(kd, d), resident(kg, d), resident(d, d),
            pl.BlockSpec((tm, d), lambda i, l: (i, 0)),
            modsp(2),
            pl.BlockSpec((1, 1, d), lambda i, l: (l[0], 0, 0)),
            modsp(4), modsp(3),
        ],
        [pl.BlockSpec((tm, d), lambda i, l: (i, 0))] * 2,
        [jax.ShapeDtypeStruct((m, d), F32), jax.ShapeDtypeStruct((m, d), BF16)], name="mixer_out",
    )(layer, y_diff, y_gdn, gates, gates, w_d, w_g, w_o, x, mod, norm_w, mod, mod)


def _group_sumsq(x, group_ones):
    sq = x * x
    hi = sq.astype(BF16)
    lo = (sq - hi.astype(F32)).astype(BF16)
    return (jnp.dot(hi, group_ones, preferred_element_type=F32)
            + jnp.dot(lo, group_ones, preferred_element_type=F32))


def _qk_prep_kernel(l_ref, q_ref, k_ref, v_ref, cos_ref, sin_ref, qw_ref, kw_ref, qo_ref, ko_ref, vo_ref):
    cos = cos_ref[...]
    sin = sin_ref[...]
    row = lax.broadcasted_iota(jnp.int32, (LANES, LANES), 0) // DIFF_HEAD_DIM
    col = lax.broadcasted_iota(jnp.int32, (LANES, LANES), 1) // DIFF_HEAD_DIM
    ones = (row == col).astype(BF16)
    lane = lax.broadcasted_iota(jnp.int32, cos.shape, 1)
    first = (lane % DIFF_HEAD_DIM) < (DIFF_HEAD_DIM // 2)
    half = DIFF_HEAD_DIM // 2

    def norm_rope(x, w, scale):
        ms = _group_sumsq(x, ones) * (1.0 / DIFF_HEAD_DIM)
        y = x * lax.rsqrt(ms + NORM_EPS) * w
        partner = jnp.where(first, pltpu.roll(y, LANES - half, 1), pltpu.roll(y, half, 1))
        return (y * cos + partner * sin) * scale

    q_scale = DIFF_HEAD_DIM ** -0.5 * math.log2(math.e)
    ones_blk = jnp.ones((cos.shape[0], LANES), BF16)
    for h in range(N_DIFF_HEADS):
        sl = slice(h * LANES, (h + 1) * LANES)
        qo_ref[:, sl] = norm_rope(q_ref[:, sl], qw_ref[0, :, sl], q_scale).astype(BF16)
        ko_ref[:, sl] = norm_rope(k_ref[:, sl], kw_ref[0, :, sl], 1.0).astype(BF16)
        vo_ref[:, 2 * h * LANES:(2 * h + 1) * LANES] = v_ref[:, sl].astype(BF16)
        vo_ref[:, (2 * h + 1) * LANES:(2 * h + 2) * LANES] = ones_blk


def _qk_prep(layer, proj, cos, sin, qn_w, kn_w):
    m = proj.shape[0]
    tm = min(512, m)
    w = DIFF_WIDTH
    rep = w // DIFF_HEAD_DIM
    depth = qn_w.shape[0]
    qw = jnp.tile(qn_w, (1, rep)).reshape(depth, 1, w)
    kw = jnp.tile(kn_w, (1, rep)).reshape(depth, 1, w)
    blk = lambda c: pl.BlockSpec((tm, w), lambda i, l, c=c: (i, c))
    tab = pl.BlockSpec((tm, LANES), lambda i, l: (i, 0))
    wsp = pl.BlockSpec((1, 1, w), lambda i, l: (l[0], 0, 0))
    osp = pl.BlockSpec((tm, w), lambda i, l: (i, 0))
    vsp = pl.BlockSpec((tm, 2 * w), lambda i, l: (i, 0))
    return _layer_call(
        _qk_prep_kernel, (m // tm,), [blk(0), blk(1), blk(2), tab, tab, wsp, wsp], [osp, osp, vsp],
        [jax.ShapeDtypeStruct((m, w), BF16)] * 2 + [jax.ShapeDtypeStruct((m, 2 * w), BF16)], name="diff_qk_prep",
    )(layer, proj, proj, proj, cos, sin, qw, kw)


def _diff_attn_kernel(l_ref, sc_ref, q_ref, k_ref, v_ref, w_ref, o_ref):
    lam = sc_ref[0]
    out_scale = sc_ref[1]
    k = k_ref[...]
    nt = (((1,), (1,)), ((), ()))
    sub = min(ATTN_SUB_ROWS, q_ref.shape[0])

    def branch(qm):
        s = lax.dot_general(qm, k, nt, preferred_element_type=F32)
        p = jnp.exp2(s - jnp.max(s, axis=-1, keepdims=True)).astype(BF16)
        ov = jnp.dot(p, v_ref[...], preferred_element_type=F32)
        return ov[:, :LANES] / ov[:, LANES:]

    for r in range(q_ref.shape[0] // sub):
        rows = slice(r * sub, (r + 1) * sub)
        q = q_ref[rows, :]
        lane = lax.broadcasted_iota(jnp.int32, q.shape, 1)
        zero = jnp.zeros_like(q)
        q0 = jnp.where(lane < DIFF_HEAD_DIM, q, zero)
        q1 = jnp.where(lane >= DIFF_HEAD_DIM, q, zero)
        o = branch(q0) - lam * branch(q1)
        ms = jnp.mean(o * o, axis=-1, keepdims=True)
        o_ref[rows, :] = (o * lax.rsqrt(ms + NORM_EPS) * w_ref[0] * out_scale).astype(BF16)


def _diff_attention(layer, scalars, q, k, v, subln_w, batch, seq):
    m = q.shape[0]
    tq = min(ATTN_Q_ROWS, seq)
    nq = seq // tq
    depth = subln_w.shape[0]
    return _layer_call(
        _diff_attn_kernel, (batch, N_DIFF_HEADS, nq),
        [
            pl.BlockSpec(memory_space=pltpu.SMEM),
            pl.BlockSpec((tq, LANES), lambda b, h, i, l: (b * nq + i, h)),
            pl.BlockSpec((seq, LANES), lambda b, h, i, l: (b, h)),
            pl.BlockSpec((seq, 2 * LANES), lambda b, h, i, l: (b, h)),
            pl.BlockSpec((1, 1, LANES), lambda b, h, i, l: (l[0], 0, 0)),
        ],
        pl.BlockSpec((tq, LANES), lambda b, h, i, l: (b * nq + i, h)),
        jax.ShapeDtypeStruct((m, DIFF_WIDTH), BF16), name="diff_attention",
    )(layer, scalars, q, k, v, subln_w.reshape(depth, 1, LANES))


def _split3(x):
    hi = x.astype(BF16)
    r1 = x - hi.astype(F32)
    mid = r1.astype(BF16)
    lo = (r1 - mid.astype(F32)).astype(BF16)
    return hi, mid, lo


def _gdn_prep_kernel(l_ref, main_ref, prev_ref, next_ref, small_ref, cw_ref, alog_ref, dtb_ref,
                     q_ref, k_ref, kt_ref, v_ref, beta_ref, gc_ref, xe_ref):
    t = pl.program_id(1)
    nt = pl.num_programs(1)
    ts = main_ref.shape[0]
    pad = CONV_WIDTH // 2
    xe_ref[0:SUBLANES, :] = jnp.where(t > 0, prev_ref[...], 0.0)
    xe_ref[SUBLANES:SUBLANES + ts, :] = main_ref[...]
    xe_ref[SUBLANES + ts:, :] = jnp.where(t < nt - 1, next_ref[...], 0.0)
    cw = cw_ref[0]
    first = SUBLANES - pad
    acc = cw[0:1, :] * xe_ref[first:first + ts, :]
    for j in range(1, CONV_WIDTH):
        acc = acc + cw[j:j + 1, :] * xe_ref[first + j:first + j + ts, :]
    y = _silu(acc)
    for h in range(N_GDN_HEADS):
        sl = slice(h * LANES, (h + 1) * LANES)
        qh = y[:, sl]
        kh = y[:, GDN_WIDTH + h * LANES:GDN_WIDTH + (h + 1) * LANES]
        q_ref[:, sl] = (qh * (lax.rsqrt(jnp.sum(qh * qh, axis=-1, keepdims=True) + NORM_EPS)
                              * GDN_HEAD_DIM ** -0.5)).astype(BF16)
        kn = kh * lax.rsqrt(jnp.sum(kh * kh, axis=-1, keepdims=True) + NORM_EPS)
        k_ref[:, sl] = kn.astype(BF16)
        kt_ref[sl, :] = kn.T.astype(BF16)
    v_ref[...] = y[:, 2 * GDN_WIDTH:].astype(BF16)

    nh2 = 2 * N_GDN_HEADS
    small = small_ref[...]
    beta_ref[...] = jax.nn.sigmoid(small[:, :nh2])
    g = -jnp.exp(alog_ref[0]) * jax.nn.softplus(small[:, nh2:2 * nh2] + dtb_ref[0])
    r = lax.broadcasted_iota(jnp.int32, (ts, ts), 0)
    c = lax.broadcasted_iota(jnp.int32, (ts, ts), 1)
    same = (r // CHUNK) == (c // CHUNK)
    pre = (same & (c <= r)).astype(BF16)
    suf = (same & (c >= r)).astype(BF16)
    parts = _split3(g)
    cs_f = sum(jnp.dot(pre, p, preferred_element_type=F32) for p in parts)
    cs_b = sum(jnp.dot(suf, p, preferred_element_type=F32) for p in parts)
    lane = lax.broadcasted_iota(jnp.int32, g.shape, 1)
    gc_ref[...] = jnp.where(lane < N_GDN_HEADS, cs_f, cs_b)


def _gdn_prep(layer, proj, small, conv_w, a_log, dt_bias, batch, seq):
    m = proj.shape[0]
    ts = min(256, seq)
    nts = seq // ts
    w3 = 3 * GDN_WIDTH
    cb = (3 * DIFF_WIDTH) // w3
    assert cb * w3 == 3 * DIFF_WIDTH
    depth = conv_w.shape[0]
    nh2 = 2 * N_GDN_HEADS
    hb = ts // SUBLANES
    last = m // SUBLANES - 1
    osp = pl.BlockSpec((ts, GDN_WIDTH), lambda b, t, l: (b * nts + t, 0))
    ssp = pl.BlockSpec((ts, nh2), lambda b, t, l: (b * nts + t, 0))
    return _layer_call(
        _gdn_prep_kernel, (batch, nts),
        [
            pl.BlockSpec((ts, w3), lambda b, t, l: (b * nts + t, cb)),
            pl.BlockSpec((SUBLANES, w3), lambda b, t, l: (jnp.maximum((b * nts + t) * hb - 1, 0), cb)),
            pl.BlockSpec((SUBLANES, w3), lambda b, t, l: (jnp.minimum((b * nts + t + 1) * hb, last), cb)),
            pl.BlockSpec((ts, LANES), lambda b, t, l: (b * nts + t, 0)),
            pl.BlockSpec((1, CONV_WIDTH, w3), lambda b, t, l: (l[0], 0, 0)),
            pl.BlockSpec((1, 1, nh2), lambda b, t, l: (l[0], 0, 0)),
            pl.BlockSpec((1, 1, nh2), lambda b, t, l: (l[0], 0, 0)),
        ],
        [osp, osp, pl.BlockSpec((GDN_WIDTH, ts), lambda b, t, l: (0, b * nts + t)), osp, ssp, ssp],
        [jax.ShapeDtypeStruct((m, GDN_WIDTH), BF16)] * 2 + [jax.ShapeDtypeStruct((GDN_WIDTH, m), BF16)]
        + [jax.ShapeDtypeStruct((m, GDN_WIDTH), BF16)] + [jax.ShapeDtypeStruct((m, nh2), F32)] * 2,
        scratch_shapes=[pltpu.VMEM((ts + 2 * SUBLANES, w3), F32)], name="gdn_prep",
    )(layer, proj, proj, proj, small, conv_w, a_log.reshape(depth, 1, nh2), dt_bias.reshape(depth, 1, nh2))


def _chunk_masks(backward):
    r = lax.broadcasted_iota(jnp.int32, (CHUNK, CHUNK), 0)
    c = lax.broadcasted_iota(jnp.int32, (CHUNK, CHUNK), 1)
    if backward:
        return r <= c, r < c
    return r >= c, r > c


def _gdn_l_kernel(l_ref, k_ref, beta_ref, gc_ref, gcrow_ref, lf_ref, lb_ref):
    h = pl.program_id(1)
    n_chunks = k_ref.shape[0] // CHUNK
    nt = (((1,), (1,)), ((), ()))
    lane16 = lax.broadcasted_iota(jnp.int32, (CHUNK, 2 * N_GDN_HEADS), 1)
    sub16 = lax.broadcasted_iota(jnp.int32, (2 * N_GDN_HEADS, CHUNK), 0)

    def body(n, carry):
        rows = pl.ds(pl.multiple_of(n * CHUNK, CHUNK), CHUNK)
        kb = k_ref[rows, :].astype(BF16)
        kk = lax.dot_general(kb, kb, nt, preferred_element_type=F32)
        beta = beta_ref[rows, :]
        gc = gc_ref[rows, :]
        gcr = gcrow_ref[0, n]
        for d, out_ref in ((0, lf_ref), (1, lb_ref)):
            col = d * N_GDN_HEADS + h
            bcol = jnp.sum(jnp.where(lane16 == col, beta, 0.0), axis=1, keepdims=True)
            gcol = jnp.sum(jnp.where(lane16 == col, gc, 0.0), axis=1, keepdims=True)
            grow = jnp.sum(jnp.where(sub16 == col, gcr, 0.0), axis=0, keepdims=True)
            _, strict = _chunk_masks(d == 1)
            dec = jnp.exp(jnp.where(strict, gcol - grow, NEG_BIG))
            out_ref[0, rows, :] = bcol * kk * dec
        return carry

    lax.fori_loop(0, n_chunks, body, 0, unroll=4)


def _gdn_build_l(layer, k, beta, gc, gc_rows, batch, seq):
    nh2 = 2 * N_GDN_HEADS
    n_chunks = seq // CHUNK
    osp = pl.BlockSpec((1, seq, CHUNK), lambda b, h, l: (b * N_GDN_HEADS + h, 0, 0))
    return _layer_call(
        _gdn_l_kernel, (batch, N_GDN_HEADS),
        [
            pl.BlockSpec((seq, LANES), lambda b, h, l: (b, h)),
            pl.BlockSpec((seq, nh2), lambda b, h, l: (b, 0)),
            pl.BlockSpec((seq, nh2), lambda b, h, l: (b, 0)),
            pl.BlockSpec((1, n_chunks, nh2, CHUNK), lambda b, h, l: (b, 0, 0, 0)),
        ],
        [osp, osp],
        [jax.ShapeDtypeStruct((batch * N_GDN_HEADS, seq, CHUNK), F32)] * 2, name="gdn_build_l",
    )(layer, k, beta, gc, gc_rows)


def _tri_inverse_kernel(l_ref, t_ref):
    sub = lax.broadcasted_iota(jnp.int32, (SUBLANES, LANES), 0)
    nblk = CHUNK // SUBLANES
    for i in range(CHUNK):
        live = i // SUBLANES + 1
        accs = [jnp.zeros((SUBLANES, LANES), F32) for _ in range(live)]
        for j in range(i):
            lij = l_ref[i, pl.ds(j, 1), :]
            for cb in range(j // SUBLANES + 1):
                accs[cb] = accs[cb] + lij * t_ref[j, cb * SUBLANES:(cb + 1) * SUBLANES, :]
        for cb in range(nblk):
            if cb < live - 1:
                val = -accs[cb]
            elif cb == live - 1:
                val = jnp.where(sub == i % SUBLANES, 1.0, 0.0) - accs[cb]
            else:
                val = jnp.zeros((SUBLANES, LANES), F32)
            t_ref[i, cb * SUBLANES:(cb + 1) * SUBLANES, :] = val


def _tri_inverse(l_all):
    g = l_all.shape[-1]
    spec = pl.BlockSpec((CHUNK, CHUNK, LANES), lambda i: (0, 0, i))
    return pl.pallas_call(
        _tri_inverse_kernel, grid=(g // LANES,), in_specs=[spec], out_specs=spec,
        out_shape=jax.ShapeDtypeStruct(l_all.shape, F32),
        compiler_params=_cparams(1), name="gdn_tri_inverse",
    )(l_all)


def _gdn_pre_kernel(l_ref, q_ref, k_ref, kt_ref, v_ref, beta_ref, gc_ref, gcrow_ref, gcpair_ref, tf_ref, tb_ref,
                    mqf_ref, mqb_ref, rf_ref, rb_ref, glf_ref, glb_ref, ol_ref):
    h = pl.program_id(1)
    n_local = q_ref.shape[0] // CHUNK
    dk = GDN_HEAD_DIM
    nt = (((1,), (1,)), ((), ()))
    lane16 = lax.broadcasted_iota(jnp.int32, (CHUNK, 2 * N_GDN_HEADS), 1)
    sub16 = lax.broadcasted_iota(jnp.int32, (2 * N_GDN_HEADS, CHUNK), 0)
    sub16p = lax.broadcasted_iota(jnp.int32, (2 * N_GDN_HEADS, LANES), 0)
    lane_half = lax.broadcasted_iota(jnp.int32, (1, LANES), 1) // CHUNK
    outs = ((tf_ref, mqf_ref, rf_ref, glf_ref), (tb_ref, mqb_ref, rb_ref, glb_ref))
    per_pair = LANES // CHUNK
    stage1 = []
    for c in range(n_local):
        rows = slice(c * CHUNK, (c + 1) * CHUNK)
        pair, half = c // per_pair, c % per_pair
        qc = q_ref[rows, :]
        kc = k_ref[rows, :]
        vc = v_ref[rows, :]
        kt_pair = kt_ref[:, pair * LANES:(pair + 1) * LANES]
        qk = lax.dot_general(qc.astype(BF16), kc.astype(BF16), nt, preferred_element_type=F32)
        for d in (0, 1):
            col = d * N_GDN_HEADS + h
            bcol = jnp.sum(jnp.where(lane16 == col, beta_ref[rows, :], 0.0), axis=1, keepdims=True)
            gcol = jnp.sum(jnp.where(lane16 == col, gc_ref[rows, :], 0.0), axis=1, keepdims=True)
            grow = jnp.sum(jnp.where(sub16 == col, gcrow_ref[0, c], 0.0), axis=0, keepdims=True)
            glast = gcol[0:1, :] if d == 1 else gcol[CHUNK - 1:CHUNK, :]
            incl, _ = _chunk_masks(d == 1)
            attn = (qk * jnp.exp(jnp.where(incl, gcol - grow, NEG_BIG))).astype(BF16)
            gam = jnp.exp(gcol)
            x = jnp.concatenate([bcol * vc, (bcol * gam) * kc], axis=1).astype(BF16)
            t = outs[d][0][0, rows, :]
            th = t.astype(BF16)
            tl = (t - th.astype(F32)).astype(BF16)
            uw = (jnp.dot(th, x, preferred_element_type=F32)
                  + jnp.dot(tl, x, preferred_element_type=F32)).astype(BF16)
            grow_pair = jnp.sum(jnp.where(sub16p == col, gcpair_ref[0, pair], 0.0), axis=0, keepdims=True)
            tail = jnp.exp(jnp.where(lane_half == half, glast - grow_pair, NEG_BIG))
            ktil_t = (kt_pair * tail).astype(BF16)
            stage1.append((c, d, attn, uw, ktil_t, gam * qc, jnp.exp(glast)))
    o_local = {}
    for c, d, attn, uw, ktil_t, gq, gl in stage1:
        _, mq_ref, r_ref, gl_ref = outs[d]
        pos = (n_local - 1 - c) if d == 1 else c
        awu = jnp.dot(attn, uw, preferred_element_type=F32)
        kwu = jnp.dot(ktil_t, jnp.concatenate([uw] * per_pair, axis=0), preferred_element_type=F32)
        mq_ref[0, pos, 0:dk, :] = kwu[:, dk:].astype(BF16)
        mq_ref[0, pos, dk:dk + CHUNK, :] = (gq - awu[:, dk:]).astype(BF16)
        r_ref[0, pos] = kwu[:, :dk].astype(BF16)
        gl_ref[0, pos] = jnp.broadcast_to(gl, (1, LANES))
        o_local[c] = o_local[c] + awu[:, :dk] if c in o_local else awu[:, :dk]
    for c, val in o_local.items():
        ol_ref[c * CHUNK:(c + 1) * CHUNK, :] = val


def _gdn_pre(layer, q, k, k_t, v, beta, gc, gc_rows, gc_pairs, t_f, t_b, batch, seq):
    nh2 = 2 * N_GDN_HEADS
    n_chunks = seq // CHUNK
    cg = min(GDN_PRE_CHUNKS, n_chunks)
    ng = n_chunks // cg
    rows = cg * CHUNK
    dk = GDN_HEAD_DIM
    bh = batch * N_GDN_HEADS
    hsp = pl.BlockSpec((rows, LANES), lambda b, h, g, l: (b * ng + g, h))
    ssp = pl.BlockSpec((rows, nh2), lambda b, h, g, l: (b * ng + g, 0))
    tsp = pl.BlockSpec((1, rows, CHUNK), lambda b, h, g, l: (b * N_GDN_HEADS + h, g, 0))

    def osp(r, mirrored):
        if mirrored:
            return pl.BlockSpec((1, cg, r, LANES), lambda b, h, g, l: (b * N_GDN_HEADS + h, ng - 1 - g, 0, 0))
        return pl.BlockSpec((1, cg, r, LANES), lambda b, h, g, l: (b * N_GDN_HEADS + h, g, 0, 0))

    shapes = [((dk + CHUNK), BF16), (dk, BF16), (1, F32)]
    out_specs, out_shape = [], []
    for r, dt in shapes:
        for mirrored in (False, True):
            out_specs.append(osp(r, mirrored))
            out_shape.append(jax.ShapeDtypeStruct((bh, n_chunks, r, LANES), dt))
    return _layer_call(
        _gdn_pre_kernel, (batch, N_GDN_HEADS, ng),
        [hsp, hsp, pl.BlockSpec((LANES, rows), lambda b, h, g, l: (h, b * ng + g)), hsp, ssp, ssp,
         pl.BlockSpec((1, cg, nh2, CHUNK), lambda b, h, g, l: (b, g, 0, 0)),
         pl.BlockSpec((1, rows // LANES, nh2, LANES), lambda b, h, g, l: (b, g, 0, 0)),
         tsp, tsp],
        out_specs + [hsp], out_shape + [jax.ShapeDtypeStruct(q.shape, F32)], name="gdn_chunk_pre",
    )(layer, q, k, k_t, v, beta, gc, gc_rows, gc_pairs, t_f, t_b)


def _gdn_state_kernel(l_ref, mqf_ref, mqb_ref, rf_ref, rb_ref, glf_ref, glb_ref, of_ref, ob_ref, st_ref):
    g = pl.program_id(1)
    n_heads, n_local = mqf_ref.shape[0], mqf_ref.shape[1]
    dk = GDN_HEAD_DIM

    @pl.when(g == 0)
    def _():
        st_ref[...] = jnp.zeros_like(st_ref)

    zero = jnp.zeros((dk, dk), BF16)
    states = [(st_ref[2 * hh], st_ref[2 * hh + 1]) for hh in range(n_heads)]
    for c in range(n_local):
        for hh in range(n_heads):
            sf, sb = states[hh]
            mq = jnp.concatenate([mqf_ref[hh, c], mqb_ref[hh, c]], axis=1)
            bd = jnp.concatenate([jnp.concatenate([sf.astype(BF16), zero], axis=1),
                                  jnp.concatenate([zero, sb.astype(BF16)], axis=1)], axis=0)
            res = jnp.dot(mq, bd, preferred_element_type=F32)
            hl = slice(hh * LANES, (hh + 1) * LANES)
            of_ref[c * CHUNK:(c + 1) * CHUNK, hl] = res[dk:, :dk]
            cb = n_local - 1 - c
            ob_ref[cb * CHUNK:(cb + 1) * CHUNK, hl] = res[dk:, dk:]
            states[hh] = (glf_ref[hh, c] * sf - res[:dk, :dk] + rf_ref[hh, c].astype(F32),
                          glb_ref[hh, c] * sb - res[:dk, dk:] + rb_ref[hh, c].astype(F32))
    for hh in range(n_heads):
        st_ref[2 * hh] = states[hh][0]
        st_ref[2 * hh + 1] = states[hh][1]


def _gdn_state(layer, pre, batch, seq):
    n_chunks = seq // CHUNK
    cg = min(GDN_SCAN_CHUNKS, n_chunks)
    ng = n_chunks // cg
    nh = N_GDN_HEADS
    dk = GDN_HEAD_DIM
    m = batch * seq
    in_specs = [pl.BlockSpec((nh, cg) + a.shape[2:], lambda b, g, l: (b, g, 0, 0)) for a in pre]
    return _layer_call(
        _gdn_state_kernel, (batch, ng), in_specs,
        [pl.BlockSpec((cg * CHUNK, GDN_WIDTH), lambda b, g, l: (b * ng + g, 0)),
         pl.BlockSpec((cg * CHUNK, GDN_WIDTH), lambda b, g, l: (b * ng + ng - 1 - g, 0))],
        [jax.ShapeDtypeStruct((m, GDN_WIDTH), F32)] * 2,
        scratch_shapes=[pltpu.VMEM((2 * nh, dk, dk), F32)], name="gdn_state_scan",
    )(layer, *pre)


def _gdn_out_kernel(l_ref, of_ref, ob_ref, ol_ref, z_ref, nw_ref, y_ref):
    for h in range(N_GDN_HEADS):
        hl = slice(h * LANES, (h + 1) * LANES)
        o = of_ref[:, hl] + ob_ref[:, hl] + ol_ref[:, hl]
        ms = jnp.mean(o * o, axis=-1, keepdims=True)
        y_ref[:, hl] = (o * lax.rsqrt(ms + NORM_EPS) * nw_ref[0] * _silu(z_ref[:, hl])).astype(BF16)


def _gdn_out(layer, o_f, o_b, o_local, proj, norm_w):
    m = o_f.shape[0]
    tm = min(512, m)
    depth = norm_w.shape[0]
    zcb = (3 * DIFF_WIDTH + 3 * GDN_WIDTH) // GDN_WIDTH
    osp = pl.BlockSpec((tm, GDN_WIDTH), lambda i, l: (i, 0))
    return _layer_call(
        _gdn_out_kernel, (m // tm,),
        [osp, osp, osp, pl.BlockSpec((tm, GDN_WIDTH), lambda i, l: (i, zcb)),
         pl.BlockSpec((1, 1, LANES), lambda i, l: (l[0], 0, 0))],
        osp, jax.ShapeDtypeStruct((m, GDN_WIDTH), BF16), name="gdn_out",
    )(layer, o_f, o_b, o_local, proj, norm_w.reshape(depth, 1, LANES))


def _gated_deltanet(layer, proj, small, conv_w, a_log, dt_bias, norm_w, batch, seq):
    q, k, k_t, v, beta, gc = _gdn_prep(layer, proj, small, conv_w, a_log, dt_bias, batch, seq)
    n_chunks = seq // CHUNK
    nh2 = 2 * N_GDN_HEADS
    gc_rows = gc.reshape(batch, n_chunks, CHUNK, nh2).transpose(0, 1, 3, 2)
    gc_pairs = gc.reshape(batch, seq // LANES, LANES, nh2).transpose(0, 1, 3, 2)
    l_f, l_b = _gdn_build_l(layer, k, beta, gc, gc_rows, batch, seq)
    g0 = batch * N_GDN_HEADS * n_chunks
    t_f = _tri_inverse(l_f.reshape(g0, CHUNK, CHUNK).transpose(1, 2, 0))
    t_b = _tri_inverse(l_b.reshape(g0, CHUNK, CHUNK).transpose(2, 1, 0))
    t_f = t_f.transpose(2, 0, 1).reshape(batch * N_GDN_HEADS, seq, CHUNK)
    t_b = t_b.transpose(2, 1, 0).reshape(batch * N_GDN_HEADS, seq, CHUNK)
    *pre, o_local = _gdn_pre(layer, q, k, k_t, v, beta, gc, gc_rows, gc_pairs, t_f, t_b, batch, seq)
    o_f, o_b = _gdn_state(layer, pre, batch, seq)
    return _gdn_out(layer, o_f, o_b, o_local, proj, norm_w)


def _layer(l, x, h, cos, sin, mod, lam_inits, p, batch, seq):
    m, d = x.shape
    layer = jnp.reshape(l, (1,)).astype(jnp.int32)
    tm = min(1024, seq)
    ident = lambda accs, extras: accs

    (proj,) = _matmul(layer, [h], [p["w_main"]], [0], [0], [], ident, [F32], MAIN_COLS, tm, 1024, "proj_main")
    (small,) = _matmul(layer, [h], [p["w_small"]], [0], [0], [], ident, [F32], LANES, tm, LANES, "proj_small")
    (gates,) = _matmul(layer, [h], [p["w_gates"]], [0], [0], [],
                       lambda accs, extras: [jax.nn.sigmoid(accs[0])], [BF16], 2 * d, tm, 1024, "proj_gates")

    lam_init = lam_inits[l]
    lv = p["diff_lambda"][l].astype(F32)
    lam = jnp.exp(jnp.sum(lv[0] * lv[1])) - jnp.exp(jnp.sum(lv[2] * lv[3])) + lam_init
    scalars = jnp.stack([lam, 1.0 - lam_init]).astype(F32)
    dq, dk, dv = _qk_prep(layer, proj, cos, sin, p["diff_qn_w"], p["diff_kn_w"])
    y_diff = _diff_attention(layer, scalars, dq, dk, dv, p["diff_subln_w"], batch, seq)

    y_gdn = _gated_deltanet(layer, proj, small, p["gdn_conv_w"], p["gdn_a_log"], p["gdn_dt_bias"],
                            p["gdn_norm_w"], batch, seq)

    x, h = _mixer_out(layer, y_diff, y_gdn, gates, p["w_branch_diff"], p["w_branch_gdn"], p["w_out"], x, mod,
                      p["norm_ffn_w"], seq)
    f = p["ffn_w_down"].shape[1]
    tf = 512
    (act,) = _matmul(layer, [h], [p["ffn_w_up"], p["ffn_w_up"]], [0, 0], [0, f // tf], [],
                     lambda accs, extras: [_silu(accs[0]) * accs[1]], [BF16], f, tm, tf, "ffn_up")
    return tuple(_matmul_residual_norm(layer, act, p["ffn_w_down"], x, mod, 5, p["norm_mix_w"], 1, 0, True, seq,
                                       min(256, seq), "ffn_down"))


def kernel(x, c, positions, ada_w, ada_b, norm_mix_w, norm_ffn_w, w_in, diff_qn_w, diff_kn_w, diff_lambda,
           diff_subln_w, gdn_conv_w, gdn_a_log, gdn_dt_bias, gdn_norm_w, w_branch_diff, w_branch_gdn, w_out,
           ffn_w_up, ffn_w_down):
    batch, seq, d = x.shape
    depth = ada_w.shape[0]
    mod = _ada_modulation(c, ada_w, ada_b)
    cos, sin = _rope_tables(positions)
    lam_inits = jnp.asarray([0.8 - 0.6 * math.exp(-0.3 * i) for i in range(depth)], F32)
    small_w = jnp.pad(w_in[:, :, MAIN_COLS:MAIN_COLS + SMALL_COLS], ((0, 0), (0, 0), (0, LANES - SMALL_COLS)))
    p = {
        "norm_mix_w": norm_mix_w.reshape(depth, 1, d), "norm_ffn_w": norm_ffn_w.reshape(depth, 1, d),
        "w_main": w_in,
        "w_small": small_w.astype(BF16),
        "w_gates": w_in[:, :, MAIN_COLS + SMALL_COLS:].astype(BF16),
        "diff_qn_w": diff_qn_w, "diff_kn_w": diff_kn_w, "diff_lambda": diff_lambda, "diff_subln_w": diff_subln_w,
        "gdn_conv_w": gdn_conv_w, "gdn_a_log": gdn_a_log, "gdn_dt_bias": gdn_dt_bias, "gdn_norm_w": gdn_norm_w,
        "w_branch_diff": w_branch_diff.astype(BF16), "w_branch_gdn": w_branch_gdn.astype(BF16),
        "w_out": w_out.astype(BF16), "ffn_w_up": ffn_w_up, "ffn_w_down": ffn_w_down.astype(BF16),
    }
    x0 = x.reshape(batch * seq, d)
    h0 = _norm_mod(jnp.zeros((1,), jnp.int32), x0, p["norm_mix_w"], mod, 1, 0, seq)
    body = lambda l, carry: _layer(l, carry[0], carry[1], cos, sin, mod, lam_inits, p, batch, seq)
    out, _ = lax.fori_loop(0, depth, body, (x0, h0))
    return out.reshape(batch, seq, d)
```

```python
import functools
import math

import jax
import jax.numpy as jnp
from jax import lax
from jax.experimental import pallas as pl
from jax.experimental.pallas import tpu as pltpu

F32 = jnp.float32
BF16 = jnp.bfloat16

N_DIFF_HEADS = 8
DIFF_HEAD_DIM = 64
DIFF_WIDTH = N_DIFF_HEADS * 2 * DIFF_HEAD_DIM
N_GDN_HEADS = 8
GDN_HEAD_DIM = 128
GDN_WIDTH = N_GDN_HEADS * GDN_HEAD_DIM
CONV_WIDTH = 5
CHUNK = 64
ROPE_THETA = 10000.0
NORM_EPS = 1e-6
N_MOD = 6
LANES = 128
SUBLANES = 8
NEG_BIG = -1e30
GDN_PRE_CHUNKS = 16
GDN_SCAN_CHUNKS = 8
ATTN_Q_ROWS = 2048
ATTN_SUB_ROWS = 256

MAIN_COLS = 3 * DIFF_WIDTH + 4 * GDN_WIDTH
SMALL_COLS = 4 * N_GDN_HEADS
VMEM_LIMIT = 48 * 1024 * 1024


def _silu(x):
    return x * jax.nn.sigmoid(x)


def _cparams(n_axes, vmem=VMEM_LIMIT):
    return pltpu.CompilerParams(dimension_semantics=("arbitrary",) * n_axes, vmem_limit_bytes=vmem)


def _layer_call(kernel, grid, in_specs, out_specs, out_shape, scratch_shapes=(), name=None):
    return pl.pallas_call(
        kernel,
        grid_spec=pltpu.PrefetchScalarGridSpec(
            num_scalar_prefetch=1, grid=grid, in_specs=in_specs, out_specs=out_specs,
            scratch_shapes=scratch_shapes),
        out_shape=out_shape,
        compiler_params=_cparams(len(grid)),
        name=name,
    )


def _ada_kernel(c_ref, w_ref, b_ref, o_ref):
    @pl.when(pl.program_id(1) == 0)
    def _():
        o_ref[0] = jnp.broadcast_to(b_ref[0], o_ref.shape[1:])

    a = _silu(c_ref[...]).astype(BF16)
    o_ref[0] += jnp.dot(a, w_ref[0].astype(BF16), preferred_element_type=F32)


def _ada_modulation(c, ada_w, ada_b):
    depth, d, n6 = ada_w.shape
    b = c.shape[0]
    rows = -(-b // SUBLANES) * SUBLANES
    c_pad = jnp.pad(c, ((0, rows - b), (0, 0)))
    tk = 128
    out = pl.pallas_call(
        _ada_kernel,
        grid=(depth, d // tk),
        in_specs=[
            pl.BlockSpec((rows, tk), lambda l, k: (0, k)),
            pl.BlockSpec((1, tk, n6), lambda l, k: (l, k, 0)),
            pl.BlockSpec((1, 1, n6), lambda l, k: (l, 0, 0)),
        ],
        out_specs=pl.BlockSpec((1, rows, n6), lambda l, k: (l, 0, 0)),
        out_shape=jax.ShapeDtypeStruct((depth, rows, n6), F32),
        compiler_params=_cparams(2),
        name="ada_modulation",
    )(c_pad, ada_w, ada_b.reshape(depth, 1, n6))
    return out[:, :b].reshape(depth, b, N_MOD, d).transpose(0, 2, 1, 3).reshape(depth, N_MOD, b, 1, d)


def _rope_kernel(ang_ref, cos_ref, sin_ref):
    ang = ang_ref[...]
    lane = lax.broadcasted_iota(jnp.int32, ang.shape, 1)
    first = (lane % DIFF_HEAD_DIM) < (DIFF_HEAD_DIM // 2)
    cos_ref[...] = jnp.cos(ang)
    s = jnp.sin(ang)
    sin_ref[...] = jnp.where(first, -s, s)


def _rope_tables(positions):
    m = positions.size
    half = DIFF_HEAD_DIM // 2
    inv_freq = ROPE_THETA ** (-jnp.arange(half, dtype=F32) * 2.0 / DIFF_HEAD_DIM)
    ang = positions.reshape(m, 1).astype(F32) * jnp.tile(inv_freq, LANES // half)[None, :]
    tm = min(1024, m)
    spec = pl.BlockSpec((tm, LANES), lambda i: (i, 0))
    return pl.pallas_call(
        _rope_kernel, grid=(m // tm,), in_specs=[spec], out_specs=[spec, spec],
        out_shape=[jax.ShapeDtypeStruct((m, LANES), F32)] * 2,
        compiler_params=_cparams(1), name="rope_tables",
    )(ang)


def _norm_mod_kernel(l_ref, x_ref, w_ref, sc_ref, sh_ref, o_ref):
    x = x_ref[...]
    ms = jnp.mean(x * x, axis=-1, keepdims=True)
    y = x * lax.rsqrt(ms + NORM_EPS) * w_ref[0]
    o_ref[...] = (y * (1.0 + sc_ref[0, 0, 0]) + sh_ref[0, 0, 0]).astype(BF16)


def _norm_mod(layer, x, norm_w, mod, scale_idx, shift_idx, seq):
    m, d = x.shape
    tm = min(512, seq)
    return _layer_call(
        _norm_mod_kernel, (m // tm,),
        [
            pl.BlockSpec((tm, d), lambda i, l: (i, 0)),
            pl.BlockSpec((1, 1, d), lambda i, l: (l[0], 0, 0)),
            pl.BlockSpec((1, 1, 1, 1, d), lambda i, l: (l[0], scale_idx, i * tm // seq, 0, 0)),
            pl.BlockSpec((1, 1, 1, 1, d), lambda i, l: (l[0], shift_idx, i * tm // seq, 0, 0)),
        ],
        pl.BlockSpec((tm, d), lambda i, l: (i, 0)),
        jax.ShapeDtypeStruct((m, d), BF16), name="norm_mod",
    )(layer, x, norm_w, mod, mod)


def _matmul(layer, xs, ws, w_x, w_off, extras, epilogue, out_dtypes, n, tm, tn, name):
    m = xs[0].shape[0]
    nx, nw, ne = len(xs), len(ws), len(extras)
    cast_w = ws[0].dtype == F32
    assert all((w.dtype == F32) == cast_w for w in ws)
    ij = (lambda a, b: (b, a)) if cast_w else (lambda a, b: (a, b))
    in_specs = [pl.BlockSpec((tm, x.shape[1]), lambda a, b, l: (ij(a, b)[0], 0)) for x in xs]
    for w, off in zip(ws, w_off):
        in_specs.append(pl.BlockSpec((1, w.shape[1], tn), lambda a, b, l, off=off: (l[0], 0, ij(a, b)[1] + off)))
    for _, bs, imap in extras:
        in_specs.append(pl.BlockSpec(bs, lambda a, b, l, imap=imap: imap(*ij(a, b), l)))
    out_specs = [pl.BlockSpec((tm, tn), lambda a, b, l: ij(a, b)) for _ in out_dtypes]
    out_shape = [jax.ShapeDtypeStruct((m, n), dt) for dt in out_dtypes]
    scratch = [pltpu.VMEM((w.shape[1], tn), BF16) for w in ws] if cast_w else []

    def kern(l_ref, *refs):
        x_refs, w_refs = refs[:nx], refs[nx:nx + nw]
        e_refs, o_refs = refs[nx + nw:nx + nw + ne], refs[nx + nw + ne:nx + nw + ne + len(out_dtypes)]
        if cast_w:
            wb_refs = refs[nx + nw + ne + len(out_dtypes):]

            @pl.when(pl.program_id(1) == 0)
            def _():
                for w_ref, wb_ref in zip(w_refs, wb_refs):
                    wb_ref[...] = w_ref[0].astype(BF16)

            w_tiles = [wb_ref[...] for wb_ref in wb_refs]
        else:
            w_tiles = [w_ref[0] for w_ref in w_refs]
        accs = [jnp.dot(x_refs[xi][...], w, preferred_element_type=F32) for xi, w in zip(w_x, w_tiles)]
        outs = epilogue(accs, [e[...] for e in e_refs])
        for o_ref, v in zip(o_refs, outs):
            o_ref[...] = v.astype(o_ref.dtype)

    grid = (n // tn, m // tm) if cast_w else (m // tm, n // tn)
    return _layer_call(kern, grid, in_specs, out_specs, out_shape, scratch_shapes=scratch, name=name)(
        layer, *xs, *ws, *[e[0] for e in extras])


def _residual_norm_kernel(l_ref, a_ref, w_ref, x_ref, gate_ref, nw_ref, sc_ref, sh_ref, xo_ref, ho_ref):
    acc = jnp.dot(a_ref[...], w_ref[0], preferred_element_type=F32)
    xn = x_ref[...] + gate_ref[0, 0, 0] * acc
    xo_ref[...] = xn
    ms = jnp.mean(xn * xn, axis=-1, keepdims=True)
    y = xn * lax.rsqrt(ms + NORM_EPS) * nw_ref[0]
    ho_ref[...] = (y * (1.0 + sc_ref[0, 0, 0]) + sh_ref[0, 0, 0]).astype(BF16)


def _matmul_residual_norm(layer, a, w, x, mod, gate_idx, norm_w, scale_idx, shift_idx, next_layer, seq, tm, name):
    m, k = a.shape
    depth, _, n = w.shape
    nl = (lambda l: jnp.minimum(l[0] + 1, depth - 1)) if next_layer else (lambda l: l[0])
    row = lambda i: i * tm // seq
    return _layer_call(
        _residual_norm_kernel, (m // tm,),
        [
            pl.BlockSpec((tm, k), lambda i, l: (i, 0)),
            pl.BlockSpec((1, k, n), lambda i, l: (l[0], 0, 0), pipeline_mode=pl.Buffered(1)),
            pl.BlockSpec((tm, n), lambda i, l: (i, 0)),
            pl.BlockSpec((1, 1, 1, 1, n), lambda i, l: (l[0], gate_idx, row(i), 0, 0)),
            pl.BlockSpec((1, 1, n), lambda i, l: (nl(l), 0, 0)),
            pl.BlockSpec((1, 1, 1, 1, n), lambda i, l: (nl(l), scale_idx, row(i), 0, 0)),
            pl.BlockSpec((1, 1, 1, 1, n), lambda i, l: (nl(l), shift_idx, row(i), 0, 0)),
        ],
        [pl.BlockSpec((tm, n), lambda i, l: (i, 0))] * 2,
        [jax.ShapeDtypeStruct((m, n), F32), jax.ShapeDtypeStruct((m, n), BF16)], name=name,
    )(layer, a, w, x, mod, norm_w, mod, mod)


def _mixer_out_kernel(l_ref, yd_ref, yg_ref, gd_ref, gg_ref, wd_ref, wg_ref, wo_ref, x_ref, gate_ref, nw_ref,
                      sc_ref, sh_ref, xo_ref, ho_ref):
    merged = (gd_ref[...] * jnp.dot(yd_ref[...], wd_ref[0], preferred_element_type=F32)
              + gg_ref[...] * jnp.dot(yg_ref[...], wg_ref[0], preferred_element_type=F32)).astype(BF16)
    acc = jnp.dot(merged, wo_ref[0], preferred_element_type=F32)
    xn = x_ref[...] + gate_ref[0, 0, 0] * acc
    xo_ref[...] = xn
    ms = jnp.mean(xn * xn, axis=-1, keepdims=True)
    y = xn * lax.rsqrt(ms + NORM_EPS) * nw_ref[0]
    ho_ref[...] = (y * (1.0 + sc_ref[0, 0, 0]) + sh_ref[0, 0, 0]).astype(BF16)


def _mixer_out(layer, y_diff, y_gdn, gates, w_d, w_g, w_o, x, mod, norm_w, seq):
    m, d = x.shape
    tm = min(256, seq)
    kd, kg = y_diff.shape[1], y_gdn.shape[1]
    row = lambda i: i * tm // seq
    resident = lambda k, n: pl.BlockSpec((1, k, n), lambda i, l: (l[0], 0, 0), pipeline_mode=pl.Buffered(1))
    modsp = lambda idx: pl.BlockSpec((1, 1, 1, 1, d), lambda i, l: (l[0], idx, row(i), 0, 0))
    return _layer_call(
        _mixer_out_kernel, (m // tm,),
        [
            pl.BlockSpec((tm, kd), lambda i, l: (i, 0)),
            pl.BlockSpec((tm, kg), lambda i, l: (i, 0)),
            pl.BlockSpec((tm, d), lambda i, l: (i, 0)),
            pl.BlockSpec((tm, d), lambda i, l: (i, 1)),
            resident(kd, d), resident(kg, d), resident(d, d),
            pl.BlockSpec((tm, d), lambda i, l: (i, 0)),
            modsp(2),
            pl.BlockSpec((1, 1, d), lambda i, l: (l[0], 0, 0)),
            modsp(4), modsp(3),
        ],
        [pl.BlockSpec((tm, d), lambda i, l: (i, 0))] * 2,
        [jax.ShapeDtypeStruct((m, d), F32), jax.ShapeDtypeStruct((m, d), BF16)], name="mixer_out",
    )(layer, y_diff, y_gdn, gates, gates, w_d, w_g, w_o, x, mod, norm_w, mod, mod)


def _group_sumsq(x, group_ones):
    sq = x * x
    hi = sq.astype(BF16)
    lo = (sq - hi.astype(F32)).astype(BF16)
    return (jnp.dot(hi, group_ones, preferred_element_type=F32)
            + jnp.dot(lo, group_ones, preferred_element_type=F32))


def _qk_prep_kernel(l_ref, q_ref, k_ref, v_ref, cos_ref, sin_ref, qw_ref, kw_ref, qo_ref, ko_ref, vo_ref):
    cos = cos_ref[...]
    sin = sin_ref[...]
    row = lax.broadcasted_iota(jnp.int32, (LANES, LANES), 0) // DIFF_HEAD_DIM
    col = lax.broadcasted_iota(jnp.int32, (LANES, LANES), 1) // DIFF_HEAD_DIM
    ones = (row == col).astype(BF16)
    lane = lax.broadcasted_iota(jnp.int32, cos.shape, 1)
    first = (lane % DIFF_HEAD_DIM) < (DIFF_HEAD_DIM // 2)
    half = DIFF_HEAD_DIM // 2

    def norm_rope(x, w, scale):
        ms = _group_sumsq(x, ones) * (1.0 / DIFF_HEAD_DIM)
        y = x * lax.rsqrt(ms + NORM_EPS) * w
        partner = jnp.where(first, pltpu.roll(y, LANES - half, 1), pltpu.roll(y, half, 1))
        return (y * cos + partner * sin) * scale

    q_scale = DIFF_HEAD_DIM ** -0.5 * math.log2(math.e)
    ones_blk = jnp.ones((cos.shape[0], LANES), BF16)
    for h in range(N_DIFF_HEADS):
        sl = slice(h * LANES, (h + 1) * LANES)
        qo_ref[:, sl] = norm_rope(q_ref[:, sl], qw_ref[0, :, sl], q_scale).astype(BF16)
        ko_ref[:, sl] = norm_rope(k_ref[:, sl], kw_ref[0, :, sl], 1.0).astype(BF16)
        vo_ref[:, 2 * h * LANES:(2 * h + 1) * LANES] = v_ref[:, sl].astype(BF16)
        vo_ref[:, (2 * h + 1) * LANES:(2 * h + 2) * LANES] = ones_blk


def _qk_prep(layer, proj, cos, sin, qn_w, kn_w):
    m = proj.shape[0]
    tm = min(512, m)
    w = DIFF_WIDTH
    rep = w // DIFF_HEAD_DIM
    depth = qn_w.shape[0]
    qw = jnp.tile(qn_w, (1, rep)).reshape(depth, 1, w)
    kw = jnp.tile(kn_w, (1, rep)).reshape(depth, 1, w)
    blk = lambda c: pl.BlockSpec((tm, w), lambda i, l, c=c: (i, c))
    tab = pl.BlockSpec((tm, LANES), lambda i, l: (i, 0))
    wsp = pl.BlockSpec((1, 1, w), lambda i, l: (l[0], 0, 0))
    osp = pl.BlockSpec((tm, w), lambda i, l: (i, 0))
    vsp = pl.BlockSpec((tm, 2 * w), lambda i, l: (i, 0))
    return _layer_call(
        _qk_prep_kernel, (m // tm,), [blk(0), blk(1), blk(2), tab, tab, wsp, wsp], [osp, osp, vsp],
        [jax.ShapeDtypeStruct((m, w), BF16)] * 2 + [jax.ShapeDtypeStruct((m, 2 * w), BF16)], name="diff_qk_prep",
    )(layer, proj, proj, proj, cos, sin, qw, kw)


def _diff_attn_kernel(l_ref, sc_ref, q_ref, k_ref, v_ref, w_ref, o_ref):
    lam = sc_ref[0]
    out_scale = sc_ref[1]
    k = k_ref[...]
    nt = (((1,), (1,)), ((), ()))
    sub = min(ATTN_SUB_ROWS, q_ref.shape[0])

    def branch(qm):
        s = lax.dot_general(qm, k, nt, preferred_element_type=F32)
        p = jnp.exp2(s - jnp.max(s, axis=-1, keepdims=True)).astype(BF16)
        ov = jnp.dot(p, v_ref[...], preferred_element_type=F32)
        return ov[:, :LANES] / ov[:, LANES:]

    for r in range(q_ref.shape[0] // sub):
        rows = slice(r * sub, (r + 1) * sub)
        q = q_ref[rows, :]
        lane = lax.broadcasted_iota(jnp.int32, q.shape, 1)
        zero = jnp.zeros_like(q)
        q0 = jnp.where(lane < DIFF_HEAD_DIM, q, zero)
        q1 = jnp.where(lane >= DIFF_HEAD_DIM, q, zero)
        o = branch(q0) - lam * branch(q1)
        ms = jnp.mean(o * o, axis=-1, keepdims=True)
        o_ref[rows, :] = (o * lax.rsqrt(ms + NORM_EPS) * w_ref[0] * out_scale).astype(BF16)


def _diff_attention(layer, scalars, q, k, v, subln_w, batch, seq):
    m = q.shape[0]
    tq = min(ATTN_Q_ROWS, seq)
    nq = seq // tq
    depth = subln_w.shape[0]
    return _layer_call(
        _diff_attn_kernel, (batch, N_DIFF_HEADS, nq),
        [
            pl.BlockSpec(memory_space=pltpu.SMEM),
            pl.BlockSpec((tq, LANES), lambda b, h, i, l: (b * nq + i, h)),
            pl.BlockSpec((seq, LANES), lambda b, h, i, l: (b, h)),
            pl.BlockSpec((seq, 2 * LANES), lambda b, h, i, l: (b, h)),
            pl.BlockSpec((1, 1, LANES), lambda b, h, i, l: (l[0], 0, 0)),
        ],
        pl.BlockSpec((tq, LANES), lambda b, h, i, l: (b * nq + i, h)),
        jax.ShapeDtypeStruct((m, DIFF_WIDTH), BF16), name="diff_attention",
    )(layer, scalars, q, k, v, subln_w.reshape(depth, 1, LANES))


def _split3(x):
    hi = x.astype(BF16)
    r1 = x - hi.astype(F32)
    mid = r1.astype(BF16)
    lo = (r1 - mid.astype(F32)).astype(BF16)
    return hi, mid, lo


def _gdn_prep_kernel(l_ref, main_ref, prev_ref, next_ref, small_ref, cw_ref, alog_ref, dtb_ref,
                     q_ref, k_ref, kt_ref, v_ref, beta_ref, gc_ref, xe_ref):
    t = pl.program_id(1)
    nt = pl.num_programs(1)
    ts = main_ref.shape[0]
    pad = CONV_WIDTH // 2
    xe_ref[0:SUBLANES, :] = jnp.where(t > 0, prev_ref[...], 0.0)
    xe_ref[SUBLANES:SUBLANES + ts, :] = main_ref[...]
    xe_ref[SUBLANES + ts:, :] = jnp.where(t < nt - 1, next_ref[...], 0.0)
    cw = cw_ref[0]
    first = SUBLANES - pad
    acc = cw[0:1, :] * xe_ref[first:first + ts, :]
    for j in range(1, CONV_WIDTH):
        acc = acc + cw[j:j + 1, :] * xe_ref[first + j:first + j + ts, :]
    y = _silu(acc)
    for h in range(N_GDN_HEADS):
        sl = slice(h * LANES, (h + 1) * LANES)
        qh = y[:, sl]
        kh = y[:, GDN_WIDTH + h * LANES:GDN_WIDTH + (h + 1) * LANES]
        q_ref[:, sl] = (qh * (lax.rsqrt(jnp.sum(qh * qh, axis=-1, keepdims=True) + NORM_EPS)
                              * GDN_HEAD_DIM ** -0.5)).astype(BF16)
        kn = kh * lax.rsqrt(jnp.sum(kh * kh, axis=-1, keepdims=True) + NORM_EPS)
        k_ref[:, sl] = kn.astype(BF16)
        kt_ref[sl, :] = kn.T.astype(BF16)
    v_ref[...] = y[:, 2 * GDN_WIDTH:].astype(BF16)

    nh2 = 2 * N_GDN_HEADS
    small = small_ref[...]
    beta_ref[...] = jax.nn.sigmoid(small[:, :nh2])
    g = -jnp.exp(alog_ref[0]) * jax.nn.softplus(small[:, nh2:2 * nh2] + dtb_ref[0])
    r = lax.broadcasted_iota(jnp.int32, (ts, ts), 0)
    c = lax.broadcasted_iota(jnp.int32, (ts, ts), 1)
    same = (r // CHUNK) == (c // CHUNK)
    pre = (same & (c <= r)).astype(BF16)
    suf = (same & (c >= r)).astype(BF16)
    parts = _split3(g)
    cs_f = sum(jnp.dot(pre, p, preferred_element_type=F32) for p in parts)
    cs_b = sum(jnp.dot(suf, p, preferred_element_type=F32) for p in parts)
    lane = lax.broadcasted_iota(jnp.int32, g.shape, 1)
    gc_ref[...] = jnp.where(lane < N_GDN_HEADS, cs_f, cs_b)


def _gdn_prep(layer, proj, small, conv_w, a_log, dt_bias, batch, seq):
    m = proj.shape[0]
    ts = min(256, seq)
    nts = seq // ts
    w3 = 3 * GDN_WIDTH
    cb = (3 * DIFF_WIDTH) // w3
    assert cb * w3 == 3 * DIFF_WIDTH
    depth = conv_w.shape[0]
    nh2 = 2 * N_GDN_HEADS
    hb = ts // SUBLANES
    last = m // SUBLANES - 1
    osp = pl.BlockSpec((ts, GDN_WIDTH), lambda b, t, l: (b * nts + t, 0))
    ssp = pl.BlockSpec((ts, nh2), lambda b, t, l: (b * nts + t, 0))
    return _layer_call(
        _gdn_prep_kernel, (batch, nts),
        [
            pl.BlockSpec((ts, w3), lambda b, t, l: (b * nts + t, cb)),
            pl.BlockSpec((SUBLANES, w3), lambda b, t, l: (jnp.maximum((b * nts + t) * hb - 1, 0), cb)),
            pl.BlockSpec((SUBLANES, w3), lambda b, t, l: (jnp.minimum((b * nts + t + 1) * hb, last), cb)),
            pl.BlockSpec((ts, LANES), lambda b, t, l: (b * nts + t, 0)),
            pl.BlockSpec((1, CONV_WIDTH, w3), lambda b, t, l: (l[0], 0, 0)),
            pl.BlockSpec((1, 1, nh2), lambda b, t, l: (l[0], 0, 0)),
            pl.BlockSpec((1, 1, nh2), lambda b, t, l: (l[0], 0, 0)),
        ],
        [osp, osp, pl.BlockSpec((GDN_WIDTH, ts), lambda b, t, l: (0, b * nts + t)), osp, ssp, ssp],
        [jax.ShapeDtypeStruct((m, GDN_WIDTH), BF16)] * 2 + [jax.ShapeDtypeStruct((GDN_WIDTH, m), BF16)]
        + [jax.ShapeDtypeStruct((m, GDN_WIDTH), BF16)] + [jax.ShapeDtypeStruct((m, nh2), F32)] * 2,
        scratch_shapes=[pltpu.VMEM((ts + 2 * SUBLANES, w3), F32)], name="gdn_prep",
    )(layer, proj, proj, proj, small, conv_w, a_log.reshape(depth, 1, nh2), dt_bias.reshape(depth, 1, nh2))


def _chunk_masks(backward):
    r = lax.broadcasted_iota(jnp.int32, (CHUNK, CHUNK), 0)
    c = lax.broadcasted_iota(jnp.int32, (CHUNK, CHUNK), 1)
    if backward:
        return r <= c, r < c
    return r >= c, r > c


def _gdn_l_kernel(l_ref, k_ref, beta_ref, gc_ref, gcrow_ref, lf_ref, lb_ref):
    h = pl.program_id(1)
    n_chunks = k_ref.shape[0] // CHUNK
    nt = (((1,), (1,)), ((), ()))
    lane16 = lax.broadcasted_iota(jnp.int32, (CHUNK, 2 * N_GDN_HEADS), 1)
    sub16 = lax.broadcasted_iota(jnp.int32, (2 * N_GDN_HEADS, CHUNK), 0)

    def body(n, carry):
        rows = pl.ds(pl.multiple_of(n * CHUNK, CHUNK), CHUNK)
        kb = k_ref[rows, :].astype(BF16)
        kk = lax.dot_general(kb, kb, nt, preferred_element_type=F32)
        beta = beta_ref[rows, :]
        gc = gc_ref[rows, :]
        gcr = gcrow_ref[0, n]
        for d, out_ref in ((0, lf_ref), (1, lb_ref)):
            col = d * N_GDN_HEADS + h
            bcol = jnp.sum(jnp.where(lane16 == col, beta, 0.0), axis=1, keepdims=True)
            gcol = jnp.sum(jnp.where(lane16 == col, gc, 0.0), axis=1, keepdims=True)
            grow = jnp.sum(jnp.where(sub16 == col, gcr, 0.0), axis=0, keepdims=True)
            _, strict = _chunk_masks(d == 1)
            dec = jnp.exp(jnp.where(strict, gcol - grow, NEG_BIG))
            out_ref[0, rows, :] = bcol * kk * dec
        return carry

    lax.fori_loop(0, n_chunks, body, 0, unroll=4)


def _gdn_build_l(layer, k, beta, gc, gc_rows, batch, seq):
    nh2 = 2 * N_GDN_HEADS
    n_chunks = seq // CHUNK
    osp = pl.BlockSpec((1, seq, CHUNK), lambda b, h, l: (b * N_GDN_HEADS + h, 0, 0))
    return _layer_call(
        _gdn_l_kernel, (batch, N_GDN_HEADS),
        [
            pl.BlockSpec((seq, LANES), lambda b, h, l: (b, h)),
            pl.BlockSpec((seq, nh2), lambda b, h, l: (b, 0)),
            pl.BlockSpec((seq, nh2), lambda b, h, l: (b, 0)),
            pl.BlockSpec((1, n_chunks, nh2, CHUNK), lambda b, h, l: (b, 0, 0, 0)),
        ],
        [osp, osp],
        [jax.ShapeDtypeStruct((batch * N_GDN_HEADS, seq, CHUNK), F32)] * 2, name="gdn_build_l",
    )(layer, k, beta, gc, gc_rows)


def _tri_inverse_kernel(l_ref, t_ref):
    sub = lax.broadcasted_iota(jnp.int32, (SUBLANES, LANES), 0)
    nblk = CHUNK // SUBLANES
    for i in range(CHUNK):
        live = i // SUBLANES + 1
        accs = [jnp.zeros((SUBLANES, LANES), F32) for _ in range(live)]
        for j in range(i):
            lij = l_ref[i, pl.ds(j, 1), :]
            for cb in range(j // SUBLANES + 1):
                accs[cb] = accs[cb] + lij * t_ref[j, cb * SUBLANES:(cb + 1) * SUBLANES, :]
        for cb in range(nblk):
            if cb < live - 1:
                val = -accs[cb]
            elif cb == live - 1:
                val = jnp.where(sub == i % SUBLANES, 1.0, 0.0) - accs[cb]
            else:
                val = jnp.zeros((SUBLANES, LANES), F32)
            t_ref[i, cb * SUBLANES:(cb + 1) * SUBLANES, :] = val


def _tri_inverse(l_all):
    g = l_all.shape[-1]
    spec = pl.BlockSpec((CHUNK, CHUNK, LANES), lambda i: (0, 0, i))
    return pl.pallas_call(
        _tri_inverse_kernel, grid=(g // LANES,), in_specs=[spec], out_specs=spec,
        out_shape=jax.ShapeDtypeStruct(l_all.shape, F32),
        compiler_params=_cparams(1), name="gdn_tri_inverse",
    )(l_all)


def _gdn_pre_kernel(l_ref, q_ref, k_ref, kt_ref, v_ref, beta_ref, gc_ref, gcrow_ref, gcpair_ref, tf_ref, tb_ref,
                    mqf_ref, mqb_ref, rf_ref, rb_ref, glf_ref, glb_ref, ol_ref):
    h = pl.program_id(1)
    n_local = q_ref.shape[0] // CHUNK
    dk = GDN_HEAD_DIM
    nt = (((1,), (1,)), ((), ()))
    lane16 = lax.broadcasted_iota(jnp.int32, (CHUNK, 2 * N_GDN_HEADS), 1)
    sub16 = lax.broadcasted_iota(jnp.int32, (2 * N_GDN_HEADS, CHUNK), 0)
    sub16p = lax.broadcasted_iota(jnp.int32, (2 * N_GDN_HEADS, LANES), 0)
    lane_half = lax.broadcasted_iota(jnp.int32, (1, LANES), 1) // CHUNK
    outs = ((tf_ref, mqf_ref, rf_ref, glf_ref), (tb_ref, mqb_ref, rb_ref, glb_ref))
    per_pair = LANES // CHUNK
    stage1 = []
    for c in range(n_local):
        rows = slice(c * CHUNK, (c + 1) * CHUNK)
        pair, half = c // per_pair, c % per_pair
        qc = q_ref[rows, :]
        kc = k_ref[rows, :]
        vc = v_ref[rows, :]
        kt_pair = kt_ref[:, pair * LANES:(pair + 1) * LANES]
        qk = lax.dot_general(qc.astype(BF16), kc.astype(BF16), nt, preferred_element_type=F32)
        for d in (0, 1):
            col = d * N_GDN_HEADS + h
            bcol = jnp.sum(jnp.where(lane16 == col, beta_ref[rows, :], 0.0), axis=1, keepdims=True)
            gcol = jnp.sum(jnp.where(lane16 == col, gc_ref[rows, :], 0.0), axis=1, keepdims=True)
            grow = jnp.sum(jnp.where(sub16 == col, gcrow_ref[0, c], 0.0), axis=0, keepdims=True)
            glast = gcol[0:1, :] if d == 1 else gcol[CHUNK - 1:CHUNK, :]
            incl, _ = _chunk_masks(d == 1)
            attn = (qk * jnp.exp(jnp.where(incl, gcol - grow, NEG_BIG))).astype(BF16)
            gam = jnp.exp(gcol)
            x = jnp.concatenate([bcol * vc, (bcol * gam) * kc], axis=1).astype(BF16)
            t = outs[d][0][0, rows, :]
            th = t.astype(BF16)
            tl = (t - th.astype(F32)).astype(BF16)
            uw = (jnp.dot(th, x, preferred_element_type=F32)
                  + jnp.dot(tl, x, preferred_element_type=F32)).astype(BF16)
            grow_pair = jnp.sum(jnp.where(sub16p == col, gcpair_ref[0, pair], 0.0), axis=0, keepdims=True)
            tail = jnp.exp(jnp.where(lane_half == half, glast - grow_pair, NEG_BIG))
            ktil_t = (kt_pair * tail).astype(BF16)
            stage1.append((c, d, attn, uw, ktil_t, gam * qc, jnp.exp(glast)))
    o_local = {}
    for c, d, attn, uw, ktil_t, gq, gl in stage1:
        _, mq_ref, r_ref, gl_ref = outs[d]
        pos = (n_local - 1 - c) if d == 1 else c
        awu = jnp.dot(attn, uw, preferred_element_type=F32)
        kwu = jnp.dot(ktil_t, jnp.concatenate([uw] * per_pair, axis=0), preferred_element_type=F32)
        mq_ref[0, pos, 0:dk, :] = kwu[:, dk:].astype(BF16)
        mq_ref[0, pos, dk:dk + CHUNK, :] = (gq - awu[:, dk:]).astype(BF16)
        r_ref[0, pos] = kwu[:, :dk].astype(BF16)
        gl_ref[0, pos] = jnp.broadcast_to(gl, (1, LANES))
        o_local[c] = o_local[c] + awu[:, :dk] if c in o_local else awu[:, :dk]
    for c, val in o_local.items():
        ol_ref[c * CHUNK:(c + 1) * CHUNK, :] = val


def _gdn_pre(layer, q, k, k_t, v, beta, gc, gc_rows, gc_pairs, t_f, t_b, batch, seq):
    nh2 = 2 * N_GDN_HEADS
    n_chunks = seq // CHUNK
    cg = min(GDN_PRE_CHUNKS, n_chunks)
    ng = n_chunks // cg
    rows = cg * CHUNK
    dk = GDN_HEAD_DIM
    bh = batch * N_GDN_HEADS
    hsp = pl.BlockSpec((rows, LANES), lambda b, h, g, l: (b * ng + g, h))
    ssp = pl.BlockSpec((rows, nh2), lambda b, h, g, l: (b * ng + g, 0))
    tsp = pl.BlockSpec((1, rows, CHUNK), lambda b, h, g, l: (b * N_GDN_HEADS + h, g, 0))

    def osp(r, mirrored):
        if mirrored:
            return pl.BlockSpec((1, cg, r, LANES), lambda b, h, g, l: (b * N_GDN_HEADS + h, ng - 1 - g, 0, 0))
        return pl.BlockSpec((1, cg, r, LANES), lambda b, h, g, l: (b * N_GDN_HEADS + h, g, 0, 0))

    shapes = [((dk + CHUNK), BF16), (dk, BF16), (1, F32)]
    out_specs, out_shape = [], []
    for r, dt in shapes:
        for mirrored in (False, True):
            out_specs.append(osp(r, mirrored))
            out_shape.append(jax.ShapeDtypeStruct((bh, n_chunks, r, LANES), dt))
    return _layer_call(
        _gdn_pre_kernel, (batch, N_GDN_HEADS, ng),
        [hsp, hsp, pl.BlockSpec((LANES, rows), lambda b, h, g, l: (h, b * ng + g)), hsp, ssp, ssp,
         pl.BlockSpec((1, cg, nh2, CHUNK), lambda b, h, g, l: (b, g, 0, 0)),
         pl.BlockSpec((1, rows // LANES, nh2, LANES), lambda b, h, g, l: (b, g, 0, 0)),
         tsp, tsp],
        out_specs + [hsp], out_shape + [jax.ShapeDtypeStruct(q.shape, F32)], name="gdn_chunk_pre",
    )(layer, q, k, k_t, v, beta, gc, gc_rows, gc_pairs, t_f, t_b)


def _gdn_state_kernel(l_ref, mqf_ref, mqb_ref, rf_ref, rb_ref, glf_ref, glb_ref, of_ref, ob_ref, st_ref):
    g = pl.program_id(1)
    n_heads, n_local = mqf_ref.shape[0], mqf_ref.shape[1]
    dk = GDN_HEAD_DIM

    @pl.when(g == 0)
    def _():
        st_ref[...] = jnp.zeros_like(st_ref)

    zero = jnp.zeros((dk, dk), BF16)
    states = [(st_ref[2 * hh], st_ref[2 * hh + 1]) for hh in range(n_heads)]
    for c in range(n_local):
        for hh in range(n_heads):
            sf, sb = states[hh]
            mq = jnp.concatenate([mqf_ref[hh, c], mqb_ref[hh, c]], axis=1)
            bd = jnp.concatenate([jnp.concatenate([sf.astype(BF16), zero], axis=1),
                                  jnp.concatenate([zero, sb.astype(BF16)], axis=1)], axis=0)
            res = jnp.dot(mq, bd, preferred_element_type=F32)
            hl = slice(hh * LANES, (hh + 1) * LANES)
            of_ref[c * CHUNK:(c + 1) * CHUNK, hl] = res[dk:, :dk]
            cb = n_local - 1 - c
            ob_ref[cb * CHUNK:(cb + 1) * CHUNK, hl] = res[dk:, dk:]
            states[hh] = (glf_ref[hh, c] * sf - res[:dk, :dk] + rf_ref[hh, c].astype(F32),
                          glb_ref[hh, c] * sb - res[:dk, dk:] + rb_ref[hh, c].astype(F32))
    for hh in range(n_heads):
        st_ref[2 * hh] = states[hh][0]
        st_ref[2 * hh + 1] = states[hh][1]


def _gdn_state(layer, pre, batch, seq):
    n_chunks = seq // CHUNK
    cg = min(GDN_SCAN_CHUNKS, n_chunks)
    ng = n_chunks // cg
    nh = N_GDN_HEADS
    dk = GDN_HEAD_DIM
    m = batch * seq
    in_specs = [pl.BlockSpec((nh, cg) + a.shape[2:], lambda b, g, l: (b, g, 0, 0)) for a in pre]
    return _layer_call(
        _gdn_state_kernel, (batch, ng), in_specs,
        [pl.BlockSpec((cg * CHUNK, GDN_WIDTH), lambda b, g, l: (b * ng + g, 0)),
         pl.BlockSpec((cg * CHUNK, GDN_WIDTH), lambda b, g, l: (b * ng + ng - 1 - g, 0))],
        [jax.ShapeDtypeStruct((m, GDN_WIDTH), F32)] * 2,
        scratch_shapes=[pltpu.VMEM((2 * nh, dk, dk), F32)], name="gdn_state_scan",
    )(layer, *pre)


def _gdn_out_kernel(l_ref, of_ref, ob_ref, ol_ref, z_ref, nw_ref, y_ref):
    for h in range(N_GDN_HEADS):
        hl = slice(h * LANES, (h + 1) * LANES)
        o = of_ref[:, hl] + ob_ref[:, hl] + ol_ref[:, hl]
        ms = jnp.mean(o * o, axis=-1, keepdims=True)
        y_ref[:, hl] = (o * lax.rsqrt(ms + NORM_EPS) * nw_ref[0] * _silu(z_ref[:, hl])).astype(BF16)


def _gdn_out(layer, o_f, o_b, o_local, proj, norm_w):
    m = o_f.shape[0]
    tm = min(512, m)
    depth = norm_w.shape[0]
    zcb = (3 * DIFF_WIDTH + 3 * GDN_WIDTH) // GDN_WIDTH
    osp = pl.BlockSpec((tm, GDN_WIDTH), lambda i, l: (i, 0))
    return _layer_call(
        _gdn_out_kernel, (m // tm,),
        [osp, osp, osp, pl.BlockSpec((tm, GDN_WIDTH), lambda i, l: (i, zcb)),
         pl.BlockSpec((1, 1, LANES), lambda i, l: (l[0], 0, 0))],
        osp, jax.ShapeDtypeStruct((m, GDN_WIDTH), BF16), name="gdn_out",
    )(layer, o_f, o_b, o_local, proj, norm_w.reshape(depth, 1, LANES))


def _gated_deltanet(layer, proj, small, conv_w, a_log, dt_bias, norm_w, batch, seq):
    q, k, k_t, v, beta, gc = _gdn_prep(layer, proj, small, conv_w, a_log, dt_bias, batch, seq)
    n_chunks = seq // CHUNK
    nh2 = 2 * N_GDN_HEADS
    gc_rows = gc.reshape(batch, n_chunks, CHUNK, nh2).transpose(0, 1, 3, 2)
    gc_pairs = gc.reshape(batch, seq // LANES, LANES, nh2).transpose(0, 1, 3, 2)
    l_f, l_b = _gdn_build_l(layer, k, beta, gc, gc_rows, batch, seq)
    g0 = batch * N_GDN_HEADS * n_chunks
    t_f = _tri_inverse(l_f.reshape(g0, CHUNK, CHUNK).transpose(1, 2, 0))
    t_b = _tri_inverse(l_b.reshape(g0, CHUNK, CHUNK).transpose(2, 1, 0))
    t_f = t_f.transpose(2, 0, 1).reshape(batch * N_GDN_HEADS, seq, CHUNK)
    t_b = t_b.transpose(2, 1, 0).reshape(batch * N_GDN_HEADS, seq, CHUNK)
    *pre, o_local = _gdn_pre(layer, q, k, k_t, v, beta, gc, gc_rows, gc_pairs, t_f, t_b, batch, seq)
    o_f, o_b = _gdn_state(layer, pre, batch, seq)
    return _gdn_out(layer, o_f, o_b, o_local, proj, norm_w)


def _layer(l, x, h, cos, sin, mod, lam_inits, p, batch, seq):
    m, d = x.shape
    layer = jnp.reshape(l, (1,)).astype(jnp.int32)
    tm = min(1024, seq)
    ident = lambda accs, extras: accs

    (proj,) = _matmul(layer, [h], [p["w_main"]], [0], [0], [], ident, [F32], MAIN_COLS, tm, 1024, "proj_main")
    (small,) = _matmul(layer, [h], [p["w_small"]], [0], [0], [], ident, [F32], LANES, tm, LANES, "proj_small")
    (gates,) = _matmul(layer, [h], [p["w_gates"]], [0], [0], [],
                       lambda accs, extras: [jax.nn.sigmoid(accs[0])], [BF16], 2 * d, tm, 1024, "proj_gates")

    lam_init = lam_inits[l]
    lv = p["diff_lambda"][l].astype(F32)
    lam = jnp.exp(jnp.sum(lv[0] * lv[1])) - jnp.exp(jnp.sum(lv[2] * lv[3])) + lam_init
    scalars = jnp.stack([lam, 1.0 - lam_init]).astype(F32)
    dq, dk, dv = _qk_prep(layer, proj, cos, sin, p["diff_qn_w"], p["diff_kn_w"])
    y_diff = _diff_attention(layer, scalars, dq, dk, dv, p["diff_subln_w"], batch, seq)

    y_gdn = _gated_deltanet(layer, proj, small, p["gdn_conv_w"], p["gdn_a_log"], p["gdn_dt_bias"],
                            p["gdn_norm_w"], batch, seq)

    x, h = _mixer_out(layer, y_diff, y_gdn, gates, p["w_branch_diff"], p["w_branch_gdn"], p["w_out"], x, mod,
                      p["norm_ffn_w"], seq)
    f = p["ffn_w_down"].shape[1]
    tf = 512
    (act,) = _matmul(layer, [h], [p["ffn_w_up"], p["ffn_w_up"]], [0, 0], [0, f // tf], [],
                     lambda accs, extras: [_silu(accs[0]) * accs[1]], [BF16], f, tm, tf, "ffn_up")
    return tuple(_matmul_residual_norm(layer, act, p["ffn_w_down"], x, mod, 5, p["norm_mix_w"], 1, 0, True, seq,
                                       min(256, seq), "ffn_down"))


def kernel(x, c, positions, ada_w, ada_b, norm_mix_w, norm_ffn_w, w_in, diff_qn_w, diff_kn_w, diff_lambda,
           diff_subln_w, gdn_conv_w, gdn_a_log, gdn_dt_bias, gdn_norm_w, w_branch_diff, w_branch_gdn, w_out,
           ffn_w_up, ffn_w_down):
    batch, seq, d = x.shape
    depth = ada_w.shape[0]
    mod = _ada_modulation(c, ada_w, ada_b)
    cos, sin = _rope_tables(positions)
    lam_inits = jnp.asarray([0.8 - 0.6 * math.exp(-0.3 * i) for i in range(depth)], F32)
    small_w = jnp.pad(w_in[:, :, MAIN_COLS:MAIN_COLS + SMALL_COLS], ((0, 0), (0, 0), (0, LANES - SMALL_COLS)))
    p = {
        "norm_mix_w": norm_mix_w.reshape(depth, 1, d), "norm_ffn_w": norm_ffn_w.reshape(depth, 1, d),
        "w_main": w_in,
        "w_small": small_w.astype(BF16),
        "w_gates": w_in[:, :, MAIN_COLS + SMALL_COLS:].astype(BF16),
        "diff_qn_w": diff_qn_w, "diff_kn_w": diff_kn_w, "diff_lambda": diff_lambda, "diff_subln_w": diff_subln_w,
        "gdn_conv_w": gdn_conv_w, "gdn_a_log": gdn_a_log, "gdn_dt_bias": gdn_dt_bias, "gdn_norm_w": gdn_norm_w,
        "w_branch_diff": w_branch_diff.astype(BF16), "w_branch_gdn": w_branch_gdn.astype(BF16),
        "w_out": w_out.astype(BF16), "ffn_w_up": ffn_w_up, "ffn_w_down": ffn_w_down.astype(BF16),
    }
    x0 = x.reshape(batch * seq, d)
    h0 = _norm_mod(jnp.zeros((1,), jnp.int32), x0, p["norm_mix_w"], mod, 1, 0, seq)
    carry = (x0, h0)
    for l in range(depth):
        carry = _layer(jnp.int32(l), carry[0], carry[1], cos, sin, mod, lam_inits, p, batch, seq)
    return carry[0].reshape(batch, seq, d)
```

```python
import functools
import math

import jax
import jax.numpy as jnp
from jax import lax
from jax.experimental import pallas as pl
from jax.experimental.pallas import tpu as pltpu

F32 = jnp.float32
BF16 = jnp.bfloat16

N_DIFF_HEADS = 8
DIFF_HEAD_DIM = 64
DIFF_WIDTH = N_DIFF_HEADS * 2 * DIFF_HEAD_DIM
N_GDN_HEADS = 8
GDN_HEAD_DIM = 128
GDN_WIDTH = N_GDN_HEADS * GDN_HEAD_DIM
CONV_WIDTH = 5
CHUNK = 64
ROPE_THETA = 10000.0
NORM_EPS = 1e-6
N_MOD = 6
LANES = 128
SUBLANES = 8
NEG_BIG = -1e30
GDN_PRE_CHUNKS = 16
GDN_SCAN_CHUNKS = 8
ATTN_Q_ROWS = 2048
ATTN_SUB_ROWS = 256

MAIN_COLS = 3 * DIFF_WIDTH + 4 * GDN_WIDTH
SMALL_COLS = 4 * N_GDN_HEADS
VMEM_LIMIT = 48 * 1024 * 1024


def _silu(x):
    return x * jax.nn.sigmoid(x)


def _cparams(n_axes, vmem=VMEM_LIMIT):
    return pltpu.CompilerParams(dimension_semantics=("arbitrary",) * n_axes, vmem_limit_bytes=vmem)


def _layer_call(kernel, grid, in_specs, out_specs, out_shape, scratch_shapes=(), name=None):
    return pl.pallas_call(
        kernel,
        grid_spec=pltpu.PrefetchScalarGridSpec(
            num_scalar_prefetch=1, grid=grid, in_specs=in_specs, out_specs=out_specs,
            scratch_shapes=scratch_shapes),
        out_shape=out_shape,
        compiler_params=_cparams(len(grid)),
        name=name,
    )


def _ada_kernel(c_ref, w_ref, b_ref, o_ref):
    @pl.when(pl.program_id(1) == 0)
    def _():
        o_ref[0] = jnp.broadcast_to(b_ref[0], o_ref.shape[1:])

    a = _silu(c_ref[...]).astype(BF16)
    o_ref[0] += jnp.dot(a, w_ref[0].astype(BF16), preferred_element_type=F32)


def _ada_modulation(c, ada_w, ada_b):
    depth, d, n6 = ada_w.shape
    b = c.shape[0]
    rows = -(-b // SUBLANES) * SUBLANES
    c_pad = jnp.pad(c, ((0, rows - b), (0, 0)))
    tk = 128
    out = pl.pallas_call(
        _ada_kernel,
        grid=(depth, d // tk),
        in_specs=[
            pl.BlockSpec((rows, tk), lambda l, k: (0, k)),
            pl.BlockSpec((1, tk, n6), lambda l, k: (l, k, 0)),
            pl.BlockSpec((1, 1, n6), lambda l, k: (l, 0, 0)),
        ],
        out_specs=pl.BlockSpec((1, rows, n6), lambda l, k: (l, 0, 0)),
        out_shape=jax.ShapeDtypeStruct((depth, rows, n6), F32),
        compiler_params=_cparams(2),
        name="ada_modulation",
    )(c_pad, ada_w, ada_b.reshape(depth, 1, n6))
    return out[:, :b].reshape(depth, b, N_MOD, d).transpose(0, 2, 1, 3).reshape(depth, N_MOD, b, 1, d)


def _rope_kernel(ang_ref, cos_ref, sin_ref):
    ang = ang_ref[...]
    lane = lax.broadcasted_iota(jnp.int32, ang.shape, 1)
    first = (lane % DIFF_HEAD_DIM) < (DIFF_HEAD_DIM // 2)
    cos_ref[...] = jnp.cos(ang)
    s = jnp.sin(ang)
    sin_ref[...] = jnp.where(first, -s, s)


def _rope_tables(positions):
    m = positions.size
    half = DIFF_HEAD_DIM // 2
    inv_freq = ROPE_THETA ** (-jnp.arange(half, dtype=F32) * 2.0 / DIFF_HEAD_DIM)
    ang = positions.reshape(m, 1).astype(F32) * jnp.tile(inv_freq, LANES // half)[None, :]
    tm = min(1024, m)
    spec = pl.BlockSpec((tm, LANES), lambda i: (i, 0))
    return pl.pallas_call(
        _rope_kernel, grid=(m // tm,), in_specs=[spec], out_specs=[spec, spec],
        out_shape=[jax.ShapeDtypeStruct((m, LANES), F32)] * 2,
        compiler_params=_cparams(1), name="rope_tables",
    )(ang)


def _norm_mod_kernel(l_ref, x_ref, w_ref, sc_ref, sh_ref, o_ref):
    x = x_ref[...]
    ms = jnp.mean(x * x, axis=-1, keepdims=True)
    y = x * lax.rsqrt(ms + NORM_EPS) * w_ref[0]
    o_ref[...] = (y * (1.0 + sc_ref[0, 0, 0]) + sh_ref[0, 0, 0]).astype(BF16)


def _norm_mod(layer, x, norm_w, mod, scale_idx, shift_idx, seq):
    m, d = x.shape
    tm = min(512, seq)
    return _layer_call(
        _norm_mod_kernel, (m // tm,),
        [
            pl.BlockSpec((tm, d), lambda i, l: (i, 0)),
            pl.BlockSpec((1, 1, d), lambda i, l: (l[0], 0, 0)),
            pl.BlockSpec((1, 1, 1, 1, d), lambda i, l: (l[0], scale_idx, i * tm // seq, 0, 0)),
            pl.BlockSpec((1, 1, 1, 1, d), lambda i, l: (l[0], shift_idx, i * tm // seq, 0, 0)),
        ],
        pl.BlockSpec((tm, d), lambda i, l: (i, 0)),
        jax.ShapeDtypeStruct((m, d), BF16), name="norm_mod",
    )(layer, x, norm_w, mod, mod)


def _matmul(layer, xs, ws, w_x, w_off, extras, epilogue, out_dtypes, n, tm, tn, name):
    m = xs[0].shape[0]
    nx, nw, ne = len(xs), len(ws), len(extras)
    cast_w = ws[0].dtype == F32
    assert all((w.dtype == F32) == cast_w for w in ws)
    ij = (lambda a, b: (b, a)) if cast_w else (lambda a, b: (a, b))
    in_specs = [pl.BlockSpec((tm, x.shape[1]), lambda a, b, l: (ij(a, b)[0], 0)) for x in xs]
    for w, off in zip(ws, w_off):
        in_specs.append(pl.BlockSpec((1, w.shape[1], tn), lambda a, b, l, off=off: (l[0], 0, ij(a, b)[1] + off)))
    for _, bs, imap in extras:
        in_specs.append(pl.BlockSpec(bs, lambda a, b, l, imap=imap: imap(*ij(a, b), l)))
    out_specs = [pl.BlockSpec((tm, tn), lambda a, b, l: ij(a, b)) for _ in out_dtypes]
    out_shape = [jax.ShapeDtypeStruct((m, n), dt) for dt in out_dtypes]
    scratch = [pltpu.VMEM((w.shape[1], tn), BF16) for w in ws] if cast_w else []

    def kern(l_ref, *refs):
        x_refs, w_refs = refs[:nx], refs[nx:nx + nw]
        e_refs, o_refs = refs[nx + nw:nx + nw + ne], refs[nx + nw + ne:nx + nw + ne + len(out_dtypes)]
        if cast_w:
            wb_refs = refs[nx + nw + ne + len(out_dtypes):]

            @pl.when(pl.program_id(1) == 0)
            def _():
                for w_ref, wb_ref in zip(w_refs, wb_refs):
                    wb_ref[...] = w_ref[0].astype(BF16)

            w_tiles = [wb_ref[...] for wb_ref in wb_refs]
        else:
            w_tiles = [w_ref[0] for w_ref in w_refs]
        accs = [jnp.dot(x_refs[xi][...], w, preferred_element_type=F32) for xi, w in zip(w_x, w_tiles)]
        outs = epilogue(accs, [e[...] for e in e_refs])
        for o_ref, v in zip(o_refs, outs):
            o_ref[...] = v.astype(o_ref.dtype)

    grid = (n // tn, m // tm) if cast_w else (m // tm, n // tn)
    return _layer_call(kern, grid, in_specs, out_specs, out_shape, scratch_shapes=scratch, name=name)(
        layer, *xs, *ws, *[e[0] for e in extras])


def _residual_norm_kernel(l_ref, a_ref, w_ref, x_ref, gate_ref, nw_ref, sc_ref, sh_ref, xo_ref, ho_ref):
    acc = jnp.dot(a_ref[...], w_ref[0], preferred_element_type=F32)
    xn = x_ref[...] + gate_ref[0, 0, 0] * acc
    xo_ref[...] = xn
    ms = jnp.mean(xn * xn, axis=-1, keepdims=True)
    y = xn * lax.rsqrt(ms + NORM_EPS) * nw_ref[0]
    ho_ref[...] = (y * (1.0 + sc_ref[0, 0, 0]) + sh_ref[0, 0, 0]).astype(BF16)


def _matmul_residual_norm(layer, a, w, x, mod, gate_idx, norm_w, scale_idx, shift_idx, next_layer, seq, tm, name):
    m, k = a.shape
    depth, _, n = w.shape
    nl = (lambda l: jnp.minimum(l[0] + 1, depth - 1)) if next_layer else (lambda l: l[0])
    row = lambda i: i * tm // seq
    return _layer_call(
        _residual_norm_kernel, (m // tm,),
        [
            pl.BlockSpec((tm, k), lambda i, l: (i, 0)),
            pl.BlockSpec((1, k, n), lambda i, l: (l[0], 0, 0), pipeline_mode=pl.Buffered(1)),
            pl.BlockSpec((tm, n), lambda i, l: (i, 0)),
            pl.BlockSpec((1, 1, 1, 1, n), lambda i, l: (l[0], gate_idx, row(i), 0, 0)),
            pl.BlockSpec((1, 1, n), lambda i, l: (nl(l), 0, 0)),
            pl.BlockSpec((1, 1, 1, 1, n), lambda i, l: (nl(l), scale_idx, row(i), 0, 0)),
            pl.BlockSpec((1, 1, 1, 1, n), lambda i, l: (nl(l), shift_idx, row(i), 0, 0)),
        ],
        [pl.BlockSpec((tm, n), lambda i, l: (i, 0))] * 2,
        [jax.ShapeDtypeStruct((m, n), F32), jax.ShapeDtypeStruct((m, n), BF16)], name=name,
    )(layer, a, w, x, mod, norm_w, mod, mod)


def _mixer_out_kernel(l_ref, yd_ref, yg_ref, gd_ref, gg_ref, wd_ref, wg_ref, wo_ref, x_ref, gate_ref, nw_ref,
                      sc_ref, sh_ref, xo_ref, ho_ref):
    merged = (gd_ref[...] * jnp.dot(yd_ref[...], wd_ref[0], preferred_element_type=F32)
              + gg_ref[...] * jnp.dot(yg_ref[...], wg_ref[0], preferred_element_type=F32)).astype(BF16)
    acc = jnp.dot(merged, wo_ref[0], preferred_element_type=F32)
    xn = x_ref[...] + gate_ref[0, 0, 0] * acc
    xo_ref[...] = xn
    ms = jnp.mean(xn * xn, axis=-1, keepdims=True)
    y = xn * lax.rsqrt(ms + NORM_EPS) * nw_ref[0]
    ho_ref[...] = (y * (1.0 + sc_ref[0, 0, 0]) + sh_ref[0, 0, 0]).astype(BF16)


def _mixer_out(layer, y_diff, y_gdn, gates, w_d, w_g, w_o, x, mod, norm_w, seq):
    m, d = x.shape
    tm = min(256, seq)
    kd, kg = y_diff.shape[1], y_gdn.shape[1]
    row = lambda i: i * tm // seq
    resident = lambda k, n: pl.BlockSpec((1, k, n), lambda i, l: (l[0], 0, 0), pipeline_mode=pl.Buffered(1))
    modsp = lambda idx: pl.BlockSpec((1, 1, 1, 1, d), lambda i, l: (l[0], idx, row(i), 0, 0))
    return _layer_call(
        _mixer_out_kernel, (m // tm,),
        [
            pl.BlockSpec((tm, kd), lambda i, l: (i, 0)),
            pl.BlockSpec((tm, kg), lambda i, l: (i, 0)),
            pl.BlockSpec((tm, d), lambda i, l: (i, 0)),
            pl.BlockSpec((tm, d), lambda i, l: (i, 1)),
            resident(kd, d), resident(kg, d), resident(d, d),
            pl.BlockSpec((tm, d), lambda i, l: (i, 0)),
            modsp(2),
            pl.BlockSpec((1, 1, d), lambda i, l: (l[0], 0, 0)),
            modsp(4), modsp(3),
        ],
        [pl.BlockSpec((tm, d), lambda i, l: (i, 0))] * 2,
        [jax.ShapeDtypeStruct((m, d), F32), jax.ShapeDtypeStruct((m, d), BF16)], name="mixer_out",
    )(layer, y_diff, y_gdn, gates, gates, w_d, w_g, w_o, x, mod, norm_w, mod, mod)


def _group_sumsq(x, group_ones):
    sq = x * x
    hi = sq.astype(BF16)
    lo = (sq - hi.astype(F32)).astype(BF16)
    return (jnp.dot(hi, group_ones, preferred_element_type=F32)
            + jnp.dot(lo, group_ones, preferred_element_type=F32))


def _qk_prep_kernel(l_ref, q_ref, k_ref, v_ref, cos_ref, sin_ref, qw_ref, kw_ref, qo_ref, ko_ref, vo_ref):
    cos = cos_ref[...]
    sin = sin_ref[...]
    row = lax.broadcasted_iota(jnp.int32, (LANES, LANES), 0) // DIFF_HEAD_DIM
    col = lax.broadcasted_iota(jnp.int32, (LANES, LANES), 1) // DIFF_HEAD_DIM
    ones = (row == col).astype(BF16)
    lane = lax.broadcasted_iota(jnp.int32, cos.shape, 1)
    first = (lane % DIFF_HEAD_DIM) < (DIFF_HEAD_DIM // 2)
    half = DIFF_HEAD_DIM // 2

    def norm_rope(x, w, scale):
        ms = _group_sumsq(x, ones) * (1.0 / DIFF_HEAD_DIM)
        y = x * lax.rsqrt(ms + NORM_EPS) * w
        partner = jnp.where(first, pltpu.roll(y, LANES - half, 1), pltpu.roll(y, half, 1))
        return (y * cos + partner * sin) * scale

    q_scale = DIFF_HEAD_DIM ** -0.5 * math.log2(math.e)
    ones_blk = jnp.ones((cos.shape[0], LANES), BF16)
    for h in range(N_DIFF_HEADS):
        sl = slice(h * LANES, (h + 1) * LANES)
        qo_ref[:, sl] = norm_rope(q_ref[:, sl], qw_ref[0, :, sl], q_scale).astype(BF16)
        ko_ref[:, sl] = norm_rope(k_ref[:, sl], kw_ref[0, :, sl], 1.0).astype(BF16)
        vo_ref[:, 2 * h * LANES:(2 * h + 1) * LANES] = v_ref[:, sl].astype(BF16)
        vo_ref[:, (2 * h + 1) * LANES:(2 * h + 2) * LANES] = ones_blk


def _qk_prep(layer, proj, cos, sin, qn_w, kn_w):
    m = proj.shape[0]
    tm = min(512, m)
    w = DIFF_WIDTH
    rep = w // DIFF_HEAD_DIM
    depth = qn_w.shape[0]
    qw = jnp.tile(qn_w, (1, rep)).reshape(depth, 1, w)
    kw = jnp.tile(kn_w, (1, rep)).reshape(depth, 1, w)
    blk = lambda c: pl.BlockSpec((tm, w), lambda i, l, c=c: (i, c))
    tab = pl.BlockSpec((tm, LANES), lambda i, l: (i, 0))
    wsp = pl.BlockSpec((1, 1, w), lambda i, l: (l[0], 0, 0))
    osp = pl.BlockSpec((tm, w), lambda i, l: (i, 0))
    vsp = pl.BlockSpec((tm, 2 * w), lambda i, l: (i, 0))
    return _layer_call(
        _qk_prep_kernel, (m // tm,), [blk(0), blk(1), blk(2), tab, tab, wsp, wsp], [osp, osp, vsp],
        [jax.ShapeDtypeStruct((m, w), BF16)] * 2 + [jax.ShapeDtypeStruct((m, 2 * w), BF16)], name="diff_qk_prep",
    )(layer, proj, proj, proj, cos, sin, qw, kw)


def _diff_attn_kernel(l_ref, sc_ref, q_ref, k_ref, v_ref, w_ref, o_ref):
    lam = sc_ref[0]
    out_scale = sc_ref[1]
    k = k_ref[...]
    nt = (((1,), (1,)), ((), ()))
    sub = min(ATTN_SUB_ROWS, q_ref.shape[0])

    def branch(qm):
        s = lax.dot_general(qm, k, nt, preferred_element_type=F32)
        p = jnp.exp2(s - jnp.max(s, axis=-1, keepdims=True)).astype(BF16)
        ov = jnp.dot(p, v_ref[...], preferred_element_type=F32)
        return ov[:, :LANES] / ov[:, LANES:]

    for r in range(q_ref.shape[0] // sub):
        rows = slice(r * sub, (r + 1) * sub)
        q = q_ref[rows, :]
        lane = lax.broadcasted_iota(jnp.int32, q.shape, 1)
        zero = jnp.zeros_like(q)
        q0 = jnp.where(lane < DIFF_HEAD_DIM, q, zero)
        q1 = jnp.where(lane >= DIFF_HEAD_DIM, q, zero)
        o = branch(q0) - lam * branch(q1)
        ms = jnp.mean(o * o, axis=-1, keepdims=True)
        o_ref[rows, :] = (o * lax.rsqrt(ms + NORM_EPS) * w_ref[0] * out_scale).astype(BF16)


def _diff_attention(layer, scalars, q, k, v, subln_w, batch, seq):
    m = q.shape[0]
    tq = min(ATTN_Q_ROWS, seq)
    nq = seq // tq
    depth = subln_w.shape[0]
    return _layer_call(
        _diff_attn_kernel, (batch, N_DIFF_HEADS, nq),
        [
            pl.BlockSpec(memory_space=pltpu.SMEM),
            pl.BlockSpec((tq, LANES), lambda b, h, i, l: (b * nq + i, h)),
            pl.BlockSpec((seq, LANES), lambda b, h, i, l: (b, h)),
            pl.BlockSpec((seq, 2 * LANES), lambda b, h, i, l: (b, h)),
            pl.BlockSpec((1, 1, LANES), lambda b, h, i, l: (l[0], 0, 0)),
        ],
        pl.BlockSpec((tq, LANES), lambda b, h, i, l: (b * nq + i, h)),
        jax.ShapeDtypeStruct((m, DIFF_WIDTH), BF16), name="diff_attention",
    )(layer, scalars, q, k, v, subln_w.reshape(depth, 1, LANES))


def _split3(x):
    hi = x.astype(BF16)
    r1 = x - hi.astype(F32)
    mid = r1.astype(BF16)
    lo = (r1 - mid.astype(F32)).astype(BF16)
    return hi, mid, lo


def _gdn_prep_kernel(l_ref, main_ref, prev_ref, next_ref, small_ref, cw_ref, alog_ref, dtb_ref,
                     q_ref, k_ref, kt_ref, v_ref, beta_ref, gc_ref, xe_ref):
    t = pl.program_id(1)
    nt = pl.num_programs(1)
    ts = main_ref.shape[0]
    pad = CONV_WIDTH // 2
    xe_ref[0:SUBLANES, :] = jnp.where(t > 0, prev_ref[...], 0.0)
    xe_ref[SUBLANES:SUBLANES + ts, :] = main_ref[...]
    xe_ref[SUBLANES + ts:, :] = jnp.where(t < nt - 1, next_ref[...], 0.0)
    cw = cw_ref[0]
    first = SUBLANES - pad
    acc = cw[0:1, :] * xe_ref[first:first + ts, :]
    for j in range(1, CONV_WIDTH):
        acc = acc + cw[j:j + 1, :] * xe_ref[first + j:first + j + ts, :]
    y = _silu(acc)
    for h in range(N_GDN_HEADS):
        sl = slice(h * LANES, (h + 1) * LANES)
        qh = y[:, sl]
        kh = y[:, GDN_WIDTH + h * LANES:GDN_WIDTH + (h + 1) * LANES]
        q_ref[:, sl] = (qh * (lax.rsqrt(jnp.sum(qh * qh, axis=-1, keepdims=True) + NORM_EPS)
                              * GDN_HEAD_DIM ** -0.5)).astype(BF16)
        kn = kh * lax.rsqrt(jnp.sum(kh * kh, axis=-1, keepdims=True) + NORM_EPS)
        k_ref[:, sl] = kn.astype(BF16)
        kt_ref[sl, :] = kn.T.astype(BF16)
    v_ref[...] = y[:, 2 * GDN_WIDTH:].astype(BF16)

    nh2 = 2 * N_GDN_HEADS
    small = small_ref[...]
    beta_ref[...] = jax.nn.sigmoid(small[:, :nh2])
    g = -jnp.exp(alog_ref[0]) * jax.nn.softplus(small[:, nh2:2 * nh2] + dtb_ref[0])
    r = lax.broadcasted_iota(jnp.int32, (ts, ts), 0)
    c = lax.broadcasted_iota(jnp.int32, (ts, ts), 1)
    same = (r // CHUNK) == (c // CHUNK)
    pre = (same & (c <= r)).astype(BF16)
    suf = (same & (c >= r)).astype(BF16)
    parts = _split3(g)
    cs_f = sum(jnp.dot(pre, p, preferred_element_type=F32) for p in parts)
    cs_b = sum(jnp.dot(suf, p, preferred_element_type=F32) for p in parts)
    lane = lax.broadcasted_iota(jnp.int32, g.shape, 1)
    gc_ref[...] = jnp.where(lane < N_GDN_HEADS, cs_f, cs_b)


def _gdn_prep(layer, proj, small, conv_w, a_log, dt_bias, batch, seq):
    m = proj.shape[0]
    ts = min(256, seq)
    nts = seq // ts
    w3 = 3 * GDN_WIDTH
    cb = (3 * DIFF_WIDTH) // w3
    assert cb * w3 == 3 * DIFF_WIDTH
    depth = conv_w.shape[0]
    nh2 = 2 * N_GDN_HEADS
    hb = ts // SUBLANES
    last = m // SUBLANES - 1
    osp = pl.BlockSpec((ts, GDN_WIDTH), lambda b, t, l: (b * nts + t, 0))
    ssp = pl.BlockSpec((ts, nh2), lambda b, t, l: (b * nts + t, 0))
    return _layer_call(
        _gdn_prep_kernel, (batch, nts),
        [
            pl.BlockSpec((ts, w3), lambda b, t, l: (b * nts + t, cb)),
            pl.BlockSpec((SUBLANES, w3), lambda b, t, l: (jnp.maximum((b * nts + t) * hb - 1, 0), cb)),
            pl.BlockSpec((SUBLANES, w3), lambda b, t, l: (jnp.minimum((b * nts + t + 1) * hb, last), cb)),
            pl.BlockSpec((ts, LANES), lambda b, t, l: (b * nts + t, 0)),
            pl.BlockSpec((1, CONV_WIDTH, w3), lambda b, t, l: (l[0], 0, 0)),
            pl.BlockSpec((1, 1, nh2), lambda b, t, l: (l[0], 0, 0)),
            pl.BlockSpec((1, 1, nh2), lambda b, t, l: (l[0], 0, 0)),
        ],
        [osp, osp, pl.BlockSpec((GDN_WIDTH, ts), lambda b, t, l: (0, b * nts + t)), osp, ssp, ssp],
        [jax.ShapeDtypeStruct((m, GDN_WIDTH), BF16)] * 2 + [jax.ShapeDtypeStruct((GDN_WIDTH, m), BF16)]
        + [jax.ShapeDtypeStruct((m, GDN_WIDTH), BF16)] + [jax.ShapeDtypeStruct((m, nh2), F32)] * 2,
        scratch_shapes=[pltpu.VMEM((ts + 2 * SUBLANES, w3), F32)], name="gdn_prep",
    )(layer, proj, proj, proj, small, conv_w, a_log.reshape(depth, 1, nh2), dt_bias.reshape(depth, 1, nh2))


def _chunk_masks(backward):
    r = lax.broadcasted_iota(jnp.int32, (CHUNK, CHUNK), 0)
    c = lax.broadcasted_iota(jnp.int32, (CHUNK, CHUNK), 1)
    if backward:
        return r <= c, r < c
    return r >= c, r > c


def _gdn_l_kernel(l_ref, k_ref, beta_ref, gc_ref, gcrow_ref, lf_ref, lb_ref):
    h = pl.program_id(1)
    n_chunks = k_ref.shape[0] // CHUNK
    nt = (((1,), (1,)), ((), ()))
    lane16 = lax.broadcasted_iota(jnp.int32, (CHUNK, 2 * N_GDN_HEADS), 1)
    sub16 = lax.broadcasted_iota(jnp.int32, (2 * N_GDN_HEADS, CHUNK), 0)

    def body(n, carry):
        rows = pl.ds(pl.multiple_of(n * CHUNK, CHUNK), CHUNK)
        kb = k_ref[rows, :].astype(BF16)
        kk = lax.dot_general(kb, kb, nt, preferred_element_type=F32)
        beta = beta_ref[rows, :]
        gc = gc_ref[rows, :]
        gcr = gcrow_ref[0, n]
        for d, out_ref in ((0, lf_ref), (1, lb_ref)):
            col = d * N_GDN_HEADS + h
            bcol = jnp.sum(jnp.where(lane16 == col, beta, 0.0), axis=1, keepdims=True)
            gcol = jnp.sum(jnp.where(lane16 == col, gc, 0.0), axis=1, keepdims=True)
            grow = jnp.sum(jnp.where(sub16 == col, gcr, 0.0), axis=0, keepdims=True)
            _, strict = _chunk_masks(d == 1)
            dec = jnp.exp(jnp.where(strict, gcol - grow, NEG_BIG))
            out_ref[0, rows, :] = bcol * kk * dec
        return carry

    lax.fori_loop(0, n_chunks, body, 0, unroll=4)


def _gdn_build_l(layer, k, beta, gc, gc_rows, batch, seq):
    nh2 = 2 * N_GDN_HEADS
    n_chunks = seq // CHUNK
    osp = pl.BlockSpec((1, seq, CHUNK), lambda b, h, l: (b * N_GDN_HEADS + h, 0, 0))
    return _layer_call(
        _gdn_l_kernel, (batch, N_GDN_HEADS),
        [
            pl.BlockSpec((seq, LANES), lambda b, h, l: (b, h)),
            pl.BlockSpec((seq, nh2), lambda b, h, l: (b, 0)),
            pl.BlockSpec((seq, nh2), lambda b, h, l: (b, 0)),
            pl.BlockSpec((1, n_chunks, nh2, CHUNK), lambda b, h, l: (b, 0, 0, 0)),
        ],
        [osp, osp],
        [jax.ShapeDtypeStruct((batch * N_GDN_HEADS, seq, CHUNK), F32)] * 2, name="gdn_build_l",
    )(layer, k, beta, gc, gc_rows)


def _tri_inverse_kernel(l_ref, t_ref):
    sub = lax.broadcasted_iota(jnp.int32, (SUBLANES, LANES), 0)
    nblk = CHUNK // SUBLANES
    for i in range(CHUNK):
        live = i // SUBLANES + 1
        accs = [jnp.zeros((SUBLANES, LANES), F32) for _ in range(live)]
        for j in range(i):
            lij = l_ref[i, pl.ds(j, 1), :]
            for cb in range(j // SUBLANES + 1):
                accs[cb] = accs[cb] + lij * t_ref[j, cb * SUBLANES:(cb + 1) * SUBLANES, :]
        for cb in range(nblk):
            if cb < live - 1:
                val = -accs[cb]
            elif cb == live - 1:
                val = jnp.where(sub == i % SUBLANES, 1.0, 0.0) - accs[cb]
            else:
                val = jnp.zeros((SUBLANES, LANES), F32)
            t_ref[i, cb * SUBLANES:(cb + 1) * SUBLANES, :] = val


def _tri_inverse(l_all):
    g = l_all.shape[-1]
    spec = pl.BlockSpec((CHUNK, CHUNK, LANES), lambda i: (0, 0, i))
    return pl.pallas_call(
        _tri_inverse_kernel, grid=(g // LANES,), in_specs=[spec], out_specs=spec,
        out_shape=jax.ShapeDtypeStruct(l_all.shape, F32),
        compiler_params=_cparams(1), name="gdn_tri_inverse",
    )(l_all)


def _gdn_pre_kernel(l_ref, q_ref, k_ref, kt_ref, v_ref, beta_ref, gc_ref, gcrow_ref, gcpair_ref, tf_ref, tb_ref,
                    mqf_ref, mqb_ref, rf_ref, rb_ref, glf_ref, glb_ref, ol_ref):
    h = pl.program_id(1)
    n_local = q_ref.shape[0] // CHUNK
    dk = GDN_HEAD_DIM
    nt = (((1,), (1,)), ((), ()))
    lane16 = lax.broadcasted_iota(jnp.int32, (CHUNK, 2 * N_GDN_HEADS), 1)
    sub16 = lax.broadcasted_iota(jnp.int32, (2 * N_GDN_HEADS, CHUNK), 0)
    sub16p = lax.broadcasted_iota(jnp.int32, (2 * N_GDN_HEADS, LANES), 0)
    lane_half = lax.broadcasted_iota(jnp.int32, (1, LANES), 1) // CHUNK
    outs = ((tf_ref, mqf_ref, rf_ref, glf_ref), (tb_ref, mqb_ref, rb_ref, glb_ref))
    per_pair = LANES // CHUNK
    stage1 = []
    for c in range(n_local):
        rows = slice(c * CHUNK, (c + 1) * CHUNK)
        pair, half = c // per_pair, c % per_pair
        qc = q_ref[rows, :]
        kc = k_ref[rows, :]
        vc = v_ref[rows, :]
        kt_pair = kt_ref[:, pair * LANES:(pair + 1) * LANES]
        qk = lax.dot_general(qc.astype(BF16), kc.astype(BF16), nt, preferred_element_type=F32)
        for d in (0, 1):
            col = d * N_GDN_HEADS + h
            bcol = jnp.sum(jnp.where(lane16 == col, beta_ref[rows, :], 0.0), axis=1, keepdims=True)
            gcol = jnp.sum(jnp.where(lane16 == col, gc_ref[rows, :], 0.0), axis=1, keepdims=True)
            grow = jnp.sum(jnp.where(sub16 == col, gcrow_ref[0, c], 0.0), axis=0, keepdims=True)
            glast = gcol[0:1, :] if d == 1 else gcol[CHUNK - 1:CHUNK, :]
            incl, _ = _chunk_masks(d == 1)
            attn = (qk * jnp.exp(jnp.where(incl, gcol - grow, NEG_BIG))).astype(BF16)
            gam = jnp.exp(gcol)
            x = jnp.concatenate([bcol * vc, (bcol * gam) * kc], axis=1).astype(BF16)
            t = outs[d][0][0, rows, :]
            th = t.astype(BF16)
            tl = (t - th.astype(F32)).astype(BF16)
            uw = (jnp.dot(th, x, preferred_element_type=F32)
                  + jnp.dot(tl, x, preferred_element_type=F32)).astype(BF16)
            grow_pair = jnp.sum(jnp.where(sub16p == col, gcpair_ref[0, pair], 0.0), axis=0, keepdims=True)
            tail = jnp.exp(jnp.where(lane_half == half, glast - grow_pair, NEG_BIG))
            ktil_t = (kt_pair * tail).astype(BF16)
            stage1.append((c, d, attn, uw, ktil_t, gam * qc, jnp.exp(glast)))
    o_local = {}
    for c, d, attn, uw, ktil_t, gq, gl in stage1:
        _, mq_ref, r_ref, gl_ref = outs[d]
        pos = (n_local - 1 - c) if d == 1 else c
        awu = jnp.dot(attn, uw, preferred_element_type=F32)
        kwu = jnp.dot(ktil_t, jnp.concatenate([uw] * per_pair, axis=0), preferred_element_type=F32)
        mq_ref[0, pos, 0:dk, :] = kwu[:, dk:].astype(BF16)
        mq_ref[0, pos, dk:dk + CHUNK, :] = (gq - awu[:, dk:]).astype(BF16)
        r_ref[0, pos] = kwu[:, :dk].astype(BF16)
        gl_ref[0, pos] = jnp.broadcast_to(gl, (1, LANES))
        o_local[c] = o_local[c] + awu[:, :dk] if c in o_local else awu[:, :dk]
    for c, val in o_local.items():
        ol_ref[c * CHUNK:(c + 1) * CHUNK, :] = val


def _gdn_pre(layer, q, k, k_t, v, beta, gc, gc_rows, gc_pairs, t_f, t_b, batch, seq):
    nh2 = 2 * N_GDN_HEADS
    n_chunks = seq // CHUNK
    cg = min(GDN_PRE_CHUNKS, n_chunks)
    ng = n_chunks // cg
    rows = cg * CHUNK
    dk = GDN_HEAD_DIM
    bh = batch * N_GDN_HEADS
    hsp = pl.BlockSpec((rows, LANES), lambda b, h, g, l: (b * ng + g, h))
    ssp = pl.BlockSpec((rows, nh2), lambda b, h, g, l: (b * ng + g, 0))
    tsp = pl.BlockSpec((1, rows, CHUNK), lambda b, h, g, l: (b * N_GDN_HEADS + h, g, 0))

    def osp(r, mirrored):
        if mirrored:
            return pl.BlockSpec((1, cg, r, LANES), lambda b, h, g, l: (b * N_GDN_HEADS + h, ng - 1 - g, 0, 0))
        return pl.BlockSpec((1, cg, r, LANES), lambda b, h, g, l: (b * N_GDN_HEADS + h, g, 0, 0))

    shapes = [((dk + CHUNK), BF16), (dk, BF16), (1, F32)]
    out_specs, out_shape = [], []
    for r, dt in shapes:
        for mirrored in (False, True):
            out_specs.append(osp(r, mirrored))
            out_shape.append(jax.ShapeDtypeStruct((bh, n_chunks, r, LANES), dt))
    return _layer_call(
        _gdn_pre_kernel, (batch, N_GDN_HEADS, ng),
        [hsp, hsp, pl.BlockSpec((LANES, rows), lambda b, h, g, l: (h, b * ng + g)), hsp, ssp, ssp,
         pl.BlockSpec((1, cg, nh2, CHUNK), lambda b, h, g, l: (b, g, 0, 0)),
         pl.BlockSpec((1, rows // LANES, nh2, LANES), lambda b, h, g, l: (b, g, 0, 0)),
         tsp, tsp],
        out_specs + [hsp], out_shape + [jax.ShapeDtypeStruct(q.shape, F32)], name="gdn_chunk_pre",
    )(layer, q, k, k_t, v, beta, gc, gc_rows, gc_pairs, t_f, t_b)


def _gdn_state_kernel(l_ref, mqf_ref, mqb_ref, rf_ref, rb_ref, glf_ref, glb_ref, of_ref, ob_ref, st_ref):
    g = pl.program_id(1)
    n_heads, n_local = mqf_ref.shape[0], mqf_ref.shape[1]
    dk = GDN_HEAD_DIM

    @pl.when(g == 0)
    def _():
        st_ref[...] = jnp.zeros_like(st_ref)

    zero = jnp.zeros((dk, dk), BF16)
    states = [(st_ref[2 * hh], st_ref[2 * hh + 1]) for hh in range(n_heads)]
    for c in range(n_local):
        for hh in range(n_heads):
            sf, sb = states[hh]
            mq = jnp.concatenate([mqf_ref[hh, c], mqb_ref[hh, c]], axis=1)
            bd = jnp.concatenate([jnp.concatenate([sf.astype(BF16), zero], axis=1),
                                  jnp.concatenate([zero, sb.astype(BF16)], axis=1)], axis=0)
            res = jnp.dot(mq, bd, preferred_element_type=F32)
            hl = slice(hh * LANES, (hh + 1) * LANES)
            of_ref[c * CHUNK:(c + 1) * CHUNK, hl] = res[dk:, :dk]
            cb = n_local - 1 - c
            ob_ref[cb * CHUNK:(cb + 1) * CHUNK, hl] = res[dk:, dk:]
            states[hh] = (glf_ref[hh, c] * sf - res[:dk, :dk] + rf_ref[hh, c].astype(F32),
                          glb_ref[hh, c] * sb - res[:dk, dk:] + rb_ref[hh, c].astype(F32))
    for hh in range(n_heads):
        st_ref[2 * hh] = states[hh][0]
        st_ref[2 * hh + 1] = states[hh][1]


def _gdn_state(layer, pre, batch, seq):
    n_chunks = seq // CHUNK
    cg = min(GDN_SCAN_CHUNKS, n_chunks)
    ng = n_chunks // cg
    nh = N_GDN_HEADS
    dk = GDN_HEAD_DIM
    m = batch * seq
    in_specs = [pl.BlockSpec((nh, cg) + a.shape[2:], lambda b, g, l: (b, g, 0, 0)) for a in pre]
    return _layer_call(
        _gdn_state_kernel, (batch, ng), in_specs,
        [pl.BlockSpec((cg * CHUNK, GDN_WIDTH), lambda b, g, l: (b * ng + g, 0)),
         pl.BlockSpec((cg * CHUNK, GDN_WIDTH), lambda b, g, l: (b * ng + ng - 1 - g, 0))],
        [jax.ShapeDtypeStruct((m, GDN_WIDTH), F32)] * 2,
        scratch_shapes=[pltpu.VMEM((2 * nh, dk, dk), F32)], name="gdn_state_scan",
    )(layer, *pre)


def _gdn_out_kernel(l_ref, of_ref, ob_ref, ol_ref, z_ref, nw_ref, y_ref):
    for h in range(N_GDN_HEADS):
        hl = slice(h * LANES, (h + 1) * LANES)
        o = of_ref[:, hl] + ob_ref[:, hl] + ol_ref[:, hl]
        ms = jnp.mean(o * o, axis=-1, keepdims=True)
        y_ref[:, hl] = (o * lax.rsqrt(ms + NORM_EPS) * nw_ref[0] * _silu(z_ref[:, hl])).astype(BF16)


def _gdn_out(layer, o_f, o_b, o_local, proj, norm_w):
    m = o_f.shape[0]
    tm = min(512, m)
    depth = norm_w.shape[0]
    zcb = (3 * DIFF_WIDTH + 3 * GDN_WIDTH) // GDN_WIDTH
    osp = pl.BlockSpec((tm, GDN_WIDTH), lambda i, l: (i, 0))
    return _layer_call(
        _gdn_out_kernel, (m // tm,),
        [osp, osp, osp, pl.BlockSpec((tm, GDN_WIDTH), lambda i, l: (i, zcb)),
         pl.BlockSpec((1, 1, LANES), lambda i, l: (l[0], 0, 0))],
        osp, jax.ShapeDtypeStruct((m, GDN_WIDTH), BF16), name="gdn_out",
    )(layer, o_f, o_b, o_local, proj, norm_w.reshape(depth, 1, LANES))


def _gated_deltanet(layer, proj, small, conv_w, a_log, dt_bias, norm_w, batch, seq):
    q, k, k_t, v, beta, gc = _gdn_prep(layer, proj, small, conv_w, a_log, dt_bias, batch, seq)
    n_chunks = seq // CHUNK
    nh2 = 2 * N_GDN_HEADS
    gc_rows = gc.reshape(batch, n_chunks, CHUNK, nh2).transpose(0, 1, 3, 2)
    gc_pairs = gc.reshape(batch, seq // LANES, LANES, nh2).transpose(0, 1, 3, 2)
    l_f, l_b = _gdn_build_l(layer, k, beta, gc, gc_rows, batch, seq)
    g0 = batch * N_GDN_HEADS * n_chunks
    t_f = _tri_inverse(l_f.reshape(g0, CHUNK, CHUNK).transpose(1, 2, 0))
    t_b = _tri_inverse(l_b.reshape(g0, CHUNK, CHUNK).transpose(2, 1, 0))
    t_f = t_f.transpose(2, 0, 1).reshape(batch * N_GDN_HEADS, seq, CHUNK)
    t_b = t_b.transpose(2, 1, 0).reshape(batch * N_GDN_HEADS, seq, CHUNK)
    *pre, o_local = _gdn_pre(layer, q, k, k_t, v, beta, gc, gc_rows, gc_pairs, t_f, t_b, batch, seq)
    o_f, o_b = _gdn_state(layer, pre, batch, seq)
    return _gdn_out(layer, o_f, o_b, o_local, proj, norm_w)


def _layer(l, x, h, cos, sin, mod, lam_inits, p, batch, seq):
    m, d = x.shape
    layer = jnp.reshape(l, (1,)).astype(jnp.int32)
    tm = min(1024, seq)
    ident = lambda accs, extras: accs

    (proj,) = _matmul(layer, [h], [p["w_main"]], [0], [0], [], ident, [F32], MAIN_COLS, tm, 1024, "proj_main")
    (small,) = _matmul(layer, [h], [p["w_small"]], [0], [0], [], ident, [F32], LANES, tm, LANES, "proj_small")
    (gates,) = _matmul(layer, [h], [p["w_gates"]], [0], [0], [],
                       lambda accs, extras: [jax.nn.sigmoid(accs[0])], [BF16], 2 * d, tm, 1024, "proj_gates")

    lam_init = lam_inits[l]
    lv = p["diff_lambda"][l].astype(F32)
    lam = jnp.exp(jnp.sum(lv[0] * lv[1])) - jnp.exp(jnp.sum(lv[2] * lv[3])) + lam_init
    scalars = jnp.stack([lam, 1.0 - lam_init]).astype(F32)
    dq, dk, dv = _qk_prep(layer, proj, cos, sin, p["diff_qn_w"], p["diff_kn_w"])
    y_diff = _diff_attention(layer, scalars, dq, dk, dv, p["diff_subln_w"], batch, seq)

    y_gdn = _gated_deltanet(layer, proj, small, p["gdn_conv_w"], p["gdn_a_log"], p["gdn_dt_bias"],
                            p["gdn_norm_w"], batch, seq)

    x, h = _mixer_out(layer, y_diff, y_gdn, gates, p["w_branch_diff"], p["w_branch_gdn"], p["w_out"], x, mod,
                      p["norm_ffn_w"], seq)
    f = p["ffn_w_down"].shape[1]
    tf = 512
    (act,) = _matmul(layer, [h], [p["ffn_w_up"], p["ffn_w_up"]], [0, 0], [0, f // tf], [],
                     lambda accs, extras: [_silu(accs[0]) * accs[1]], [BF16], f, tm, tf, "ffn_up")
    return tuple(_matmul_residual_norm(layer, act, p["ffn_w_down"], x, mod, 5, p["norm_mix_w"], 1, 0, True, seq,
                                       min(256, seq), "ffn_down"))


def kernel(x, c, positions, ada_w, ada_b, norm_mix_w, norm_ffn_w, w_in, diff_qn_w, diff_kn_w, diff_lambda,
           diff_subln_w, gdn_conv_w, gdn_a_log, gdn_dt_bias, gdn_norm_w, w_branch_diff, w_branch_gdn, w_out,
           ffn_w_up, ffn_w_down):
    batch, seq, d = x.shape
    depth = ada_w.shape[0]
    mod = _ada_modulation(c, ada_w, ada_b)
    cos, sin = _rope_tables(positions)
    lam_inits = jnp.asarray([0.8 - 0.6 * math.exp(-0.3 * i) for i in range(depth)], F32)
    small_w = jnp.pad(w_in[:, :, MAIN_COLS:MAIN_COLS + SMALL_COLS], ((0, 0), (0, 0), (0, LANES - SMALL_COLS)))
    p = {
        "norm_mix_w": norm_mix_w.reshape(depth, 1, d), "norm_ffn_w": norm_ffn_w.reshape(depth, 1, d),
        "w_main": w_in[:, :, :MAIN_COLS].astype(BF16),
        "w_small": small_w.astype(BF16),
        "w_gates": w_in[:, :, MAIN_COLS + SMALL_COLS:].astype(BF16),
        "diff_qn_w": diff_qn_w, "diff_kn_w": diff_kn_w, "diff_lambda": diff_lambda, "diff_subln_w": diff_subln_w,
        "gdn_conv_w": gdn_conv_w, "gdn_a_log": gdn_a_log, "gdn_dt_bias": gdn_dt_bias, "gdn_norm_w": gdn_norm_w,
        "w_branch_diff": w_branch_diff.astype(BF16), "w_branch_gdn": w_branch_gdn.astype(BF16),
        "w_out": w_out.astype(BF16), "ffn_w_up": ffn_w_up, "ffn_w_down": ffn_w_down.astype(BF16),
    }
    x0 = x.reshape(batch * seq, d)
    h0 = _norm_mod(jnp.zeros((1,), jnp.int32), x0, p["norm_mix_w"], mod, 1, 0, seq)
    carry = (x0, h0)
    for l in range(depth):
        carry = _layer(jnp.int32(l), carry[0], carry[1], cos, sin, mod, lam_inits, p, batch, seq)
    return carry[0].reshape(batch, seq, d)
```

```python
import functools
import math

import jax
import jax.numpy as jnp
from jax import lax
from jax.experimental import pallas as pl
from jax.experimental.pallas import tpu as pltpu

F32 = jnp.float32
BF16 = jnp.bfloat16

N_DIFF_HEADS = 8
DIFF_HEAD_DIM = 64
DIFF_WIDTH = N_DIFF_HEADS * 2 * DIFF_HEAD_DIM
N_GDN_HEADS = 8
GDN_HEAD_DIM = 128
GDN_WIDTH = N_GDN_HEADS * GDN_HEAD_DIM
CONV_WIDTH = 5
CHUNK = 64
ROPE_THETA = 10000.0
NORM_EPS = 1e-6
N_MOD = 6
LANES = 128
SUBLANES = 8
NEG_BIG = -1e30
GDN_PRE_CHUNKS = 16
GDN_SCAN_CHUNKS = 8
ATTN_Q_ROWS = 2048
ATTN_SUB_ROWS = 256

MAIN_COLS = 3 * DIFF_WIDTH + 4 * GDN_WIDTH
SMALL_COLS = 4 * N_GDN_HEADS
VMEM_LIMIT = 48 * 1024 * 1024


def _silu(x):
    return x * jax.nn.sigmoid(x)


def _cparams(n_axes, vmem=VMEM_LIMIT):
    return pltpu.CompilerParams(dimension_semantics=("arbitrary",) * n_axes, vmem_limit_bytes=vmem)


def _layer_call(kernel, grid, in_specs, out_specs, out_shape, scratch_shapes=(), name=None):
    return pl.pallas_call(
        kernel,
        grid_spec=pltpu.PrefetchScalarGridSpec(
            num_scalar_prefetch=1, grid=grid, in_specs=in_specs, out_specs=out_specs,
            scratch_shapes=scratch_shapes),
        out_shape=out_shape,
        compiler_params=_cparams(len(grid)),
        name=name,
    )


def _ada_kernel(c_ref, w_ref, b_ref, o_ref):
    @pl.when(pl.program_id(1) == 0)
    def _():
        o_ref[0] = jnp.broadcast_to(b_ref[0], o_ref.shape[1:])

    a = _silu(c_ref[...]).astype(BF16)
    o_ref[0] += jnp.dot(a, w_ref[0].astype(BF16), preferred_element_type=F32)


def _ada_modulation(c, ada_w, ada_b):
    depth, d, n6 = ada_w.shape
    b = c.shape[0]
    rows = -(-b // SUBLANES) * SUBLANES
    c_pad = jnp.pad(c, ((0, rows - b), (0, 0)))
    tk = 128
    out = pl.pallas_call(
        _ada_kernel,
        grid=(depth, d // tk),
        in_specs=[
            pl.BlockSpec((rows, tk), lambda l, k: (0, k)),
            pl.BlockSpec((1, tk, n6), lambda l, k: (l, k, 0)),
            pl.BlockSpec((1, 1, n6), lambda l, k: (l, 0, 0)),
        ],
        out_specs=pl.BlockSpec((1, rows, n6), lambda l, k: (l, 0, 0)),
        out_shape=jax.ShapeDtypeStruct((depth, rows, n6), F32),
        compiler_params=_cparams(2),
        name="ada_modulation",
    )(c_pad, ada_w, ada_b.reshape(depth, 1, n6))
    return out[:, :b].reshape(depth, b, N_MOD, d).transpose(0, 2, 1, 3).reshape(depth, N_MOD, b, 1, d)


def _rope_kernel(ang_ref, cos_ref, sin_ref):
    ang = ang_ref[...]
    lane = lax.broadcasted_iota(jnp.int32, ang.shape, 1)
    first = (lane % DIFF_HEAD_DIM) < (DIFF_HEAD_DIM // 2)
    cos_ref[...] = jnp.cos(ang)
    s = jnp.sin(ang)
    sin_ref[...] = jnp.where(first, -s, s)


def _rope_tables(positions):
    m = positions.size
    half = DIFF_HEAD_DIM // 2
    inv_freq = ROPE_THETA ** (-jnp.arange(half, dtype=F32) * 2.0 / DIFF_HEAD_DIM)
    ang = positions.reshape(m, 1).astype(F32) * jnp.tile(inv_freq, LANES // half)[None, :]
    tm = min(1024, m)
    spec = pl.BlockSpec((tm, LANES), lambda i: (i, 0))
    return pl.pallas_call(
        _rope_kernel, grid=(m // tm,), in_specs=[spec], out_specs=[spec, spec],
        out_shape=[jax.ShapeDtypeStruct((m, LANES), F32)] * 2,
        compiler_params=_cparams(1), name="rope_tables",
    )(ang)


def _norm_mod_kernel(l_ref, x_ref, w_ref, sc_ref, sh_ref, o_ref):
    x = x_ref[...]
    ms = jnp.mean(x * x, axis=-1, keepdims=True)
    y = x * lax.rsqrt(ms + NORM_EPS) * w_ref[0]
    o_ref[...] = (y * (1.0 + sc_ref[0, 0, 0]) + sh_ref[0, 0, 0]).astype(BF16)


def _norm_mod(layer, x, norm_w, mod, scale_idx, shift_idx, seq):
    m, d = x.shape
    tm = min(512, seq)
    return _layer_call(
        _norm_mod_kernel, (m // tm,),
        [
            pl.BlockSpec((tm, d), lambda i, l: (i, 0)),
            pl.BlockSpec((1, 1, d), lambda i, l: (l[0], 0, 0)),
            pl.BlockSpec((1, 1, 1, 1, d), lambda i, l: (l[0], scale_idx, i * tm // seq, 0, 0)),
            pl.BlockSpec((1, 1, 1, 1, d), lambda i, l: (l[0], shift_idx, i * tm // seq, 0, 0)),
        ],
        pl.BlockSpec((tm, d), lambda i, l: (i, 0)),
        jax.ShapeDtypeStruct((m, d), BF16), name="norm_mod",
    )(layer, x, norm_w, mod, mod)


def _matmul(layer, xs, ws, w_x, w_off, extras, epilogue, out_dtypes, n, tm, tn, name, w_transposed=False):
    m = xs[0].shape[0]
    nx, nw, ne = len(xs), len(ws), len(extras)
    cast_w = ws[0].dtype == F32
    assert all((w.dtype == F32) == cast_w for w in ws)
    ij = (lambda a, b: (b, a)) if cast_w else (lambda a, b: (a, b))
    in_specs = [pl.BlockSpec((tm, x.shape[1]), lambda a, b, l: (ij(a, b)[0], 0)) for x in xs]
    for w, off in zip(ws, w_off):
        if w_transposed:
            if off % tn:
                in_specs.append(pl.BlockSpec((pl.Element(1), pl.Element(tn), pl.Element(w.shape[2])),
                                             lambda a, b, l, off=off: (
                                                 l[0], pl.multiple_of(ij(a, b)[1] * tn + off, math.gcd(off, tn)), 0)))
                continue
            off = off // tn
            in_specs.append(pl.BlockSpec((1, tn, w.shape[2]),
                                         lambda a, b, l, off=off: (l[0], ij(a, b)[1] + off, 0)))
        else:
            in_specs.append(pl.BlockSpec((1, w.shape[1], tn),
                                         lambda a, b, l, off=off: (l[0], 0, ij(a, b)[1] + off)))
    for _, bs, imap in extras:
        in_specs.append(pl.BlockSpec(bs, lambda a, b, l, imap=imap: imap(*ij(a, b), l)))
    out_specs = [pl.BlockSpec((tm, tn), lambda a, b, l: ij(a, b)) for _ in out_dtypes]
    out_shape = [jax.ShapeDtypeStruct((m, n), dt) for dt in out_dtypes]
    scratch = []
    if cast_w:
        scratch = [pltpu.VMEM((tn, w.shape[2]) if w_transposed else (w.shape[1], tn), BF16) for w in ws]
    dims = (((1,), (1,)), ((), ())) if w_transposed else (((1,), (0,)), ((), ()))

    def kern(l_ref, *refs):
        x_refs, w_refs = refs[:nx], refs[nx:nx + nw]
        e_refs, o_refs = refs[nx + nw:nx + nw + ne], refs[nx + nw + ne:nx + nw + ne + len(out_dtypes)]
        if cast_w:
            wb_refs = refs[nx + nw + ne + len(out_dtypes):]

            @pl.when(pl.program_id(1) == 0)
            def _():
                for w_ref, wb_ref in zip(w_refs, wb_refs):
                    wb_ref[...] = w_ref[0].astype(BF16)

            w_tiles = [wb_ref[...] for wb_ref in wb_refs]
        else:
            w_tiles = [w_ref[0] for w_ref in w_refs]
        accs = [lax.dot_general(x_refs[xi][...], w, dims, preferred_element_type=F32)
                for xi, w in zip(w_x, w_tiles)]
        outs = epilogue(accs, [e[...] for e in e_refs])
        for o_ref, v in zip(o_refs, outs):
            o_ref[...] = v.astype(o_ref.dtype)

    grid = (n // tn, m // tm) if cast_w else (m // tm, n // tn)
    return _layer_call(kern, grid, in_specs, out_specs, out_shape, scratch_shapes=scratch, name=name)(
        layer, *xs, *ws, *[e[0] for e in extras])


def _residual_norm_kernel(l_ref, a_ref, w_ref, x_ref, gate_ref, nw_ref, sc_ref, sh_ref, xo_ref, ho_ref):
    acc = jnp.dot(a_ref[...], w_ref[0], preferred_element_type=F32)
    xn = x_ref[...] + gate_ref[0, 0, 0] * acc
    xo_ref[...] = xn
    ms = jnp.mean(xn * xn, axis=-1, keepdims=True)
    y = xn * lax.rsqrt(ms + NORM_EPS) * nw_ref[0]
    ho_ref[...] = (y * (1.0 + sc_ref[0, 0, 0]) + sh_ref[0, 0, 0]).astype(BF16)


def _matmul_residual_norm(layer, a, w, x, mod, gate_idx, norm_w, scale_idx, shift_idx, next_layer, seq, tm, name):
    m, k = a.shape
    depth, _, n = w.shape
    nl = (lambda l: jnp.minimum(l[0] + 1, depth - 1)) if next_layer else (lambda l: l[0])
    row = lambda i: i * tm // seq
    return _layer_call(
        _residual_norm_kernel, (m // tm,),
        [
            pl.BlockSpec((tm, k), lambda i, l: (i, 0)),
            pl.BlockSpec((1, k, n), lambda i, l: (l[0], 0, 0), pipeline_mode=pl.Buffered(1)),
            pl.BlockSpec((tm, n), lambda i, l: (i, 0)),
            pl.BlockSpec((1, 1, 1, 1, n), lambda i, l: (l[0], gate_idx, row(i), 0, 0)),
            pl.BlockSpec((1, 1, n), lambda i, l: (nl(l), 0, 0)),
            pl.BlockSpec((1, 1, 1, 1, n), lambda i, l: (nl(l), scale_idx, row(i), 0, 0)),
            pl.BlockSpec((1, 1, 1, 1, n), lambda i, l: (nl(l), shift_idx, row(i), 0, 0)),
        ],
        [pl.BlockSpec((tm, n), lambda i, l: (i, 0))] * 2,
        [jax.ShapeDtypeStruct((m, n), F32), jax.ShapeDtypeStruct((m, n), BF16)], name=name,
    )(layer, a, w, x, mod, norm_w, mod, mod)


def _mixer_out_kernel(l_ref, yd_ref, yg_ref, gd_ref, gg_ref, wd_ref, wg_ref, wo_ref, x_ref, gate_ref, nw_ref,
                      sc_ref, sh_ref, xo_ref, ho_ref):
    merged = (gd_ref[...] * jnp.dot(yd_ref[...], wd_ref[0], preferred_element_type=F32)
              + gg_ref[...] * jnp.dot(yg_ref[...], wg_ref[0], preferred_element_type=F32)).astype(BF16)
    acc = jnp.dot(merged, wo_ref[0], preferred_element_type=F32)
    xn = x_ref[...] + gate_ref[0, 0, 0] * acc
    xo_ref[...] = xn
    ms = jnp.mean(xn * xn, axis=-1, keepdims=True)
    y = xn * lax.rsqrt(ms + NORM_EPS) * nw_ref[0]
    ho_ref[...] = (y * (1.0 + sc_ref[0, 0, 0]) + sh_ref[0, 0, 0]).astype(BF16)


def _mixer_out(layer, y_diff, y_gdn, gates, w_d, w_g, w_o, x, mod, norm_w, seq):
    m, d = x.shape
    tm = min(256, seq)
    kd, kg = y_diff.shape[1], y_gdn.shape[1]
    row = lambda i: i * tm // seq
    resident = lambda k, n: pl.BlockSpec((1, k, n), lambda i, l: (l[0], 0, 0), pipeline_mode=pl.Buffered(1))
    modsp = lambda idx: pl.BlockSpec((1, 1, 1, 1, d), lambda i, l: (l[0], idx, row(i), 0, 0))
    return _layer_call(
        _mixer_out_kernel, (m // tm,),
        [
            pl.BlockSpec((tm, kd), lambda i, l: (i, 0)),
            pl.BlockSpec((tm, kg), lambda i, l: (i, 0)),
            pl.BlockSpec((tm, d), lambda i, l: (i, 0)),
            pl.BlockSpec((tm, d), lambda i, l: (i, 1)),
            resident(kd, d), resident(kg, d), resident(d, d),
            pl.BlockSpec((tm, d), lambda i, l: (i, 0)),
            modsp(2),
            pl.BlockSpec((1, 1, d), lambda i, l: (l[0], 0, 0)),
            modsp(4), modsp(3),
        ],
        [pl.BlockSpec((tm, d), lambda i, l: (i, 0))] * 2,
        [jax.ShapeDtypeStruct((m, d), F32), jax.ShapeDtypeStruct((m, d), BF16)], name="mixer_out",
    )(layer, y_diff, y_gdn, gates, gates, w_d, w_g, w_o, x, mod, norm_w, mod, mod)


def _group_sumsq(x, group_ones):
    sq = x * x
    hi = sq.astype(BF16)
    lo = (sq - hi.astype(F32)).astype(BF16)
    return (jnp.dot(hi, group_ones, preferred_element_type=F32)
            + jnp.dot(lo, group_ones, preferred_element_type=F32))


def _qk_prep_kernel(l_ref, q_ref, k_ref, v_ref, cos_ref, sin_ref, qw_ref, kw_ref, qo_ref, ko_ref, vo_ref):
    cos = cos_ref[...]
    sin = sin_ref[...]
    row = lax.broadcasted_iota(jnp.int32, (LANES, LANES), 0) // DIFF_HEAD_DIM
    col = lax.broadcasted_iota(jnp.int32, (LANES, LANES), 1) // DIFF_HEAD_DIM
    ones = (row == col).astype(BF16)
    lane = lax.broadcasted_iota(jnp.int32, cos.shape, 1)
    first = (lane % DIFF_HEAD_DIM) < (DIFF_HEAD_DIM // 2)
    half = DIFF_HEAD_DIM // 2

    def norm_rope(x, w, scale):
        ms = _group_sumsq(x, ones) * (1.0 / DIFF_HEAD_DIM)
        y = x * lax.rsqrt(ms + NORM_EPS) * w
        partner = jnp.where(first, pltpu.roll(y, LANES - half, 1), pltpu.roll(y, half, 1))
        return (y * cos + partner * sin) * scale

    q_scale = DIFF_HEAD_DIM ** -0.5 * math.log2(math.e)
    ones_blk = jnp.ones((cos.shape[0], LANES), BF16)
    for h in range(N_DIFF_HEADS):
        sl = slice(h * LANES, (h + 1) * LANES)
        qo_ref[:, sl] = norm_rope(q_ref[:, sl], qw_ref[0, :, sl], q_scale).astype(BF16)
        ko_ref[:, sl] = norm_rope(k_ref[:, sl], kw_ref[0, :, sl], 1.0).astype(BF16)
        vo_ref[:, 2 * h * LANES:(2 * h + 1) * LANES] = v_ref[:, sl].astype(BF16)
        vo_ref[:, (2 * h + 1) * LANES:(2 * h + 2) * LANES] = ones_blk


def _qk_prep(layer, proj, cos, sin, qn_w, kn_w):
    m = proj.shape[0]
    tm = min(512, m)
    w = DIFF_WIDTH
    rep = w // DIFF_HEAD_DIM
    depth = qn_w.shape[0]
    qw = jnp.tile(qn_w, (1, rep)).reshape(depth, 1, w)
    kw = jnp.tile(kn_w, (1, rep)).reshape(depth, 1, w)
    blk = lambda c: pl.BlockSpec((tm, w), lambda i, l, c=c: (i, c))
    tab = pl.BlockSpec((tm, LANES), lambda i, l: (i, 0))
    wsp = pl.BlockSpec((1, 1, w), lambda i, l: (l[0], 0, 0))
    osp = pl.BlockSpec((tm, w), lambda i, l: (i, 0))
    vsp = pl.BlockSpec((tm, 2 * w), lambda i, l: (i, 0))
    return _layer_call(
        _qk_prep_kernel, (m // tm,), [blk(0), blk(1), blk(2), tab, tab, wsp, wsp], [osp, osp, vsp],
        [jax.ShapeDtypeStruct((m, w), BF16)] * 2 + [jax.ShapeDtypeStruct((m, 2 * w), BF16)], name="diff_qk_prep",
    )(layer, proj, proj, proj, cos, sin, qw, kw)


def _diff_attn_kernel(l_ref, sc_ref, q_ref, k_ref, v_ref, w_ref, o_ref):
    lam = sc_ref[0]
    out_scale = sc_ref[1]
    k = k_ref[...]
    nt = (((1,), (1,)), ((), ()))
    sub = min(ATTN_SUB_ROWS, q_ref.shape[0])

    def branch(qm):
        s = lax.dot_general(qm, k, nt, preferred_element_type=F32)
        p = jnp.exp2(s - jnp.max(s, axis=-1, keepdims=True)).astype(BF16)
        ov = jnp.dot(p, v_ref[...], preferred_element_type=F32)
        return ov[:, :LANES] / ov[:, LANES:]

    for r in range(q_ref.shape[0] // sub):
        rows = slice(r * sub, (r + 1) * sub)
        q = q_ref[rows, :]
        lane = lax.broadcasted_iota(jnp.int32, q.shape, 1)
        zero = jnp.zeros_like(q)
        q0 = jnp.where(lane < DIFF_HEAD_DIM, q, zero)
        q1 = jnp.where(lane >= DIFF_HEAD_DIM, q, zero)
        o = branch(q0) - lam * branch(q1)
        ms = jnp.mean(o * o, axis=-1, keepdims=True)
        o_ref[rows, :] = (o * lax.rsqrt(ms + NORM_EPS) * w_ref[0] * out_scale).astype(BF16)


def _diff_attention(layer, scalars, q, k, v, subln_w, batch, seq):
    m = q.shape[0]
    tq = min(ATTN_Q_ROWS, seq)
    nq = seq // tq
    depth = subln_w.shape[0]
    return _layer_call(
        _diff_attn_kernel, (batch, N_DIFF_HEADS, nq),
        [
            pl.BlockSpec(memory_space=pltpu.SMEM),
            pl.BlockSpec((tq, LANES), lambda b, h, i, l: (b * nq + i, h)),
            pl.BlockSpec((seq, LANES), lambda b, h, i, l: (b, h)),
            pl.BlockSpec((seq, 2 * LANES), lambda b, h, i, l: (b, h)),
            pl.BlockSpec((1, 1, LANES), lambda b, h, i, l: (l[0], 0, 0)),
        ],
        pl.BlockSpec((tq, LANES), lambda b, h, i, l: (b * nq + i, h)),
        jax.ShapeDtypeStruct((m, DIFF_WIDTH), BF16), name="diff_attention",
    )(layer, scalars, q, k, v, subln_w.reshape(depth, 1, LANES))


def _split3(x):
    hi = x.astype(BF16)
    r1 = x - hi.astype(F32)
    mid = r1.astype(BF16)
    lo = (r1 - mid.astype(F32)).astype(BF16)
    return hi, mid, lo


def _gdn_prep_kernel(l_ref, main_ref, prev_ref, next_ref, small_ref, cw_ref, alog_ref, dtb_ref,
                     q_ref, k_ref, kt_ref, v_ref, beta_ref, gc_ref, xe_ref):
    t = pl.program_id(1)
    nt = pl.num_programs(1)
    ts = main_ref.shape[0]
    pad = CONV_WIDTH // 2
    xe_ref[0:SUBLANES, :] = jnp.where(t > 0, prev_ref[...], 0.0)
    xe_ref[SUBLANES:SUBLANES + ts, :] = main_ref[...]
    xe_ref[SUBLANES + ts:, :] = jnp.where(t < nt - 1, next_ref[...], 0.0)
    cw = cw_ref[0]
    first = SUBLANES - pad
    acc = cw[0:1, :] * xe_ref[first:first + ts, :]
    for j in range(1, CONV_WIDTH):
        acc = acc + cw[j:j + 1, :] * xe_ref[first + j:first + j + ts, :]
    y = _silu(acc)
    for h in range(N_GDN_HEADS):
        sl = slice(h * LANES, (h + 1) * LANES)
        qh = y[:, sl]
        kh = y[:, GDN_WIDTH + h * LANES:GDN_WIDTH + (h + 1) * LANES]
        q_ref[:, sl] = (qh * (lax.rsqrt(jnp.sum(qh * qh, axis=-1, keepdims=True) + NORM_EPS)
                              * GDN_HEAD_DIM ** -0.5)).astype(BF16)
        kn = kh * lax.rsqrt(jnp.sum(kh * kh, axis=-1, keepdims=True) + NORM_EPS)
        k_ref[:, sl] = kn.astype(BF16)
        kt_ref[sl, :] = kn.T.astype(BF16)
    v_ref[...] = y[:, 2 * GDN_WIDTH:].astype(BF16)

    nh2 = 2 * N_GDN_HEADS
    small = small_ref[...]
    beta_ref[...] = jax.nn.sigmoid(small[:, :nh2])
    g = -jnp.exp(alog_ref[0]) * jax.nn.softplus(small[:, nh2:2 * nh2] + dtb_ref[0])
    r = lax.broadcasted_iota(jnp.int32, (ts, ts), 0)
    c = lax.broadcasted_iota(jnp.int32, (ts, ts), 1)
    same = (r // CHUNK) == (c // CHUNK)
    pre = (same & (c <= r)).astype(BF16)
    suf = (same & (c >= r)).astype(BF16)
    parts = _split3(g)
    cs_f = sum(jnp.dot(pre, p, preferred_element_type=F32) for p in parts)
    cs_b = sum(jnp.dot(suf, p, preferred_element_type=F32) for p in parts)
    lane = lax.broadcasted_iota(jnp.int32, g.shape, 1)
    gc_ref[...] = jnp.where(lane < N_GDN_HEADS, cs_f, cs_b)


def _gdn_prep(layer, proj, small, conv_w, a_log, dt_bias, batch, seq):
    m = proj.shape[0]
    ts = min(256, seq)
    nts = seq // ts
    w3 = 3 * GDN_WIDTH
    cb = (3 * DIFF_WIDTH) // w3
    assert cb * w3 == 3 * DIFF_WIDTH
    depth = conv_w.shape[0]
    nh2 = 2 * N_GDN_HEADS
    hb = ts // SUBLANES
    last = m // SUBLANES - 1
    osp = pl.BlockSpec((ts, GDN_WIDTH), lambda b, t, l: (b * nts + t, 0))
    ssp = pl.BlockSpec((ts, nh2), lambda b, t, l: (b * nts + t, 0))
    return _layer_call(
        _gdn_prep_kernel, (batch, nts),
        [
            pl.BlockSpec((ts, w3), lambda b, t, l: (b * nts + t, cb)),
            pl.BlockSpec((SUBLANES, w3), lambda b, t, l: (jnp.maximum((b * nts + t) * hb - 1, 0), cb)),
            pl.BlockSpec((SUBLANES, w3), lambda b, t, l: (jnp.minimum((b * nts + t + 1) * hb, last), cb)),
            pl.BlockSpec((ts, LANES), lambda b, t, l: (b * nts + t, 0)),
            pl.BlockSpec((1, CONV_WIDTH, w3), lambda b, t, l: (l[0], 0, 0)),
            pl.BlockSpec((1, 1, nh2), lambda b, t, l: (l[0], 0, 0)),
            pl.BlockSpec((1, 1, nh2), lambda b, t, l: (l[0], 0, 0)),
        ],
        [osp, osp, pl.BlockSpec((GDN_WIDTH, ts), lambda b, t, l: (0, b * nts + t)), osp, ssp, ssp],
        [jax.ShapeDtypeStruct((m, GDN_WIDTH), BF16)] * 2 + [jax.ShapeDtypeStruct((GDN_WIDTH, m), BF16)]
        + [jax.ShapeDtypeStruct((m, GDN_WIDTH), BF16)] + [jax.ShapeDtypeStruct((m, nh2), F32)] * 2,
        scratch_shapes=[pltpu.VMEM((ts + 2 * SUBLANES, w3), F32)], name="gdn_prep",
    )(layer, proj, proj, proj, small, conv_w, a_log.reshape(depth, 1, nh2), dt_bias.reshape(depth, 1, nh2))


def _chunk_masks(backward):
    r = lax.broadcasted_iota(jnp.int32, (CHUNK, CHUNK), 0)
    c = lax.broadcasted_iota(jnp.int32, (CHUNK, CHUNK), 1)
    if backward:
        return r <= c, r < c
    return r >= c, r > c


def _gdn_l_kernel(l_ref, k_ref, beta_ref, gc_ref, gcrow_ref, lf_ref, lb_ref):
    h = pl.program_id(1)
    n_chunks = k_ref.shape[0] // CHUNK
    nt = (((1,), (1,)), ((), ()))
    lane16 = lax.broadcasted_iota(jnp.int32, (CHUNK, 2 * N_GDN_HEADS), 1)
    sub16 = lax.broadcasted_iota(jnp.int32, (2 * N_GDN_HEADS, CHUNK), 0)

    def body(n, carry):
        rows = pl.ds(pl.multiple_of(n * CHUNK, CHUNK), CHUNK)
        kb = k_ref[rows, :].astype(BF16)
        kk = lax.dot_general(kb, kb, nt, preferred_element_type=F32)
        beta = beta_ref[rows, :]
        gc = gc_ref[rows, :]
        gcr = gcrow_ref[0, n]
        for d, out_ref in ((0, lf_ref), (1, lb_ref)):
            col = d * N_GDN_HEADS + h
            bcol = jnp.sum(jnp.where(lane16 == col, beta, 0.0), axis=1, keepdims=True)
            gcol = jnp.sum(jnp.where(lane16 == col, gc, 0.0), axis=1, keepdims=True)
            grow = jnp.sum(jnp.where(sub16 == col, gcr, 0.0), axis=0, keepdims=True)
            _, strict = _chunk_masks(d == 1)
            dec = jnp.exp(jnp.where(strict, gcol - grow, NEG_BIG))
            out_ref[0, rows, :] = bcol * kk * dec
        return carry

    lax.fori_loop(0, n_chunks, body, 0, unroll=4)


def _gdn_build_l(layer, k, beta, gc, gc_rows, batch, seq):
    nh2 = 2 * N_GDN_HEADS
    n_chunks = seq // CHUNK
    osp = pl.BlockSpec((1, seq, CHUNK), lambda b, h, l: (b * N_GDN_HEADS + h, 0, 0))
    return _layer_call(
        _gdn_l_kernel, (batch, N_GDN_HEADS),
        [
            pl.BlockSpec((seq, LANES), lambda b, h, l: (b, h)),
            pl.BlockSpec((seq, nh2), lambda b, h, l: (b, 0)),
            pl.BlockSpec((seq, nh2), lambda b, h, l: (b, 0)),
            pl.BlockSpec((1, n_chunks, nh2, CHUNK), lambda b, h, l: (b, 0, 0, 0)),
        ],
        [osp, osp],
        [jax.ShapeDtypeStruct((batch * N_GDN_HEADS, seq, CHUNK), F32)] * 2, name="gdn_build_l",
    )(layer, k, beta, gc, gc_rows)


def _tri_inverse_kernel(l_ref, t_ref):
    sub = lax.broadcasted_iota(jnp.int32, (SUBLANES, LANES), 0)
    nblk = CHUNK // SUBLANES
    for i in range(CHUNK):
        live = i // SUBLANES + 1
        accs = [jnp.zeros((SUBLANES, LANES), F32) for _ in range(live)]
        for j in range(i):
            lij = l_ref[i, pl.ds(j, 1), :]
            for cb in range(j // SUBLANES + 1):
                accs[cb] = accs[cb] + lij * t_ref[j, cb * SUBLANES:(cb + 1) * SUBLANES, :]
        for cb in range(nblk):
            if cb < live - 1:
                val = -accs[cb]
            elif cb == live - 1:
                val = jnp.where(sub == i % SUBLANES, 1.0, 0.0) - accs[cb]
            else:
                val = jnp.zeros((SUBLANES, LANES), F32)
            t_ref[i, cb * SUBLANES:(cb + 1) * SUBLANES, :] = val


def _tri_inverse(l_all):
    g = l_all.shape[-1]
    spec = pl.BlockSpec((CHUNK, CHUNK, LANES), lambda i: (0, 0, i))
    return pl.pallas_call(
        _tri_inverse_kernel, grid=(g // LANES,), in_specs=[spec], out_specs=spec,
        out_shape=jax.ShapeDtypeStruct(l_all.shape, F32),
        compiler_params=_cparams(1), name="gdn_tri_inverse",
    )(l_all)


def _gdn_pre_kernel(l_ref, q_ref, k_ref, kt_ref, v_ref, beta_ref, gc_ref, gcrow_ref, gcpair_ref, tf_ref, tb_ref,
                    mqf_ref, mqb_ref, rf_ref, rb_ref, glf_ref, glb_ref, ol_ref):
    h = pl.program_id(1)
    n_local = q_ref.shape[0] // CHUNK
    dk = GDN_HEAD_DIM
    nt = (((1,), (1,)), ((), ()))
    lane16 = lax.broadcasted_iota(jnp.int32, (CHUNK, 2 * N_GDN_HEADS), 1)
    sub16 = lax.broadcasted_iota(jnp.int32, (2 * N_GDN_HEADS, CHUNK), 0)
    sub16p = lax.broadcasted_iota(jnp.int32, (2 * N_GDN_HEADS, LANES), 0)
    lane_half = lax.broadcasted_iota(jnp.int32, (1, LANES), 1) // CHUNK
    outs = ((tf_ref, mqf_ref, rf_ref, glf_ref), (tb_ref, mqb_ref, rb_ref, glb_ref))
    per_pair = LANES // CHUNK
    stage1 = []
    for c in range(n_local):
        rows = slice(c * CHUNK, (c + 1) * CHUNK)
        pair, half = c // per_pair, c % per_pair
        qc = q_ref[rows, :]
        kc = k_ref[rows, :]
        vc = v_ref[rows, :]
        kt_pair = kt_ref[:, pair * LANES:(pair + 1) * LANES]
        qk = lax.dot_general(qc.astype(BF16), kc.astype(BF16), nt, preferred_element_type=F32)
        for d in (0, 1):
            col = d * N_GDN_HEADS + h
            bcol = jnp.sum(jnp.where(lane16 == col, beta_ref[rows, :], 0.0), axis=1, keepdims=True)
            gcol = jnp.sum(jnp.where(lane16 == col, gc_ref[rows, :], 0.0), axis=1, keepdims=True)
            grow = jnp.sum(jnp.where(sub16 == col, gcrow_ref[0, c], 0.0), axis=0, keepdims=True)
            glast = gcol[0:1, :] if d == 1 else gcol[CHUNK - 1:CHUNK, :]
            incl, _ = _chunk_masks(d == 1)
            attn = (qk * jnp.exp(jnp.where(incl, gcol - grow, NEG_BIG))).astype(BF16)
            gam = jnp.exp(gcol)
            x = jnp.concatenate([bcol * vc, (bcol * gam) * kc], axis=1).astype(BF16)
            t = outs[d][0][0, rows, :]
            th = t.astype(BF16)
            tl = (t - th.astype(F32)).astype(BF16)
            uw = (jnp.dot(th, x, preferred_element_type=F32)
                  + jnp.dot(tl, x, preferred_element_type=F32)).astype(BF16)
            grow_pair = jnp.sum(jnp.where(sub16p == col, gcpair_ref[0, pair], 0.0), axis=0, keepdims=True)
            tail = jnp.exp(jnp.where(lane_half == half, glast - grow_pair, NEG_BIG))
            ktil_t = (kt_pair * tail).astype(BF16)
            stage1.append((c, d, attn, uw, ktil_t, gam * qc, jnp.exp(glast)))
    o_local = {}
    for c, d, attn, uw, ktil_t, gq, gl in stage1:
        _, mq_ref, r_ref, gl_ref = outs[d]
        pos = (n_local - 1 - c) if d == 1 else c
        awu = jnp.dot(attn, uw, preferred_element_type=F32)
        kwu = jnp.dot(ktil_t, jnp.concatenate([uw] * per_pair, axis=0), preferred_element_type=F32)
        mq_ref[0, pos, 0:dk, :] = kwu[:, dk:].astype(BF16)
        mq_ref[0, pos, dk:dk + CHUNK, :] = (gq - awu[:, dk:]).astype(BF16)
        r_ref[0, pos] = kwu[:, :dk].astype(BF16)
        gl_ref[0, pos] = jnp.broadcast_to(gl, (1, LANES))
        o_local[c] = o_local[c] + awu[:, :dk] if c in o_local else awu[:, :dk]
    for c, val in o_local.items():
        ol_ref[c * CHUNK:(c + 1) * CHUNK, :] = val


def _gdn_pre(layer, q, k, k_t, v, beta, gc, gc_rows, gc_pairs, t_f, t_b, batch, seq):
    nh2 = 2 * N_GDN_HEADS
    n_chunks = seq // CHUNK
    cg = min(GDN_PRE_CHUNKS, n_chunks)
    ng = n_chunks // cg
    rows = cg * CHUNK
    dk = GDN_HEAD_DIM
    bh = batch * N_GDN_HEADS
    hsp = pl.BlockSpec((rows, LANES), lambda b, h, g, l: (b * ng + g, h))
    ssp = pl.BlockSpec((rows, nh2), lambda b, h, g, l: (b * ng + g, 0))
    tsp = pl.BlockSpec((1, rows, CHUNK), lambda b, h, g, l: (b * N_GDN_HEADS + h, g, 0))

    def osp(r, mirrored):
        if mirrored:
            return pl.BlockSpec((1, cg, r, LANES), lambda b, h, g, l: (b * N_GDN_HEADS + h, ng - 1 - g, 0, 0))
        return pl.BlockSpec((1, cg, r, LANES), lambda b, h, g, l: (b * N_GDN_HEADS + h, g, 0, 0))

    shapes = [((dk + CHUNK), BF16), (dk, BF16), (1, F32)]
    out_specs, out_shape = [], []
    for r, dt in shapes:
        for mirrored in (False, True):
            out_specs.append(osp(r, mirrored))
            out_shape.append(jax.ShapeDtypeStruct((bh, n_chunks, r, LANES), dt))
    return _layer_call(
        _gdn_pre_kernel, (batch, N_GDN_HEADS, ng),
        [hsp, hsp, pl.BlockSpec((LANES, rows), lambda b, h, g, l: (h, b * ng + g)), hsp, ssp, ssp,
         pl.BlockSpec((1, cg, nh2, CHUNK), lambda b, h, g, l: (b, g, 0, 0)),
         pl.BlockSpec((1, rows // LANES, nh2, LANES), lambda b, h, g, l: (b, g, 0, 0)),
         tsp, tsp],
        out_specs + [hsp], out_shape + [jax.ShapeDtypeStruct(q.shape, F32)], name="gdn_chunk_pre",
    )(layer, q, k, k_t, v, beta, gc, gc_rows, gc_pairs, t_f, t_b)


def _gdn_state_kernel(l_ref, mqf_ref, mqb_ref, rf_ref, rb_ref, glf_ref, glb_ref, of_ref, ob_ref, st_ref):
    g = pl.program_id(1)
    n_heads, n_local = mqf_ref.shape[0], mqf_ref.shape[1]
    dk = GDN_HEAD_DIM

    @pl.when(g == 0)
    def _():
        st_ref[...] = jnp.zeros_like(st_ref)

    zero = jnp.zeros((dk, dk), BF16)
    states = [(st_ref[2 * hh], st_ref[2 * hh + 1]) for hh in range(n_heads)]
    for c in range(n_local):
        for hh in range(n_heads):
            sf, sb = states[hh]
            mq = jnp.concatenate([mqf_ref[hh, c], mqb_ref[hh, c]], axis=1)
            bd = jnp.concatenate([jnp.concatenate([sf.astype(BF16), zero], axis=1),
                                  jnp.concatenate([zero, sb.astype(BF16)], axis=1)], axis=0)
            res = jnp.dot(mq, bd, preferred_element_type=F32)
            hl = slice(hh * LANES, (hh + 1) * LANES)
            of_ref[c * CHUNK:(c + 1) * CHUNK, hl] = res[dk:, :dk]
            cb = n_local - 1 - c
            ob_ref[cb * CHUNK:(cb + 1) * CHUNK, hl] = res[dk:, dk:]
            states[hh] = (glf_ref[hh, c] * sf - res[:dk, :dk] + rf_ref[hh, c].astype(F32),
                          glb_ref[hh, c] * sb - res[:dk, dk:] + rb_ref[hh, c].astype(F32))
    for hh in range(n_heads):
        st_ref[2 * hh] = states[hh][0]
        st_ref[2 * hh + 1] = states[hh][1]


def _gdn_state(layer, pre, batch, seq):
    n_chunks = seq // CHUNK
    cg = min(GDN_SCAN_CHUNKS, n_chunks)
    ng = n_chunks // cg
    nh = N_GDN_HEADS
    dk = GDN_HEAD_DIM
    m = batch * seq
    in_specs = [pl.BlockSpec((nh, cg) + a.shape[2:], lambda b, g, l: (b, g, 0, 0)) for a in pre]
    return _layer_call(
        _gdn_state_kernel, (batch, ng), in_specs,
        [pl.BlockSpec((cg * CHUNK, GDN_WIDTH), lambda b, g, l: (b * ng + g, 0)),
         pl.BlockSpec((cg * CHUNK, GDN_WIDTH), lambda b, g, l: (b * ng + ng - 1 - g, 0))],
        [jax.ShapeDtypeStruct((m, GDN_WIDTH), F32)] * 2,
        scratch_shapes=[pltpu.VMEM((2 * nh, dk, dk), F32)], name="gdn_state_scan",
    )(layer, *pre)


def _gdn_out_kernel(l_ref, of_ref, ob_ref, ol_ref, z_ref, nw_ref, y_ref):
    for h in range(N_GDN_HEADS):
        hl = slice(h * LANES, (h + 1) * LANES)
        o = of_ref[:, hl] + ob_ref[:, hl] + ol_ref[:, hl]
        ms = jnp.mean(o * o, axis=-1, keepdims=True)
        y_ref[:, hl] = (o * lax.rsqrt(ms + NORM_EPS) * nw_ref[0] * _silu(z_ref[:, hl])).astype(BF16)


def _gdn_out(layer, o_f, o_b, o_local, proj, norm_w):
    m = o_f.shape[0]
    tm = min(512, m)
    depth = norm_w.shape[0]
    zcb = (3 * DIFF_WIDTH + 3 * GDN_WIDTH) // GDN_WIDTH
    osp = pl.BlockSpec((tm, GDN_WIDTH), lambda i, l: (i, 0))
    return _layer_call(
        _gdn_out_kernel, (m // tm,),
        [osp, osp, osp, pl.BlockSpec((tm, GDN_WIDTH), lambda i, l: (i, zcb)),
         pl.BlockSpec((1, 1, LANES), lambda i, l: (l[0], 0, 0))],
        osp, jax.ShapeDtypeStruct((m, GDN_WIDTH), BF16), name="gdn_out",
    )(layer, o_f, o_b, o_local, proj, norm_w.reshape(depth, 1, LANES))


def _gated_deltanet(layer, proj, small, conv_w, a_log, dt_bias, norm_w, batch, seq):
    q, k, k_t, v, beta, gc = _gdn_prep(layer, proj, small, conv_w, a_log, dt_bias, batch, seq)
    n_chunks = seq // CHUNK
    nh2 = 2 * N_GDN_HEADS
    gc_rows = gc.reshape(batch, n_chunks, CHUNK, nh2).transpose(0, 1, 3, 2)
    gc_pairs = gc.reshape(batch, seq // LANES, LANES, nh2).transpose(0, 1, 3, 2)
    l_f, l_b = _gdn_build_l(layer, k, beta, gc, gc_rows, batch, seq)
    g0 = batch * N_GDN_HEADS * n_chunks
    t_f = _tri_inverse(l_f.reshape(g0, CHUNK, CHUNK).transpose(1, 2, 0))
    t_b = _tri_inverse(l_b.reshape(g0, CHUNK, CHUNK).transpose(2, 1, 0))
    t_f = t_f.transpose(2, 0, 1).reshape(batch * N_GDN_HEADS, seq, CHUNK)
    t_b = t_b.transpose(2, 1, 0).reshape(batch * N_GDN_HEADS, seq, CHUNK)
    *pre, o_local = _gdn_pre(layer, q, k, k_t, v, beta, gc, gc_rows, gc_pairs, t_f, t_b, batch, seq)
    o_f, o_b = _gdn_state(layer, pre, batch, seq)
    return _gdn_out(layer, o_f, o_b, o_local, proj, norm_w)


def _layer(l, x, h, cos, sin, mod, lam_inits, p, batch, seq):
    m, d = x.shape
    layer = jnp.reshape(l, (1,)).astype(jnp.int32)
    tm = min(1024, seq)
    ident = lambda accs, extras: accs

    w_t = p["w_in_t"]
    (proj,) = _matmul(layer, [h], [w_t], [0], [0], [], ident, [F32], MAIN_COLS, tm, 1024, "proj_main",
                      w_transposed=True)
    (small,) = _matmul(layer, [h], [w_t], [0], [MAIN_COLS], [], ident, [F32], LANES, tm, LANES, "proj_small",
                       w_transposed=True)
    (gates,) = _matmul(layer, [h], [w_t], [0], [MAIN_COLS + SMALL_COLS], [],
                       lambda accs, extras: [jax.nn.sigmoid(accs[0])], [BF16], 2 * d, tm, 1024, "proj_gates",
                       w_transposed=True)

    lam_init = lam_inits[l]
    lv = p["diff_lambda"][l].astype(F32)
    lam = jnp.exp(jnp.sum(lv[0] * lv[1])) - jnp.exp(jnp.sum(lv[2] * lv[3])) + lam_init
    scalars = jnp.stack([lam, 1.0 - lam_init]).astype(F32)
    dq, dk, dv = _qk_prep(layer, proj, cos, sin, p["diff_qn_w"], p["diff_kn_w"])
    y_diff = _diff_attention(layer, scalars, dq, dk, dv, p["diff_subln_w"], batch, seq)

    y_gdn = _gated_deltanet(layer, proj, small, p["gdn_conv_w"], p["gdn_a_log"], p["gdn_dt_bias"],
                            p["gdn_norm_w"], batch, seq)

    x, h = _mixer_out(layer, y_diff, y_gdn, gates, p["w_branch_diff"], p["w_branch_gdn"], p["w_out"], x, mod,
                      p["norm_ffn_w"], seq)
    f = p["ffn_w_down"].shape[1]
    tf = 512
    (act,) = _matmul(layer, [h], [p["ffn_w_up"], p["ffn_w_up"]], [0, 0], [0, f // tf], [],
                     lambda accs, extras: [_silu(accs[0]) * accs[1]], [BF16], f, tm, tf, "ffn_up")
    return tuple(_matmul_residual_norm(layer, act, p["ffn_w_down"], x, mod, 5, p["norm_mix_w"], 1, 0, True, seq,
                                       min(256, seq), "ffn_down"))


def kernel(x, c, positions, ada_w, ada_b, norm_mix_w, norm_ffn_w, w_in, diff_qn_w, diff_kn_w, diff_lambda,
           diff_subln_w, gdn_conv_w, gdn_a_log, gdn_dt_bias, gdn_norm_w, w_branch_diff, w_branch_gdn, w_out,
           ffn_w_up, ffn_w_down):
    batch, seq, d = x.shape
    depth = ada_w.shape[0]
    mod = _ada_modulation(c, ada_w, ada_b)
    cos, sin = _rope_tables(positions)
    lam_inits = jnp.asarray([0.8 - 0.6 * math.exp(-0.3 * i) for i in range(depth)], F32)
    w_in_t = jnp.swapaxes(w_in, 1, 2)
    p = {
        "norm_mix_w": norm_mix_w.reshape(depth, 1, d), "norm_ffn_w": norm_ffn_w.reshape(depth, 1, d),
        "w_in_t": w_in_t,
        "diff_qn_w": diff_qn_w, "diff_kn_w": diff_kn_w, "diff_lambda": diff_lambda, "diff_subln_w": diff_subln_w,
        "gdn_conv_w": gdn_conv_w, "gdn_a_log": gdn_a_log, "gdn_dt_bias": gdn_dt_bias, "gdn_norm_w": gdn_norm_w,
        "w_branch_diff": w_branch_diff.astype(BF16), "w_branch_gdn": w_branch_gdn.astype(BF16),
        "w_out": w_out.astype(BF16), "ffn_w_up": ffn_w_up, "ffn_w_down": ffn_w_down.astype(BF16),
    }
    x0 = x.reshape(batch * seq, d)
    h0 = _norm_mod(jnp.zeros((1,), jnp.int32), x0, p["norm_mix_w"], mod, 1, 0, seq)
    carry = (x0, h0)
    for l in range(depth):
        carry = _layer(jnp.int32(l), carry[0], carry[1], cos, sin, mod, lam_inits, p, batch, seq)
    return carry[0].reshape(batch, seq, d)
```

```python
import functools
import math

import jax
import jax.numpy as jnp
from jax import lax
from jax.experimental import pallas as pl
from jax.experimental.pallas import tpu as pltpu

F32 = jnp.float32
BF16 = jnp.bfloat16

N_DIFF_HEADS = 8
DIFF_HEAD_DIM = 64
DIFF_WIDTH = N_DIFF_HEADS * 2 * DIFF_HEAD_DIM
N_GDN_HEADS = 8
GDN_HEAD_DIM = 128
GDN_WIDTH = N_GDN_HEADS * GDN_HEAD_DIM
CONV_WIDTH = 5
CHUNK = 64
ROPE_THETA = 10000.0
NORM_EPS = 1e-6
N_MOD = 6
LANES = 128
SUBLANES = 8
NEG_BIG = -1e30
GDN_PRE_CHUNKS = 32
GDN_SCAN_CHUNKS = 8
ATTN_Q_ROWS = 2048
ATTN_SUB_ROWS = 256

MAIN_COLS = 3 * DIFF_WIDTH + 4 * GDN_WIDTH
SMALL_COLS = 4 * N_GDN_HEADS
VMEM_LIMIT = 48 * 1024 * 1024


def _silu(x):
    return x * jax.nn.sigmoid(x)


def _cparams(n_axes, vmem=VMEM_LIMIT):
    return pltpu.CompilerParams(dimension_semantics=("arbitrary",) * n_axes, vmem_limit_bytes=vmem)


def _layer_call(kernel, grid, in_specs, out_specs, out_shape, scratch_shapes=(), name=None):
    return pl.pallas_call(
        kernel,
        grid_spec=pltpu.PrefetchScalarGridSpec(
            num_scalar_prefetch=1, grid=grid, in_specs=in_specs, out_specs=out_specs,
            scratch_shapes=scratch_shapes),
        out_shape=out_shape,
        compiler_params=_cparams(len(grid)),
        name=name,
    )


def _ada_kernel(c_ref, w_ref, b_ref, o_ref):
    @pl.when(pl.program_id(1) == 0)
    def _():
        o_ref[0] = jnp.broadcast_to(b_ref[0], o_ref.shape[1:])

    a = _silu(c_ref[...]).astype(BF16)
    o_ref[0] += jnp.dot(a, w_ref[0].astype(BF16), preferred_element_type=F32)


def _ada_modulation(c, ada_w, ada_b):
    depth, d, n6 = ada_w.shape
    b = c.shape[0]
    rows = -(-b // SUBLANES) * SUBLANES
    c_pad = jnp.pad(c, ((0, rows - b), (0, 0)))
    tk = 128
    out = pl.pallas_call(
        _ada_kernel,
        grid=(depth, d // tk),
        in_specs=[
            pl.BlockSpec((rows, tk), lambda l, k: (0, k)),
            pl.BlockSpec((1, tk, n6), lambda l, k: (l, k, 0)),
            pl.BlockSpec((1, 1, n6), lambda l, k: (l, 0, 0)),
        ],
        out_specs=pl.BlockSpec((1, rows, n6), lambda l, k: (l, 0, 0)),
        out_shape=jax.ShapeDtypeStruct((depth, rows, n6), F32),
        compiler_params=_cparams(2),
        name="ada_modulation",
    )(c_pad, ada_w, ada_b.reshape(depth, 1, n6))
    return out[:, :b].reshape(depth, b, N_MOD, d).transpose(0, 2, 1, 3).reshape(depth, N_MOD, b, 1, d)


def _rope_kernel(ang_ref, cos_ref, sin_ref):
    ang = ang_ref[...]
    lane = lax.broadcasted_iota(jnp.int32, ang.shape, 1)
    first = (lane % DIFF_HEAD_DIM) < (DIFF_HEAD_DIM // 2)
    cos_ref[...] = jnp.cos(ang)
    s = jnp.sin(ang)
    sin_ref[...] = jnp.where(first, -s, s)


def _rope_tables(positions):
    m = positions.size
    half = DIFF_HEAD_DIM // 2
    inv_freq = ROPE_THETA ** (-jnp.arange(half, dtype=F32) * 2.0 / DIFF_HEAD_DIM)
    ang = positions.reshape(m, 1).astype(F32) * jnp.tile(inv_freq, LANES // half)[None, :]
    tm = min(1024, m)
    spec = pl.BlockSpec((tm, LANES), lambda i: (i, 0))
    return pl.pallas_call(
        _rope_kernel, grid=(m // tm,), in_specs=[spec], out_specs=[spec, spec],
        out_shape=[jax.ShapeDtypeStruct((m, LANES), F32)] * 2,
        compiler_params=_cparams(1), name="rope_tables",
    )(ang)


def _norm_mod_kernel(l_ref, x_ref, w_ref, sc_ref, sh_ref, o_ref):
    x = x_ref[...]
    ms = jnp.mean(x * x, axis=-1, keepdims=True)
    y = x * lax.rsqrt(ms + NORM_EPS) * w_ref[0]
    o_ref[...] = (y * (1.0 + sc_ref[0, 0, 0]) + sh_ref[0, 0, 0]).astype(BF16)


def _norm_mod(layer, x, norm_w, mod, scale_idx, shift_idx, seq):
    m, d = x.shape
    tm = min(512, seq)
    return _layer_call(
        _norm_mod_kernel, (m // tm,),
        [
            pl.BlockSpec((tm, d), lambda i, l: (i, 0)),
            pl.BlockSpec((1, 1, d), lambda i, l: (l[0], 0, 0)),
            pl.BlockSpec((1, 1, 1, 1, d), lambda i, l: (l[0], scale_idx, i * tm // seq, 0, 0)),
            pl.BlockSpec((1, 1, 1, 1, d), lambda i, l: (l[0], shift_idx, i * tm // seq, 0, 0)),
        ],
        pl.BlockSpec((tm, d), lambda i, l: (i, 0)),
        jax.ShapeDtypeStruct((m, d), BF16), name="norm_mod",
    )(layer, x, norm_w, mod, mod)


def _matmul(layer, xs, ws, w_x, w_off, extras, epilogue, out_dtypes, n, tm, tn, name, w_transposed=False):
    m = xs[0].shape[0]
    nx, nw, ne = len(xs), len(ws), len(extras)
    cast_w = ws[0].dtype == F32
    assert all((w.dtype == F32) == cast_w for w in ws)
    ij = (lambda a, b: (b, a)) if cast_w else (lambda a, b: (a, b))
    in_specs = [pl.BlockSpec((tm, x.shape[1]), lambda a, b, l: (ij(a, b)[0], 0)) for x in xs]
    for w, off in zip(ws, w_off):
        if w_transposed:
            if off % tn:
                in_specs.append(pl.BlockSpec((pl.Element(1), pl.Element(tn), pl.Element(w.shape[2])),
                                             lambda a, b, l, off=off: (
                                                 l[0], pl.multiple_of(ij(a, b)[1] * tn + off, math.gcd(off, tn)), 0)))
                continue
            off = off // tn
            in_specs.append(pl.BlockSpec((1, tn, w.shape[2]),
                                         lambda a, b, l, off=off: (l[0], ij(a, b)[1] + off, 0)))
        else:
            in_specs.append(pl.BlockSpec((1, w.shape[1], tn),
                                         lambda a, b, l, off=off: (l[0], 0, ij(a, b)[1] + off)))
    for _, bs, imap in extras:
        in_specs.append(pl.BlockSpec(bs, lambda a, b, l, imap=imap: imap(*ij(a, b), l)))
    out_specs = [pl.BlockSpec((tm, tn), lambda a, b, l: ij(a, b)) for _ in out_dtypes]
    out_shape = [jax.ShapeDtypeStruct((m, n), dt) for dt in out_dtypes]
    scratch = []
    if cast_w:
        scratch = [pltpu.VMEM((tn, w.shape[2]) if w_transposed else (w.shape[1], tn), BF16) for w in ws]
    dims = (((1,), (1,)), ((), ())) if w_transposed else (((1,), (0,)), ((), ()))

    def kern(l_ref, *refs):
        x_refs, w_refs = refs[:nx], refs[nx:nx + nw]
        e_refs, o_refs = refs[nx + nw:nx + nw + ne], refs[nx + nw + ne:nx + nw + ne + len(out_dtypes)]
        if cast_w:
            wb_refs = refs[nx + nw + ne + len(out_dtypes):]

            @pl.when(pl.program_id(1) == 0)
            def _():
                for w_ref, wb_ref in zip(w_refs, wb_refs):
                    wb_ref[...] = w_ref[0].astype(BF16)

            w_tiles = [wb_ref[...] for wb_ref in wb_refs]
        else:
            w_tiles = [w_ref[0] for w_ref in w_refs]
        accs = [lax.dot_general(x_refs[xi][...], w, dims, preferred_element_type=F32)
                for xi, w in zip(w_x, w_tiles)]
        outs = epilogue(accs, [e[...] for e in e_refs])
        for o_ref, v in zip(o_refs, outs):
            o_ref[...] = v.astype(o_ref.dtype)

    grid = (n // tn, m // tm) if cast_w else (m // tm, n // tn)
    return _layer_call(kern, grid, in_specs, out_specs, out_shape, scratch_shapes=scratch, name=name)(
        layer, *xs, *ws, *[e[0] for e in extras])


def _residual_norm_kernel(l_ref, a_ref, w_ref, x_ref, gate_ref, nw_ref, sc_ref, sh_ref, xo_ref, ho_ref):
    acc = jnp.dot(a_ref[...], w_ref[0], preferred_element_type=F32)
    xn = x_ref[...] + gate_ref[0, 0, 0] * acc
    xo_ref[...] = xn
    ms = jnp.mean(xn * xn, axis=-1, keepdims=True)
    y = xn * lax.rsqrt(ms + NORM_EPS) * nw_ref[0]
    ho_ref[...] = (y * (1.0 + sc_ref[0, 0, 0]) + sh_ref[0, 0, 0]).astype(BF16)


def _matmul_residual_norm(layer, a, w, x, mod, gate_idx, norm_w, scale_idx, shift_idx, next_layer, seq, tm, name):
    m, k = a.shape
    depth, _, n = w.shape
    nl = (lambda l: jnp.minimum(l[0] + 1, depth - 1)) if next_layer else (lambda l: l[0])
    row = lambda i: i * tm // seq
    return _layer_call(
        _residual_norm_kernel, (m // tm,),
        [
            pl.BlockSpec((tm, k), lambda i, l: (i, 0)),
            pl.BlockSpec((1, k, n), lambda i, l: (l[0], 0, 0), pipeline_mode=pl.Buffered(1)),
            pl.BlockSpec((tm, n), lambda i, l: (i, 0)),
            pl.BlockSpec((1, 1, 1, 1, n), lambda i, l: (l[0], gate_idx, row(i), 0, 0)),
            pl.BlockSpec((1, 1, n), lambda i, l: (nl(l), 0, 0)),
            pl.BlockSpec((1, 1, 1, 1, n), lambda i, l: (nl(l), scale_idx, row(i), 0, 0)),
            pl.BlockSpec((1, 1, 1, 1, n), lambda i, l: (nl(l), shift_idx, row(i), 0, 0)),
        ],
        [pl.BlockSpec((tm, n), lambda i, l: (i, 0))] * 2,
        [jax.ShapeDtypeStruct((m, n), F32), jax.ShapeDtypeStruct((m, n), BF16)], name=name,
    )(layer, a, w, x, mod, norm_w, mod, mod)


def _mixer_out_kernel(l_ref, yd_ref, yg_ref, gd_ref, gg_ref, wd_ref, wg_ref, wo_ref, x_ref, gate_ref, nw_ref,
                      sc_ref, sh_ref, xo_ref, ho_ref):
    merged = (gd_ref[...] * jnp.dot(yd_ref[...], wd_ref[0], preferred_element_type=F32)
              + gg_ref[...] * jnp.dot(yg_ref[...], wg_ref[0], preferred_element_type=F32)).astype(BF16)
    acc = jnp.dot(merged, wo_ref[0], preferred_element_type=F32)
    xn = x_ref[...] + gate_ref[0, 0, 0] * acc
    xo_ref[...] = xn
    ms = jnp.mean(xn * xn, axis=-1, keepdims=True)
    y = xn * lax.rsqrt(ms + NORM_EPS) * nw_ref[0]
    ho_ref[...] = (y * (1.0 + sc_ref[0, 0, 0]) + sh_ref[0, 0, 0]).astype(BF16)


def _mixer_out(layer, y_diff, y_gdn, gates, w_d, w_g, w_o, x, mod, norm_w, seq):
    m, d = x.shape
    tm = min(256, seq)
    kd, kg = y_diff.shape[1], y_gdn.shape[1]
    row = lambda i: i * tm // seq
    resident = lambda k, n: pl.BlockSpec((1, k, n), lambda i, l: (l[0], 0, 0), pipeline_mode=pl.Buffered(1))
    modsp = lambda idx: pl.BlockSpec((1, 1, 1, 1, d), lambda i, l: (l[0], idx, row(i), 0, 0))
    return _layer_call(
        _mixer_out_kernel, (m // tm,),
        [
            pl.BlockSpec((tm, kd), lambda i, l: (i, 0)),
            pl.BlockSpec((tm, kg), lambda i, l: (i, 0)),
            pl.BlockSpec((tm, d), lambda i, l: (i, 0)),
            pl.BlockSpec((tm, d), lambda i, l: (i, 1)),
            resident(kd, d), resident(kg, d), resident(d, d),
            pl.BlockSpec((tm, d), lambda i, l: (i, 0)),
            modsp(2),
            pl.BlockSpec((1, 1, d), lambda i, l: (l[0], 0, 0)),
            modsp(4), modsp(3),
        ],
        [pl.BlockSpec((tm, d), lambda i, l: (i, 0))] * 2,
        [jax.ShapeDtypeStruct((m, d), F32), jax.ShapeDtypeStruct((m, d), BF16)], name="mixer_out",
    )(layer, y_diff, y_gdn, gates, gates, w_d, w_g, w_o, x, mod, norm_w, mod, mod)


def _group_sumsq(x, group_ones):
    sq = x * x
    hi = sq.astype(BF16)
    lo = (sq - hi.astype(F32)).astype(BF16)
    return (jnp.dot(hi, group_ones, preferred_element_type=F32)
            + jnp.dot(lo, group_ones, preferred_element_type=F32))


def _qk_prep_kernel(l_ref, q_ref, k_ref, v_ref, cos_ref, sin_ref, qw_ref, kw_ref, qo_ref, ko_ref, vo_ref):
    cos = cos_ref[...]
    sin = sin_ref[...]
    row = lax.broadcasted_iota(jnp.int32, (LANES, LANES), 0) // DIFF_HEAD_DIM
    col = lax.broadcasted_iota(jnp.int32, (LANES, LANES), 1) // DIFF_HEAD_DIM
    ones = (row == col).astype(BF16)
    lane = lax.broadcasted_iota(jnp.int32, cos.shape, 1)
    first = (lane % DIFF_HEAD_DIM) < (DIFF_HEAD_DIM // 2)
    half = DIFF_HEAD_DIM // 2

    def norm_rope(x, w, scale):
        ms = _group_sumsq(x, ones) * (1.0 / DIFF_HEAD_DIM)
        y = x * lax.rsqrt(ms + NORM_EPS) * w
        partner = jnp.where(first, pltpu.roll(y, LANES - half, 1), pltpu.roll(y, half, 1))
        return (y * cos + partner * sin) * scale

    q_scale = DIFF_HEAD_DIM ** -0.5 * math.log2(math.e)
    ones_blk = jnp.ones((cos.shape[0], LANES), BF16)
    for h in range(N_DIFF_HEADS):
        sl = slice(h * LANES, (h + 1) * LANES)
        qo_ref[:, sl] = norm_rope(q_ref[:, sl], qw_ref[0, :, sl], q_scale).astype(BF16)
        ko_ref[:, sl] = norm_rope(k_ref[:, sl], kw_ref[0, :, sl], 1.0).astype(BF16)
        vo_ref[:, 2 * h * LANES:(2 * h + 1) * LANES] = v_ref[:, sl].astype(BF16)
        vo_ref[:, (2 * h + 1) * LANES:(2 * h + 2) * LANES] = ones_blk


def _qk_prep(layer, proj, cos, sin, qn_w, kn_w):
    m = proj.shape[0]
    tm = min(512, m)
    w = DIFF_WIDTH
    rep = w // DIFF_HEAD_DIM
    depth = qn_w.shape[0]
    qw = jnp.tile(qn_w, (1, rep)).reshape(depth, 1, w)
    kw = jnp.tile(kn_w, (1, rep)).reshape(depth, 1, w)
    blk = lambda c: pl.BlockSpec((tm, w), lambda i, l, c=c: (i, c))
    tab = pl.BlockSpec((tm, LANES), lambda i, l: (i, 0))
    wsp = pl.BlockSpec((1, 1, w), lambda i, l: (l[0], 0, 0))
    osp = pl.BlockSpec((tm, w), lambda i, l: (i, 0))
    vsp = pl.BlockSpec((tm, 2 * w), lambda i, l: (i, 0))
    return _layer_call(
        _qk_prep_kernel, (m // tm,), [blk(0), blk(1), blk(2), tab, tab, wsp, wsp], [osp, osp, vsp],
        [jax.ShapeDtypeStruct((m, w), BF16)] * 2 + [jax.ShapeDtypeStruct((m, 2 * w), BF16)], name="diff_qk_prep",
    )(layer, proj, proj, proj, cos, sin, qw, kw)


def _diff_attn_kernel(l_ref, sc_ref, q_ref, k_ref, v_ref, w_ref, o_ref):
    lam = sc_ref[0]
    out_scale = sc_ref[1]
    k = k_ref[...]
    nt = (((1,), (1,)), ((), ()))
    sub = min(ATTN_SUB_ROWS, q_ref.shape[0])

    def branch(qm):
        s = lax.dot_general(qm, k, nt, preferred_element_type=F32)
        p = jnp.exp2(s - jnp.max(s, axis=-1, keepdims=True)).astype(BF16)
        ov = jnp.dot(p, v_ref[...], preferred_element_type=F32)
        return ov[:, :LANES] / ov[:, LANES:]

    for r in range(q_ref.shape[0] // sub):
        rows = slice(r * sub, (r + 1) * sub)
        q = q_ref[rows, :]
        lane = lax.broadcasted_iota(jnp.int32, q.shape, 1)
        zero = jnp.zeros_like(q)
        q0 = jnp.where(lane < DIFF_HEAD_DIM, q, zero)
        q1 = jnp.where(lane >= DIFF_HEAD_DIM, q, zero)
        o = branch(q0) - lam * branch(q1)
        ms = jnp.mean(o * o, axis=-1, keepdims=True)
        o_ref[rows, :] = (o * lax.rsqrt(ms + NORM_EPS) * w_ref[0] * out_scale).astype(BF16)


def _diff_attention(layer, scalars, q, k, v, subln_w, batch, seq):
    m = q.shape[0]
    tq = min(ATTN_Q_ROWS, seq)
    nq = seq // tq
    depth = subln_w.shape[0]
    return _layer_call(
        _diff_attn_kernel, (batch, N_DIFF_HEADS, nq),
        [
            pl.BlockSpec(memory_space=pltpu.SMEM),
            pl.BlockSpec((tq, LANES), lambda b, h, i, l: (b * nq + i, h)),
            pl.BlockSpec((seq, LANES), lambda b, h, i, l: (b, h)),
            pl.BlockSpec((seq, 2 * LANES), lambda b, h, i, l: (b, h)),
            pl.BlockSpec((1, 1, LANES), lambda b, h, i, l: (l[0], 0, 0)),
        ],
        pl.BlockSpec((tq, LANES), lambda b, h, i, l: (b * nq + i, h)),
        jax.ShapeDtypeStruct((m, DIFF_WIDTH), BF16), name="diff_attention",
    )(layer, scalars, q, k, v, subln_w.reshape(depth, 1, LANES))


def _split3(x):
    hi = x.astype(BF16)
    r1 = x - hi.astype(F32)
    mid = r1.astype(BF16)
    lo = (r1 - mid.astype(F32)).astype(BF16)
    return hi, mid, lo


def _gdn_prep_kernel(l_ref, main_ref, prev_ref, next_ref, small_ref, cw_ref, alog_ref, dtb_ref,
                     q_ref, k_ref, kt_ref, v_ref, beta_ref, gc_ref, xe_ref):
    t = pl.program_id(1)
    nt = pl.num_programs(1)
    ts = main_ref.shape[0]
    pad = CONV_WIDTH // 2
    xe_ref[0:SUBLANES, :] = jnp.where(t > 0, prev_ref[...], 0.0)
    xe_ref[SUBLANES:SUBLANES + ts, :] = main_ref[...]
    xe_ref[SUBLANES + ts:, :] = jnp.where(t < nt - 1, next_ref[...], 0.0)
    cw = cw_ref[0]
    first = SUBLANES - pad
    acc = cw[0:1, :] * xe_ref[first:first + ts, :]
    for j in range(1, CONV_WIDTH):
        acc = acc + cw[j:j + 1, :] * xe_ref[first + j:first + j + ts, :]
    y = _silu(acc)
    for h in range(N_GDN_HEADS):
        sl = slice(h * LANES, (h + 1) * LANES)
        qh = y[:, sl]
        kh = y[:, GDN_WIDTH + h * LANES:GDN_WIDTH + (h + 1) * LANES]
        q_ref[:, sl] = (qh * (lax.rsqrt(jnp.sum(qh * qh, axis=-1, keepdims=True) + NORM_EPS)
                              * GDN_HEAD_DIM ** -0.5)).astype(BF16)
        kn = kh * lax.rsqrt(jnp.sum(kh * kh, axis=-1, keepdims=True) + NORM_EPS)
        k_ref[:, sl] = kn.astype(BF16)
        kt_ref[sl, :] = kn.T.astype(BF16)
    v_ref[...] = y[:, 2 * GDN_WIDTH:].astype(BF16)

    nh2 = 2 * N_GDN_HEADS
    small = small_ref[...]
    beta_ref[...] = jax.nn.sigmoid(small[:, :nh2])
    g = -jnp.exp(alog_ref[0]) * jax.nn.softplus(small[:, nh2:2 * nh2] + dtb_ref[0])
    r = lax.broadcasted_iota(jnp.int32, (ts, ts), 0)
    c = lax.broadcasted_iota(jnp.int32, (ts, ts), 1)
    same = (r // CHUNK) == (c // CHUNK)
    pre = (same & (c <= r)).astype(BF16)
    suf = (same & (c >= r)).astype(BF16)
    parts = _split3(g)
    cs_f = sum(jnp.dot(pre, p, preferred_element_type=F32) for p in parts)
    cs_b = sum(jnp.dot(suf, p, preferred_element_type=F32) for p in parts)
    lane = lax.broadcasted_iota(jnp.int32, g.shape, 1)
    gc_ref[...] = jnp.where(lane < N_GDN_HEADS, cs_f, cs_b)


def _gdn_prep(layer, proj, small, conv_w, a_log, dt_bias, batch, seq):
    m = proj.shape[0]
    ts = min(256, seq)
    nts = seq // ts
    w3 = 3 * GDN_WIDTH
    cb = (3 * DIFF_WIDTH) // w3
    assert cb * w3 == 3 * DIFF_WIDTH
    depth = conv_w.shape[0]
    nh2 = 2 * N_GDN_HEADS
    hb = ts // SUBLANES
    last = m // SUBLANES - 1
    osp = pl.BlockSpec((ts, GDN_WIDTH), lambda b, t, l: (b * nts + t, 0))
    ssp = pl.BlockSpec((ts, nh2), lambda b, t, l: (b * nts + t, 0))
    return _layer_call(
        _gdn_prep_kernel, (batch, nts),
        [
            pl.BlockSpec((ts, w3), lambda b, t, l: (b * nts + t, cb)),
            pl.BlockSpec((SUBLANES, w3), lambda b, t, l: (jnp.maximum((b * nts + t) * hb - 1, 0), cb)),
            pl.BlockSpec((SUBLANES, w3), lambda b, t, l: (jnp.minimum((b * nts + t + 1) * hb, last), cb)),
            pl.BlockSpec((ts, LANES), lambda b, t, l: (b * nts + t, 0)),
            pl.BlockSpec((1, CONV_WIDTH, w3), lambda b, t, l: (l[0], 0, 0)),
            pl.BlockSpec((1, 1, nh2), lambda b, t, l: (l[0], 0, 0)),
            pl.BlockSpec((1, 1, nh2), lambda b, t, l: (l[0], 0, 0)),
        ],
        [osp, osp, pl.BlockSpec((GDN_WIDTH, ts), lambda b, t, l: (0, b * nts + t)), osp, ssp, ssp],
        [jax.ShapeDtypeStruct((m, GDN_WIDTH), BF16)] * 2 + [jax.ShapeDtypeStruct((GDN_WIDTH, m), BF16)]
        + [jax.ShapeDtypeStruct((m, GDN_WIDTH), BF16)] + [jax.ShapeDtypeStruct((m, nh2), F32)] * 2,
        scratch_shapes=[pltpu.VMEM((ts + 2 * SUBLANES, w3), F32)], name="gdn_prep",
    )(layer, proj, proj, proj, small, conv_w, a_log.reshape(depth, 1, nh2), dt_bias.reshape(depth, 1, nh2))


def _chunk_masks(backward):
    r = lax.broadcasted_iota(jnp.int32, (CHUNK, CHUNK), 0)
    c = lax.broadcasted_iota(jnp.int32, (CHUNK, CHUNK), 1)
    if backward:
        return r <= c, r < c
    return r >= c, r > c


def _gdn_l_kernel(l_ref, k_ref, beta_ref, gc_ref, gcrow_ref, lf_ref, lb_ref):
    h = pl.program_id(1)
    n_chunks = k_ref.shape[0] // CHUNK
    nt = (((1,), (1,)), ((), ()))
    lane16 = lax.broadcasted_iota(jnp.int32, (CHUNK, 2 * N_GDN_HEADS), 1)
    sub16 = lax.broadcasted_iota(jnp.int32, (2 * N_GDN_HEADS, CHUNK), 0)

    def body(n, carry):
        rows = pl.ds(pl.multiple_of(n * CHUNK, CHUNK), CHUNK)
        kb = k_ref[rows, :].astype(BF16)
        kk = lax.dot_general(kb, kb, nt, preferred_element_type=F32)
        beta = beta_ref[rows, :]
        gc = gc_ref[rows, :]
        gcr = gcrow_ref[0, n]
        for d, out_ref in ((0, lf_ref), (1, lb_ref)):
            col = d * N_GDN_HEADS + h
            bcol = jnp.sum(jnp.where(lane16 == col, beta, 0.0), axis=1, keepdims=True)
            gcol = jnp.sum(jnp.where(lane16 == col, gc, 0.0), axis=1, keepdims=True)
            grow = jnp.sum(jnp.where(sub16 == col, gcr, 0.0), axis=0, keepdims=True)
            _, strict = _chunk_masks(d == 1)
            dec = jnp.exp(jnp.where(strict, gcol - grow, NEG_BIG))
            out_ref[0, rows, :] = bcol * kk * dec
        return carry

    lax.fori_loop(0, n_chunks, body, 0, unroll=4)


def _gdn_build_l(layer, k, beta, gc, gc_rows, batch, seq):
    nh2 = 2 * N_GDN_HEADS
    n_chunks = seq // CHUNK
    osp = pl.BlockSpec((1, seq, CHUNK), lambda b, h, l: (b * N_GDN_HEADS + h, 0, 0))
    return _layer_call(
        _gdn_l_kernel, (batch, N_GDN_HEADS),
        [
            pl.BlockSpec((seq, LANES), lambda b, h, l: (b, h)),
            pl.BlockSpec((seq, nh2), lambda b, h, l: (b, 0)),
            pl.BlockSpec((seq, nh2), lambda b, h, l: (b, 0)),
            pl.BlockSpec((1, n_chunks, nh2, CHUNK), lambda b, h, l: (b, 0, 0, 0)),
        ],
        [osp, osp],
        [jax.ShapeDtypeStruct((batch * N_GDN_HEADS, seq, CHUNK), F32)] * 2, name="gdn_build_l",
    )(layer, k, beta, gc, gc_rows)


def _tri_inverse_kernel(l_ref, t_ref):
    sub = lax.broadcasted_iota(jnp.int32, (SUBLANES, LANES), 0)
    nblk = CHUNK // SUBLANES
    for i in range(CHUNK):
        live = i // SUBLANES + 1
        accs = [jnp.zeros((SUBLANES, LANES), F32) for _ in range(live)]
        for j in range(i):
            lij = l_ref[i, pl.ds(j, 1), :]
            for cb in range(j // SUBLANES + 1):
                accs[cb] = accs[cb] + lij * t_ref[j, cb * SUBLANES:(cb + 1) * SUBLANES, :]
        for cb in range(nblk):
            if cb < live - 1:
                val = -accs[cb]
            elif cb == live - 1:
                val = jnp.where(sub == i % SUBLANES, 1.0, 0.0) - accs[cb]
            else:
                val = jnp.zeros((SUBLANES, LANES), F32)
            t_ref[i, cb * SUBLANES:(cb + 1) * SUBLANES, :] = val


def _tri_inverse(l_all):
    g = l_all.shape[-1]
    spec = pl.BlockSpec((CHUNK, CHUNK, LANES), lambda i: (0, 0, i))
    return pl.pallas_call(
        _tri_inverse_kernel, grid=(g // LANES,), in_specs=[spec], out_specs=spec,
        out_shape=jax.ShapeDtypeStruct(l_all.shape, F32),
        compiler_params=_cparams(1), name="gdn_tri_inverse",
    )(l_all)


def _gdn_pre_kernel(l_ref, q_ref, k_ref, kt_ref, v_ref, beta_ref, gc_ref, gcrow_ref, gcpair_ref, tf_ref, tb_ref,
                    mqf_ref, mqb_ref, rf_ref, rb_ref, glf_ref, glb_ref, ol_ref):
    h = pl.program_id(1)
    n_local = q_ref.shape[0] // CHUNK
    dk = GDN_HEAD_DIM
    nt = (((1,), (1,)), ((), ()))
    lane16 = lax.broadcasted_iota(jnp.int32, (CHUNK, 2 * N_GDN_HEADS), 1)
    sub16 = lax.broadcasted_iota(jnp.int32, (2 * N_GDN_HEADS, CHUNK), 0)
    sub16p = lax.broadcasted_iota(jnp.int32, (2 * N_GDN_HEADS, LANES), 0)
    lane_half = lax.broadcasted_iota(jnp.int32, (1, LANES), 1) // CHUNK
    outs = ((tf_ref, mqf_ref, rf_ref, glf_ref), (tb_ref, mqb_ref, rb_ref, glb_ref))
    per_pair = LANES // CHUNK
    stage1 = []
    for c in range(n_local):
        rows = slice(c * CHUNK, (c + 1) * CHUNK)
        pair, half = c // per_pair, c % per_pair
        qc = q_ref[rows, :]
        kc = k_ref[rows, :]
        vc = v_ref[rows, :]
        kt_pair = kt_ref[:, pair * LANES:(pair + 1) * LANES]
        qk = lax.dot_general(qc.astype(BF16), kc.astype(BF16), nt, preferred_element_type=F32)
        for d in (0, 1):
            col = d * N_GDN_HEADS + h
            bcol = jnp.sum(jnp.where(lane16 == col, beta_ref[rows, :], 0.0), axis=1, keepdims=True)
            gcol = jnp.sum(jnp.where(lane16 == col, gc_ref[rows, :], 0.0), axis=1, keepdims=True)
            grow = jnp.sum(jnp.where(sub16 == col, gcrow_ref[0, c], 0.0), axis=0, keepdims=True)
            glast = gcol[0:1, :] if d == 1 else gcol[CHUNK - 1:CHUNK, :]
            incl, _ = _chunk_masks(d == 1)
            attn = (qk * jnp.exp(jnp.where(incl, gcol - grow, NEG_BIG))).astype(BF16)
            gam = jnp.exp(gcol)
            x = jnp.concatenate([bcol * vc, (bcol * gam) * kc], axis=1).astype(BF16)
            t = outs[d][0][0, rows, :]
            th = t.astype(BF16)
            tl = (t - th.astype(F32)).astype(BF16)
            uw = (jnp.dot(th, x, preferred_element_type=F32)
                  + jnp.dot(tl, x, preferred_element_type=F32)).astype(BF16)
            grow_pair = jnp.sum(jnp.where(sub16p == col, gcpair_ref[0, pair], 0.0), axis=0, keepdims=True)
            tail = jnp.exp(jnp.where(lane_half == half, glast - grow_pair, NEG_BIG))
            ktil_t = (kt_pair * tail).astype(BF16)
            stage1.append((c, d, attn, uw, ktil_t, gam * qc, jnp.exp(glast)))
    o_local = {}
    for c, d, attn, uw, ktil_t, gq, gl in stage1:
        _, mq_ref, r_ref, gl_ref = outs[d]
        pos = (n_local - 1 - c) if d == 1 else c
        awu = jnp.dot(attn, uw, preferred_element_type=F32)
        kwu = jnp.dot(ktil_t, jnp.concatenate([uw] * per_pair, axis=0), preferred_element_type=F32)
        mq_ref[0, pos, 0:dk, :] = kwu[:, dk:].astype(BF16)
        mq_ref[0, pos, dk:dk + CHUNK, :] = (gq - awu[:, dk:]).astype(BF16)
        r_ref[0, pos] = kwu[:, :dk].astype(BF16)
        gl_ref[0, pos] = jnp.broadcast_to(gl, (1, LANES))
        o_local[c] = o_local[c] + awu[:, :dk] if c in o_local else awu[:, :dk]
    for c, val in o_local.items():
        ol_ref[c * CHUNK:(c + 1) * CHUNK, :] = val


def _gdn_pre(layer, q, k, k_t, v, beta, gc, gc_rows, gc_pairs, t_f, t_b, batch, seq):
    nh2 = 2 * N_GDN_HEADS
    n_chunks = seq // CHUNK
    cg = min(GDN_PRE_CHUNKS, n_chunks)
    ng = n_chunks // cg
    rows = cg * CHUNK
    dk = GDN_HEAD_DIM
    bh = batch * N_GDN_HEADS
    hsp = pl.BlockSpec((rows, LANES), lambda b, h, g, l: (b * ng + g, h))
    ssp = pl.BlockSpec((rows, nh2), lambda b, h, g, l: (b * ng + g, 0))
    tsp = pl.BlockSpec((1, rows, CHUNK), lambda b, h, g, l: (b * N_GDN_HEADS + h, g, 0))

    def osp(r, mirrored):
        if mirrored:
            return pl.BlockSpec((1, cg, r, LANES), lambda b, h, g, l: (b * N_GDN_HEADS + h, ng - 1 - g, 0, 0))
        return pl.BlockSpec((1, cg, r, LANES), lambda b, h, g, l: (b * N_GDN_HEADS + h, g, 0, 0))

    shapes = [((dk + CHUNK), BF16), (dk, BF16), (1, F32)]
    out_specs, out_shape = [], []
    for r, dt in shapes:
        for mirrored in (False, True):
            out_specs.append(osp(r, mirrored))
            out_shape.append(jax.ShapeDtypeStruct((bh, n_chunks, r, LANES), dt))
    return _layer_call(
        _gdn_pre_kernel, (batch, N_GDN_HEADS, ng),
        [hsp, hsp, pl.BlockSpec((LANES, rows), lambda b, h, g, l: (h, b * ng + g)), hsp, ssp, ssp,
         pl.BlockSpec((1, cg, nh2, CHUNK), lambda b, h, g, l: (b, g, 0, 0)),
         pl.BlockSpec((1, rows // LANES, nh2, LANES), lambda b, h, g, l: (b, g, 0, 0)),
         tsp, tsp],
        out_specs + [hsp], out_shape + [jax.ShapeDtypeStruct(q.shape, F32)], name="gdn_chunk_pre",
    )(layer, q, k, k_t, v, beta, gc, gc_rows, gc_pairs, t_f, t_b)


def _gdn_state_kernel(l_ref, mqf_ref, mqb_ref, rf_ref, rb_ref, glf_ref, glb_ref, of_ref, ob_ref, st_ref):
    g = pl.program_id(1)
    n_heads, n_local = mqf_ref.shape[0], mqf_ref.shape[1]
    dk = GDN_HEAD_DIM

    @pl.when(g == 0)
    def _():
        st_ref[...] = jnp.zeros_like(st_ref)

    zero = jnp.zeros((dk, dk), BF16)
    states = [(st_ref[2 * hh], st_ref[2 * hh + 1]) for hh in range(n_heads)]
    for c in range(n_local):
        for hh in range(n_heads):
            sf, sb = states[hh]
            mq = jnp.concatenate([mqf_ref[hh, c], mqb_ref[hh, c]], axis=1)
            bd = jnp.concatenate([jnp.concatenate([sf.astype(BF16), zero], axis=1),
                                  jnp.concatenate([zero, sb.astype(BF16)], axis=1)], axis=0)
            res = jnp.dot(mq, bd, preferred_element_type=F32)
            hl = slice(hh * LANES, (hh + 1) * LANES)
            of_ref[c * CHUNK:(c + 1) * CHUNK, hl] = res[dk:, :dk]
            cb = n_local - 1 - c
            ob_ref[cb * CHUNK:(cb + 1) * CHUNK, hl] = res[dk:, dk:]
            states[hh] = (glf_ref[hh, c] * sf - res[:dk, :dk] + rf_ref[hh, c].astype(F32),
                          glb_ref[hh, c] * sb - res[:dk, dk:] + rb_ref[hh, c].astype(F32))
    for hh in range(n_heads):
        st_ref[2 * hh] = states[hh][0]
        st_ref[2 * hh + 1] = states[hh][1]


def _gdn_state(layer, pre, batch, seq):
    n_chunks = seq // CHUNK
    cg = min(GDN_SCAN_CHUNKS, n_chunks)
    ng = n_chunks // cg
    nh = N_GDN_HEADS
    dk = GDN_HEAD_DIM
    m = batch * seq
    in_specs = [pl.BlockSpec((nh, cg) + a.shape[2:], lambda b, g, l: (b, g, 0, 0)) for a in pre]
    return _layer_call(
        _gdn_state_kernel, (batch, ng), in_specs,
        [pl.BlockSpec((cg * CHUNK, GDN_WIDTH), lambda b, g, l: (b * ng + g, 0)),
         pl.BlockSpec((cg * CHUNK, GDN_WIDTH), lambda b, g, l: (b * ng + ng - 1 - g, 0))],
        [jax.ShapeDtypeStruct((m, GDN_WIDTH), F32)] * 2,
        scratch_shapes=[pltpu.VMEM((2 * nh, dk, dk), F32)], name="gdn_state_scan",
    )(layer, *pre)


def _gdn_out_kernel(l_ref, of_ref, ob_ref, ol_ref, z_ref, nw_ref, y_ref):
    for h in range(N_GDN_HEADS):
        hl = slice(h * LANES, (h + 1) * LANES)
        o = of_ref[:, hl] + ob_ref[:, hl] + ol_ref[:, hl]
        ms = jnp.mean(o * o, axis=-1, keepdims=True)
        y_ref[:, hl] = (o * lax.rsqrt(ms + NORM_EPS) * nw_ref[0] * _silu(z_ref[:, hl])).astype(BF16)


def _gdn_out(layer, o_f, o_b, o_local, proj, norm_w):
    m = o_f.shape[0]
    tm = min(512, m)
    depth = norm_w.shape[0]
    zcb = (3 * DIFF_WIDTH + 3 * GDN_WIDTH) // GDN_WIDTH
    osp = pl.BlockSpec((tm, GDN_WIDTH), lambda i, l: (i, 0))
    return _layer_call(
        _gdn_out_kernel, (m // tm,),
        [osp, osp, osp, pl.BlockSpec((tm, GDN_WIDTH), lambda i, l: (i, zcb)),
         pl.BlockSpec((1, 1, LANES), lambda i, l: (l[0], 0, 0))],
        osp, jax.ShapeDtypeStruct((m, GDN_WIDTH), BF16), name="gdn_out",
    )(layer, o_f, o_b, o_local, proj, norm_w.reshape(depth, 1, LANES))


def _gated_deltanet(layer, proj, small, conv_w, a_log, dt_bias, norm_w, batch, seq):
    q, k, k_t, v, beta, gc = _gdn_prep(layer, proj, small, conv_w, a_log, dt_bias, batch, seq)
    n_chunks = seq // CHUNK
    nh2 = 2 * N_GDN_HEADS
    gc_rows = gc.reshape(batch, n_chunks, CHUNK, nh2).transpose(0, 1, 3, 2)
    gc_pairs = gc.reshape(batch, seq // LANES, LANES, nh2).transpose(0, 1, 3, 2)
    l_f, l_b = _gdn_build_l(layer, k, beta, gc, gc_rows, batch, seq)
    g0 = batch * N_GDN_HEADS * n_chunks
    t_f = _tri_inverse(l_f.reshape(g0, CHUNK, CHUNK).transpose(1, 2, 0))
    t_b = _tri_inverse(l_b.reshape(g0, CHUNK, CHUNK).transpose(2, 1, 0))
    t_f = t_f.transpose(2, 0, 1).reshape(batch * N_GDN_HEADS, seq, CHUNK)
    t_b = t_b.transpose(2, 1, 0).reshape(batch * N_GDN_HEADS, seq, CHUNK)
    *pre, o_local = _gdn_pre(layer, q, k, k_t, v, beta, gc, gc_rows, gc_pairs, t_f, t_b, batch, seq)
    o_f, o_b = _gdn_state(layer, pre, batch, seq)
    return _gdn_out(layer, o_f, o_b, o_local, proj, norm_w)


def _layer(l, x, h, cos, sin, mod, lam_inits, p, batch, seq):
    m, d = x.shape
    layer = jnp.reshape(l, (1,)).astype(jnp.int32)
    tm = min(1024, seq)
    ident = lambda accs, extras: accs

    w_t = p["w_in_t"]
    (proj,) = _matmul(layer, [h], [w_t], [0], [0], [], ident, [F32], MAIN_COLS, tm, 1024, "proj_main",
                      w_transposed=True)
    (small,) = _matmul(layer, [h], [w_t], [0], [MAIN_COLS], [], ident, [F32], LANES, tm, LANES, "proj_small",
                       w_transposed=True)
    (gates,) = _matmul(layer, [h], [w_t], [0], [MAIN_COLS + SMALL_COLS], [],
                       lambda accs, extras: [jax.nn.sigmoid(accs[0])], [BF16], 2 * d, tm, 1024, "proj_gates",
                       w_transposed=True)

    lam_init = lam_inits[l]
    lv = p["diff_lambda"][l].astype(F32)
    lam = jnp.exp(jnp.sum(lv[0] * lv[1])) - jnp.exp(jnp.sum(lv[2] * lv[3])) + lam_init
    scalars = jnp.stack([lam, 1.0 - lam_init]).astype(F32)
    dq, dk, dv = _qk_prep(layer, proj, cos, sin, p["diff_qn_w"], p["diff_kn_w"])
    y_diff = _diff_attention(layer, scalars, dq, dk, dv, p["diff_subln_w"], batch, seq)

    y_gdn = _gated_deltanet(layer, proj, small, p["gdn_conv_w"], p["gdn_a_log"], p["gdn_dt_bias"],
                            p["gdn_norm_w"], batch, seq)

    x, h = _mixer_out(layer, y_diff, y_gdn, gates, p["w_branch_diff"], p["w_branch_gdn"], p["w_out"], x, mod,
                      p["norm_ffn_w"], seq)
    f = p["ffn_w_down"].shape[1]
    tf = 512
    (act,) = _matmul(layer, [h], [p["ffn_w_up"], p["ffn_w_up"]], [0, 0], [0, f // tf], [],
                     lambda accs, extras: [_silu(accs[0]) * accs[1]], [BF16], f, tm, tf, "ffn_up")
    return tuple(_matmul_residual_norm(layer, act, p["ffn_w_down"], x, mod, 5, p["norm_mix_w"], 1, 0, True, seq,
                                       min(256, seq), "ffn_down"))


def kernel(x, c, positions, ada_w, ada_b, norm_mix_w, norm_ffn_w, w_in, diff_qn_w, diff_kn_w, diff_lambda,
           diff_subln_w, gdn_conv_w, gdn_a_log, gdn_dt_bias, gdn_norm_w, w_branch_diff, w_branch_gdn, w_out,
           ffn_w_up, ffn_w_down):
    batch, seq, d = x.shape
    depth = ada_w.shape[0]
    mod = _ada_modulation(c, ada_w, ada_b)
    cos, sin = _rope_tables(positions)
    lam_inits = jnp.asarray([0.8 - 0.6 * math.exp(-0.3 * i) for i in range(depth)], F32)
    w_in_t = jnp.swapaxes(w_in, 1, 2)
    p = {
        "norm_mix_w": norm_mix_w.reshape(depth, 1, d), "norm_ffn_w": norm_ffn_w.reshape(depth, 1, d),
        "w_in_t": w_in_t,
        "diff_qn_w": diff_qn_w, "diff_kn_w": diff_kn_w, "diff_lambda": diff_lambda, "diff_subln_w": diff_subln_w,
        "gdn_conv_w": gdn_conv_w, "gdn_a_log": gdn_a_log, "gdn_dt_bias": gdn_dt_bias, "gdn_norm_w": gdn_norm_w,
        "w_branch_diff": w_branch_diff.astype(BF16), "w_branch_gdn": w_branch_gdn.astype(BF16),
        "w_out": w_out.astype(BF16), "ffn_w_up": ffn_w_up, "ffn_w_down": ffn_w_down.astype(BF16),
    }
    x0 = x.reshape(batch * seq, d)
    h0 = _norm_mod(jnp.zeros((1,), jnp.int32), x0, p["norm_mix_w"], mod, 1, 0, seq)
    carry = (x0, h0)
    for l in range(depth):
        carry = _layer(jnp.int32(l), carry[0], carry[1], cos, sin, mod, lam_inits, p, batch, seq)
    return carry[0].reshape(batch, seq, d)
```
